```python
import math
import jax, jax.numpy as jnp
from jax import lax
import numpy as np

D_MODEL = 1024
BATCH = 8
SEQ = 4096
DEPTH = 2

CTX_LEN = 256
GRID_W = 64
N_MIXERS = 2
EXPAND = 2
D_INNER = EXPAND * D_MODEL
HG_DK = 128
HG_HEADS = D_INNER // HG_DK
HG_DV = D_INNER // HG_HEADS
HG_CHUNK = 32
HG_IN_COLS = 5 * D_INNER
DA_HEADS = 16
DA_DQK = 64
DA_DV = 2 * DA_DQK
DA_IN_COLS = 4 * D_INNER
Q_BLOCK = 128
ROPE_THETA = 10000.0
EPS = 1e-6
N_HGRN_LAYERS = (DEPTH + N_MIXERS - 1) // N_MIXERS
N_DIFF_LAYERS = DEPTH // N_MIXERS

kernel_name = "hybrid_hgrn2_diffattn_dit_block"

F32 = jnp.float32


def rmsnorm(x, g):
    xf = x.astype(F32)
    y = xf * lax.rsqrt(jnp.mean(xf * xf, axis=-1, keepdims=True) + EPS)
    return (y * g.astype(F32)).astype(x.dtype)


def modulate(x, g, shift, scale):
    return rmsnorm(x, g) * (1 + scale) + shift


def _heads(t, n_heads):
    B, L, _ = t.shape
    return t.reshape(B, L, n_heads, -1).transpose(0, 2, 1, 3)


def gla_scan(q, k, v, logf, s0):
    B, H, L, DK = q.shape
    DV = v.shape[-1]
    n = L // HG_CHUNK

    def to_chunks(t):
        return jnp.moveaxis(t.reshape(B, H, n, HG_CHUNK, t.shape[-1]), 2, 0)

    mask = jnp.tril(jnp.ones((HG_CHUNK, HG_CHUNK), dtype=bool))

    def step(s, inp):
        qc, kc, vc, gc = inp
        b = jnp.cumsum(gc, axis=2)
        o_inter = jnp.einsum('bhck,bhkv->bhcv', qc * jnp.exp(b), s)
        d = b[:, :, :, None, :] - b[:, :, None, :, :]
        decay = jnp.exp(jnp.where(mask[:, :, None], d, -jnp.inf))
        a = jnp.einsum('bhik,bhjk,bhijk->bhij', qc, kc, decay)
        o_intra = jnp.einsum('bhij,bhjv->bhiv', a, vc)
        b_last = b[:, :, -1:, :]
        s_new = jnp.exp(b_last[:, :, 0, :])[..., None] * s + jnp.einsum(
            'bhck,bhcv->bhkv', kc * jnp.exp(b_last - b), vc)
        return s_new, o_inter + o_intra

    s_fin, o = lax.scan(step, s0, (to_chunks(q), to_chunks(k), to_chunks(v), to_chunks(logf)))
    o = jnp.moveaxis(o, 0, 2).reshape(B, H, L, DV)
    return o, s_fin


def hgrn2_project(h, w_in, lb_fwd, lb_bwd):
    q, zf, zb, i, gate = jnp.split(h @ w_in, 5, axis=-1)
    q = jax.nn.silu(q.astype(F32))

    def forget(z, lb):
        z = z.astype(F32)
        logf = jnp.log(lb + (1 - lb) * jax.nn.sigmoid(z))
        k = (1 - lb) * jax.nn.sigmoid(-z)
        return _heads(k, HG_HEADS), _heads(logf, HG_HEADS)

    kf, gf = forget(zf, lb_fwd)
    kb, gb = forget(zb, lb_bwd)
    return _heads(q, HG_HEADS), kf, gf, kb, gb, _heads(i.astype(F32), HG_HEADS), gate


def hgrn2_bidir(q, kf, gf, kb, gb, v, s0f, s0b):
    flip = lambda t: jnp.flip(t, axis=2)
    of, sf = gla_scan(q, kf, v, gf, s0f)
    ob, sb = gla_scan(flip(q), flip(kb), flip(v), flip(gb), s0b)
    return of + flip(ob), sf, sb


def hgrn2_readout(o, gate, norm_g, w_out, dtype):
    B, H, L, DV = o.shape
    o = o * lax.rsqrt(jnp.mean(o * o, axis=-1, keepdims=True) + EPS)
    o = o.transpose(0, 2, 1, 3).reshape(B, L, H * DV) * norm_g.astype(F32)
    return (o * jax.nn.silu(gate.astype(F32))).astype(dtype) @ w_out


def hgrn2_mixer(h_ctx, h_lat, w_in, lb_fwd, lb_bwd, norm_g, w_out, emit_ctx):
    dtype = h_lat.dtype
    B = h_lat.shape[0]
    s0 = jnp.zeros((B, HG_HEADS, HG_DK, HG_DV), F32)
    qc, kfc, gfc, kbc, gbc, vc, gc = hgrn2_project(h_ctx, w_in, lb_fwd, lb_bwd)
    o_ctx, s_cf, s_cb = hgrn2_bidir(qc, kfc, gfc, kbc, gbc, vc, s0, s0)
    ql, kfl, gfl, kbl, gbl, vl, gl = hgrn2_project(h_lat, w_in, lb_fwd, lb_bwd)
    o_lat, _, _ = hgrn2_bidir(ql, kfl, gfl, kbl, gbl, vl, s_cf, s_cb)
    y_lat = hgrn2_readout(o_lat, gl, norm_g, w_out, dtype)
    y_ctx = hgrn2_readout(o_ctx, gc, norm_g, w_out, dtype) if emit_ctx else None
    return y_ctx, y_lat


def axial_rope_tables(row, col, dtype):
    ax = DA_DQK // 2
    inv = 1.0 / (ROPE_THETA ** (jnp.arange(0, ax, 2, dtype=F32) / ax))
    L = row.shape[0]
    ang_r = row.astype(F32)[:, None] * inv
    ang_c = col.astype(F32)[:, None] * inv
    shp = (1, L, 1, 1, ax // 2)
    return (jnp.cos(ang_r).reshape(shp).astype(dtype), jnp.sin(ang_r).reshape(shp).astype(dtype),
            jnp.cos(ang_c).reshape(shp).astype(dtype), jnp.sin(ang_c).reshape(shp).astype(dtype))


def _rot(x, cos, sin):
    x1, x2 = jnp.split(x, 2, axis=-1)
    return jnp.concatenate([x1 * cos - x2 * sin, x2 * cos + x1 * sin], axis=-1)


def apply_axial_rope(x, cos_r, sin_r, cos_c, sin_c):
    xr, xc = jnp.split(x, 2, axis=-1)
    return jnp.concatenate([_rot(xr, cos_r, sin_r), _rot(xc, cos_c, sin_c)], axis=-1)


def diff_softmax(q, k, v, lam):
    s = jnp.einsum('bqhpd,bkhpd->bhpqk', q, k).astype(F32) * (DA_DQK ** -0.5)
    p = jax.nn.softmax(s, axis=-1)
    a = p[:, :, 0] - lam * p[:, :, 1]
    return jnp.einsum('bhqk,bkhv->bqhv', a, v.astype(F32))


def diff_attn_mixer(h_ctx, h_lat, w_in, lq1, lk1, lq2, lk2, subln_g, w_out, lambda_init, rope, emit_ctx):
    dtype = h_lat.dtype
    lam = (jnp.exp(jnp.sum(lq1.astype(F32) * lk1.astype(F32)))
           - jnp.exp(jnp.sum(lq2.astype(F32) * lk2.astype(F32))) + lambda_init)

    def project(h):
        B, L, _ = h.shape
        q, k, v, gate = jnp.split(h @ w_in, 4, axis=-1)
        return (q.reshape(B, L, DA_HEADS, 2, DA_DQK), k.reshape(B, L, DA_HEADS, 2, DA_DQK),
                v.reshape(B, L, DA_HEADS, DA_DV), gate)

    def readout(o, gate):
        B, L = o.shape[:2]
        o = o * lax.rsqrt(jnp.mean(o * o, axis=-1, keepdims=True) + EPS) * subln_g.astype(F32)
        o = (o * (1.0 - lambda_init)).reshape(B, L, D_INNER)
        return (o * jax.nn.silu(gate.astype(F32))).astype(dtype) @ w_out

    qc, kc, vc, gc = project(h_ctx)
    ql, kl, vl, gl = project(h_lat)
    ql = apply_axial_rope(ql, *rope)
    kl = apply_axial_rope(kl, *rope)
    keys = jnp.concatenate([kc, kl], axis=1)
    vals = jnp.concatenate([vc, vl], axis=1)

    B, L = ql.shape[:2]
    nb = L // Q_BLOCK
    qb = jnp.moveaxis(ql.reshape(B, nb, Q_BLOCK, DA_HEADS, 2, DA_DQK), 1, 0)
    o_lat = lax.map(lambda qq: diff_softmax(qq, keys, vals, lam), qb)
    o_lat = jnp.moveaxis(o_lat, 0, 1).reshape(B, L, DA_HEADS, DA_DV)
    y_lat = readout(o_lat, gl)
    y_ctx = readout(diff_softmax(qc, kc, vc, lam), gc) if emit_ctx else None
    return y_ctx, y_lat


def setup_inputs(seed: int = 0) -> dict:
    key = jax.random.key(seed)
    ks = jax.random.split(key, 20)

    def nrm(k, shape, s):
        return jax.random.normal(k, shape, F32) * s

    return {
        "x": nrm(ks[0], (BATCH, SEQ, D_MODEL), 1.0),
        "c": nrm(ks[1], (BATCH, D_MODEL), 1.0),
        "ctx": nrm(ks[2], (BATCH, CTX_LEN, D_MODEL), 1.0),
        "c_ctx": nrm(ks[3], (D_MODEL,), 1.0),
        "w_ada": nrm(ks[4], (DEPTH, D_MODEL, 3 * D_MODEL), 0.5 * D_MODEL ** -0.5),
        "b_ada": nrm(ks[5], (DEPTH, 3 * D_MODEL), 0.01),
        "norm_g": 1.0 + nrm(ks[6], (DEPTH, D_MODEL), 0.02),
        "hg_w_in": nrm(ks[7], (N_HGRN_LAYERS, D_MODEL, HG_IN_COLS), D_MODEL ** -0.5),
        "hg_lb_logits": nrm(ks[8], (N_HGRN_LAYERS + 1, 2, D_INNER), 0.5),
        "hg_norm_g": 1.0 + nrm(ks[9], (N_HGRN_LAYERS, D_INNER), 0.02),
        "hg_w_out": nrm(ks[10], (N_HGRN_LAYERS, D_INNER, D_MODEL), D_INNER ** -0.5),
        "da_w_in": nrm(ks[11], (N_DIFF_LAYERS, D_MODEL, DA_IN_COLS), D_MODEL ** -0.5),
        "da_lam_q1": nrm(ks[12], (N_DIFF_LAYERS, DA_DQK), 0.1),
        "da_lam_k1": nrm(ks[13], (N_DIFF_LAYERS, DA_DQK), 0.1),
        "da_lam_q2": nrm(ks[14], (N_DIFF_LAYERS, DA_DQK), 0.1),
        "da_lam_k2": nrm(ks[15], (N_DIFF_LAYERS, DA_DQK), 0.1),
        "da_subln_g": 1.0 + nrm(ks[16], (N_DIFF_LAYERS, DA_DV), 0.02),
        "da_w_out": nrm(ks[17], (N_DIFF_LAYERS, D_INNER, D_MODEL), D_INNER ** -0.5),
        "final_g": 1.0 + nrm(ks[18], (D_MODEL,), 0.02),
    }


def reference(x, c, ctx, c_ctx, w_ada, b_ada, norm_g, hg_w_in, hg_lb_logits, hg_norm_g, hg_w_out,
              da_w_in, da_lam_q1, da_lam_k1, da_lam_q2, da_lam_k2, da_subln_g, da_w_out, final_g):
    n_lat = x.shape[1]
    rows = n_lat // GRID_W
    row = jnp.broadcast_to(jnp.arange(rows, dtype=jnp.int32)[:, None], (rows, GRID_W)).reshape(-1)
    col = jnp.broadcast_to(jnp.arange(GRID_W, dtype=jnp.int32)[None, :], (rows, GRID_W)).reshape(-1)
    rope = axial_rope_tables(row, col, x.dtype)

    lb_all = jnp.cumsum(jax.nn.softmax(hg_lb_logits.astype(F32), axis=0), axis=0)

    sc = jax.nn.silu(c)
    scc = jax.nn.silu(c_ctx)
    x_lat, x_ctx = x, ctx
    for i in range(DEPTH):
        emit_ctx = i < DEPTH - 1
        shift, scale, gate = jnp.split(sc @ w_ada[i] + b_ada[i], 3, axis=-1)
        shift_c, scale_c, gate_c = jnp.split(scc @ w_ada[i] + b_ada[i], 3, axis=-1)
        h_lat = modulate(x_lat, norm_g[i], shift[:, None, :], scale[:, None, :])
        h_ctx = modulate(x_ctx, norm_g[i], shift_c, scale_c)
        j = i // N_MIXERS
        if i % N_MIXERS == 0:
            y_ctx, y_lat = hgrn2_mixer(h_ctx, h_lat, hg_w_in[j], lb_all[j, 0], lb_all[j, 1],
                                       hg_norm_g[j], hg_w_out[j], emit_ctx)
        else:
            lambda_init = 0.8 - 0.6 * math.exp(-0.3 * i)
            y_ctx, y_lat = diff_attn_mixer(h_ctx, h_lat, da_w_in[j], da_lam_q1[j], da_lam_k1[j],
                                           da_lam_q2[j], da_lam_k2[j], da_subln_g[j], da_w_out[j],
                                           lambda_init, rope, emit_ctx)
        x_lat = x_lat + gate[:, None, :] * y_lat
        if emit_ctx:
            x_ctx = x_ctx + gate_c * y_ctx
    return rmsnorm(x_lat, final_g)
```

```python
import functools
import math

import numpy as np
import jax
import jax.numpy as jnp
from jax import lax
from jax.experimental import pallas as pl
from jax.experimental.pallas import tpu as pltpu

F32 = jnp.float32
BF16 = jnp.bfloat16

EPS = 1e-6
GRID_W = 64
ROPE_THETA = 10000.0
HEAD_W = 128
DA_DQK = 64
TOK_TILE = 256
SCAN_CHUNK = 32
SCAN_HEADS = 4
Q_TILE = 256
EXP_CLAMP = 80.0
LOG2E = 1.4426950408889634
VMEM_LIMIT = 56 * 1024 * 1024

NT = (((1,), (1,)), ((), ()))
TN = (((0,), (0,)), ((), ()))


def _sigmoid(x):
    return 1.0 / (1.0 + jnp.exp(-x))


def _silu(x):
    return x * _sigmoid(x)


def _params(sem):
    return pltpu.CompilerParams(dimension_semantics=sem, vmem_limit_bytes=VMEM_LIMIT)


def _whole_vmem():
    return pl.BlockSpec(memory_space=pltpu.VMEM)


def _adaln_kernel(c_ref, w_ref, b_ref, o_ref):
    s = _silu(c_ref[...]).astype(BF16)
    o_ref[0] = jnp.dot(s, w_ref[0], preferred_element_type=F32) + b_ref[0]


def _adaln(cc, w_ada, b_ada):
    depth, d, d3 = w_ada.shape
    rows = cc.shape[0]
    return pl.pallas_call(
        _adaln_kernel,
        grid=(depth,),
        in_specs=[pl.BlockSpec((rows, d), lambda i: (0, 0)),
                  pl.BlockSpec((1, d, d3), lambda i: (i, 0, 0)),
                  pl.BlockSpec((1, 1, d3), lambda i: (i, 0, 0))],
        out_specs=pl.BlockSpec((1, rows, d3), lambda i: (i, 0, 0)),
        out_shape=jax.ShapeDtypeStruct((depth, rows, d3), F32),
        compiler_params=_params(("arbitrary",)),
        name="adaln",
    )(cc, w_ada.astype(BF16), b_ada.reshape(depth, 1, d3))


def _modulated_norm(x, g, shift, scale):
    y = x * lax.rsqrt(jnp.mean(x * x, axis=-1, keepdims=True) + EPS)
    return (y * g) * (1.0 + scale) + shift


def _hg_proj_kernel(ctx_ref, x_ref, mod_ref, ng_ref, w_ref, lbl_ref,
                    q_ref, gf_ref, gb_ref, v_ref, gate_ref):
    t = pl.program_id(1)
    x = jnp.where(t == 0, ctx_ref[0], x_ref[0])
    h = _modulated_norm(x, ng_ref[...], mod_ref[0, 0, 0:1, :], mod_ref[0, 0, 1:2, :]).astype(BF16)
    di = q_ref.shape[-1]

    def seg(j):
        return jnp.dot(h, w_ref[:, j * di:(j + 1) * di], preferred_element_type=F32)

    def log_forget(z, d):
        l0 = lbl_ref[0, d:d + 1, :]
        l1 = lbl_ref[1, d:d + 1, :]
        m = jnp.maximum(l0, l1)
        e0 = jnp.exp(l0 - m)
        lb = e0 / (e0 + jnp.exp(l1 - m))
        return jnp.log(lb + (1.0 - lb) * _sigmoid(z))

    q_ref[0] = _silu(seg(0)).astype(BF16)
    gf_ref[0] = log_forget(seg(1), 0)
    gb_ref[0] = log_forget(seg(2), 1)
    v_ref[0] = seg(3).astype(BF16)
    gate_ref[0] = _silu(seg(4)).astype(BF16)


def _hg_proj(ctx, x, mod, norm_g, w_in, lb_logits):
    b, l, d = x.shape
    di = w_in.shape[1] // 5
    nt = l // TOK_TILE + 1
    tot = l + TOK_TILE
    tok = lambda dt: jax.ShapeDtypeStruct((b, tot, di), dt)
    out_blk = pl.BlockSpec((1, TOK_TILE, di), lambda i, t: (i, t, 0))
    return pl.pallas_call(
        _hg_proj_kernel,
        grid=(b, nt),
        in_specs=[pl.BlockSpec((1, TOK_TILE, d), lambda i, t: (i, 0, 0)),
                  pl.BlockSpec((1, TOK_TILE, d), lambda i, t: (i, jnp.maximum(t - 1, 0), 0)),
                  pl.BlockSpec((1, 1, 2, d), lambda i, t: (i, jnp.minimum(t, 1), 0, 0)),
                  pl.BlockSpec((1, d), lambda i, t: (0, 0)),
                  _whole_vmem(),
                  pl.BlockSpec((2, 2, di), lambda i, t: (0, 0, 0))],
        out_specs=[out_blk] * 5,
        out_shape=[tok(BF16), tok(F32), tok(F32), tok(BF16), tok(BF16)],
        compiler_params=_params(("arbitrary", "arbitrary")),
        name="hg_proj",
    )(ctx, x, mod, norm_g, w_in, lb_logits)


def _split3(g):
    hi = g.astype(BF16)
    r1 = g - hi.astype(F32)
    mid = r1.astype(BF16)
    lo = (r1 - mid.astype(F32)).astype(BF16)
    return hi, mid, lo


def _hg_scan_kernel(qf_ref, gf_ref, vf_ref, qb_ref, gb_ref, vb_ref, of_ref, ob_ref,
                    st_ref, bf_ref, bb_ref):
    t = pl.program_id(2)
    c = SCAN_CHUNK
    n_chunk = TOK_TILE // c

    @pl.when(t == 0)
    def _():
        st_ref[...] = jnp.zeros_like(st_ref)

    row = lax.broadcasted_iota(jnp.int32, (TOK_TILE, TOK_TILE), 0)
    col = lax.broadcasted_iota(jnp.int32, (TOK_TILE, TOK_TILE), 1)
    same = (row // c) == (col // c)
    tri_f = jnp.where(same & (col <= row), 1.0, 0.0).astype(BF16)
    tri_b = jnp.where(same & (col >= row), 1.0, 0.0).astype(BF16)

    def cumsum(tri, g):
        hi, mid, lo = _split3(g)
        return (jnp.dot(tri, hi, preferred_element_type=F32)
                + jnp.dot(tri, mid, preferred_element_type=F32)
                + jnp.dot(tri, lo, preferred_element_type=F32))

    bf_ref[...] = cumsum(tri_f, gf_ref[0])
    bb_ref[...] = cumsum(tri_b, gb_ref[0])

    ri = lax.broadcasted_iota(jnp.int32, (c, c), 0)
    ci = lax.broadcasted_iota(jnp.int32, (c, c), 1)
    keep_f = ci <= ri
    keep_b = ci >= ri

    def chunk(direction, base, q_ref, g_ref, v_ref, b_ref, o_ref, keep, mid_row, last_row):
        rows = pl.ds(base, c)
        for hh in range(SCAN_HEADS):
            lanes = slice(hh * HEAD_W, (hh + 1) * HEAD_W)
            q = q_ref[0, rows, lanes].astype(F32)
            g = g_ref[0, rows, lanes]
            v = v_ref[0, rows, lanes]
            bcum = b_ref[rows, lanes]
            r = bcum[mid_row:mid_row + 1, :]
            b_last = bcum[last_row:last_row + 1, :]
            d = jnp.clip(bcum - r, -EXP_CLAMP, EXP_CLAMP)
            e = jnp.exp(d)
            q_mid = q * e
            k_mid = (1.0 - jnp.exp(g)) * jnp.exp(-d)
            q_dec = (q_mid * jnp.exp(r)).astype(BF16)
            k_dec = (k_mid * jnp.exp(b_last - r)).astype(BF16)
            a = lax.dot_general(q_mid.astype(BF16), k_mid.astype(BF16), NT, preferred_element_type=F32)
            a = jnp.where(keep, a, 0.0).astype(BF16)
            st = st_ref[direction, hh]
            o = (lax.dot_general(q_dec, st.astype(BF16), NT, preferred_element_type=F32)
                 + jnp.dot(a, v, preferred_element_type=F32))
            o_ref[0, rows, lanes] = o
            st_ref[direction, hh] = jnp.exp(b_last) * st + lax.dot_general(
                v, k_dec, TN, preferred_element_type=F32)

    def body(i, carry):
        base_f = pl.multiple_of(i * c, c)
        base_b = pl.multiple_of((n_chunk - 1 - i) * c, c)
        chunk(0, base_f, qf_ref, gf_ref, vf_ref, bf_ref, of_ref, keep_f, c // 2 - 1, c - 1)
        chunk(1, base_b, qb_ref, gb_ref, vb_ref, bb_ref, ob_ref, keep_b, c // 2, 0)
        return carry

    lax.fori_loop(0, n_chunk, body, 0)


def _hg_scan(q, gf, gb, v):
    b, tot, di = q.shape
    nt = tot // TOK_TILE
    w = SCAN_HEADS * HEAD_W
    fwd = pl.BlockSpec((1, TOK_TILE, w), lambda i, h, t: (i, t, h))
    bwd = pl.BlockSpec((1, TOK_TILE, w), lambda i, h, t: (i, jnp.where(t == 0, 0, nt - t), h))
    out = jax.ShapeDtypeStruct((b, tot, di), F32)
    return pl.pallas_call(
        _hg_scan_kernel,
        grid=(b, di // w, nt),
        in_specs=[fwd, fwd, fwd, bwd, bwd, bwd],
        out_specs=[fwd, bwd],
        out_shape=[out, out],
        scratch_shapes=[pltpu.VMEM((2, SCAN_HEADS, HEAD_W, HEAD_W), F32),
                        pltpu.VMEM((TOK_TILE, w), F32),
                        pltpu.VMEM((TOK_TILE, w), F32)],
        compiler_params=_params(("arbitrary", "arbitrary", "arbitrary")),
        name="hg_scan",
    )(q, gf, v, q, gb, v)


def _hg_out_kernel(of_ref, ob_ref, gate_ref, ctx_ref, x_ref, gm_ref, hng_ref, w_ref, o_ref, y_ref):
    t = pl.program_id(1)
    n_head = of_ref.shape[-1] // HEAD_W
    for hh in range(n_head):
        lanes = slice(hh * HEAD_W, (hh + 1) * HEAD_W)
        o = of_ref[0, :, lanes] + ob_ref[0, :, lanes]
        o = o * lax.rsqrt(jnp.mean(o * o, axis=-1, keepdims=True) + EPS) * hng_ref[:, lanes]
        y_ref[:, lanes] = (o * gate_ref[0, :, lanes].astype(F32)).astype(BF16)
    y = jnp.dot(y_ref[...], w_ref[...], preferred_element_type=F32)
    x = jnp.where(t == 0, ctx_ref[0], x_ref[0])
    o_ref[0] = x + gm_ref[0, 0] * y


def _hg_out(of, ob, gate, ctx, x, gate_mod, hg_norm_g, w_out):
    b, tot, di = of.shape
    d = x.shape[-1]
    nt = tot // TOK_TILE
    tokw = pl.BlockSpec((1, TOK_TILE, di), lambda i, t: (i, t, 0))
    return pl.pallas_call(
        _hg_out_kernel,
        grid=(b, nt),
        in_specs=[tokw, tokw, tokw,
                  pl.BlockSpec((1, TOK_TILE, d), lambda i, t: (i, 0, 0)),
                  pl.BlockSpec((1, TOK_TILE, d), lambda i, t: (i, jnp.maximum(t - 1, 0), 0)),
                  pl.BlockSpec((1, 1, 1, d), lambda i, t: (i, jnp.minimum(t, 1), 0, 0)),
                  pl.BlockSpec((1, di), lambda i, t: (0, 0)),
                  _whole_vmem()],
        out_specs=pl.BlockSpec((1, TOK_TILE, d), lambda i, t: (i, t, 0)),
        out_shape=jax.ShapeDtypeStruct((b, tot, d), F32),
        scratch_shapes=[pltpu.VMEM((TOK_TILE, di), BF16)],
        compiler_params=_params(("arbitrary", "arbitrary")),
        name="hg_out",
    )(of, ob, gate, ctx, x, gate_mod, hg_norm_g, w_out)


def _rope_lane_tables(l_lat):
    ax = DA_DQK // 2
    inv = 1.0 / (ROPE_THETA ** (np.arange(0, ax, 2, dtype=np.float64) / ax))
    pos = np.arange(l_lat)
    ang_r = (pos // GRID_W)[:, None] * inv
    ang_c = (pos % GRID_W)[:, None] * inv
    zero = np.zeros_like(ang_r)
    cos64 = np.concatenate([np.cos(ang_r), np.cos(ang_r), np.cos(ang_c), np.cos(ang_c)], axis=1)
    up64 = np.concatenate([-np.sin(ang_r), zero, -np.sin(ang_c), zero], axis=1)
    dn64 = np.concatenate([zero, np.sin(ang_r), zero, np.sin(ang_c)], axis=1)
    lat = [np.concatenate([t, t], axis=1) for t in (cos64, up64, dn64)]
    ident = [np.ones((TOK_TILE, HEAD_W)), np.zeros((TOK_TILE, HEAD_W)), np.zeros((TOK_TILE, HEAD_W))]
    return [np.concatenate([i, t], axis=0).astype(np.float32) for i, t in zip(ident, lat)], (cos64, up64, dn64)


def _da_proj_kernel(x_ref, mod_ref, ng_ref, wq_ref, wk_ref, wv_ref, wg_ref,
                    kc_ref, ku_ref, kd_ref, qc_ref, qs_ref,
                    k_ref, vt_ref, qt_ref, gate_ref):
    t = pl.program_id(1)
    h = _modulated_norm(x_ref[0], ng_ref[...], mod_ref[0, 0, 0:1, :], mod_ref[0, 0, 1:2, :]).astype(BF16)
    n_head = k_ref.shape[-1] // HEAD_W

    k = jnp.dot(h, wk_ref[...], preferred_element_type=F32)
    cos, s_up, s_dn = kc_ref[...], ku_ref[...], kd_ref[...]
    for hh in range(n_head):
        lanes = slice(hh * HEAD_W, (hh + 1) * HEAD_W)
        kh = k[:, lanes]
        rot = (kh * cos + pltpu.roll(kh, HEAD_W - 16, axis=1) * s_up
               + pltpu.roll(kh, 16, axis=1) * s_dn)
        k_ref[0, :, lanes] = rot.astype(BF16)

    vt_ref[0] = lax.dot_general(wv_ref[...], h, NT, preferred_element_type=F32).astype(BF16)

    @pl.when(t > 0)
    def _():
        qt = lax.dot_general(wq_ref[...], h, NT, preferred_element_type=F32)
        qc, qs = qc_ref[...], qs_ref[...]
        for grp in range(qt.shape[0] // DA_DQK):
            x = qt[grp * DA_DQK:(grp + 1) * DA_DQK]
            partner = jnp.concatenate([x[16:32], x[0:16], x[48:64], x[32:48]], axis=0)
            qt_ref[0, grp * DA_DQK:(grp + 1) * DA_DQK, :] = (x * qc + partner * qs).astype(BF16)
        gate_ref[0] = _silu(jnp.dot(h, wg_ref[...], preferred_element_type=F32)).astype(BF16)


def _da_proj(x1, mod, norm_g, w_in, l_lat):
    b, tot, d = x1.shape
    di = w_in.shape[1] // 4
    nt = tot // TOK_TILE
    wq, wk, wv, wg = (w_in[:, j * di:(j + 1) * di] for j in range(4))
    ktabs, (cos64, up64, dn64) = _rope_lane_tables(l_lat)
    q_scale = DA_DQK ** -0.5 * LOG2E
    qc = (cos64.T * q_scale).astype(np.float32)
    qs = ((up64 + dn64).T * q_scale).astype(np.float32)
    ktab = pl.BlockSpec((TOK_TILE, HEAD_W), lambda i, t: (t, 0))
    qtab = pl.BlockSpec((DA_DQK, TOK_TILE), lambda i, t: (0, jnp.maximum(t - 1, 0)))
    return pl.pallas_call(
        _da_proj_kernel,
        grid=(b, nt),
        in_specs=[pl.BlockSpec((1, TOK_TILE, d), lambda i, t: (i, t, 0)),
                  pl.BlockSpec((1, 1, 2, d), lambda i, t: (i, jnp.minimum(t, 1), 0, 0)),
                  pl.BlockSpec((1, d), lambda i, t: (0, 0)),
                  _whole_vmem(), _whole_vmem(), _whole_vmem(), _whole_vmem(),
                  ktab, ktab, ktab, qtab, qtab],
        out_specs=[pl.BlockSpec((1, TOK_TILE, di), lambda i, t: (i, t, 0)),
                   pl.BlockSpec((1, di, TOK_TILE), lambda i, t: (i, 0, t)),
                   pl.BlockSpec((1, di, TOK_TILE), lambda i, t: (i, 0, jnp.maximum(t - 1, 0))),
                   pl.BlockSpec((1, TOK_TILE, di), lambda i, t: (i, jnp.maximum(t - 1, 0), 0))],
        out_shape=[jax.ShapeDtypeStruct((b, tot, di), BF16),
                   jax.ShapeDtypeStruct((b, di, tot), BF16),
                   jax.ShapeDtypeStruct((b, di, l_lat), BF16),
                   jax.ShapeDtypeStruct((b, l_lat, di), BF16)],
        compiler_params=_params(("arbitrary", "arbitrary")),
        name="da_proj",
    )(x1, mod, norm_g, wq.T.astype(BF16), wk.astype(BF16), wv.T.astype(BF16), wg.astype(BF16),
      *[jnp.asarray(a) for a in ktabs], jnp.asarray(qc), jnp.asarray(qs))


def _da_attn_kernel(k_ref, vt_ref, qt_ref, lam_ref, sg_ref, o_ref, *, lambda_init):
    lq1, lk1, lq2, lk2 = (lam_ref[i:i + 1, :] for i in range(4))
    lam = (jnp.exp(jnp.sum(lq1 * lk1, axis=-1, keepdims=True))
           - jnp.exp(jnp.sum(lq2 * lk2, axis=-1, keepdims=True)) + lambda_init)
    k = k_ref[0]
    vt = vt_ref[0]
    out_gain = sg_ref[...] * (1.0 - lambda_init)
    half0 = lax.broadcasted_iota(jnp.int32, (HEAD_W, Q_TILE), 0) < DA_DQK
    n_q = qt_ref.shape[-1] // Q_TILE

    def body(i, carry):
        start = pl.multiple_of(i * Q_TILE, Q_TILE)
        qt = qt_ref[0, :, pl.ds(start, Q_TILE)]
        zero = jnp.zeros_like(qt)
        rhs = jnp.concatenate([jnp.where(half0, qt, zero), jnp.where(half0, zero, qt)], axis=1)
        s = jnp.dot(k, rhs, preferred_element_type=F32)
        p = jnp.exp2(s - jnp.max(s, axis=0, keepdims=True))
        inv = 1.0 / jnp.sum(p, axis=0, keepdims=True)
        a = (p[:, :Q_TILE] * inv[:, :Q_TILE] - p[:, Q_TILE:] * (lam * inv[:, Q_TILE:])).astype(BF16)
        ot = jnp.dot(vt, a, preferred_element_type=F32)
        ot = ot * lax.rsqrt(jnp.mean(ot * ot, axis=0, keepdims=True) + EPS) * out_gain
        o_ref[0, pl.ds(start, Q_TILE), :] = ot.T
        return carry

    lax.fori_loop(0, n_q, body, 0)


def _da_attn(k, vt, qt, lam_vecs, subln_g, lambda_init):
    b, tot, di = k.shape
    l_lat = qt.shape[-1]
    n_head = di // HEAD_W
    return pl.pallas_call(
        functools.partial(_da_attn_kernel, lambda_init=lambda_init),
        grid=(b, n_head),
        in_specs=[pl.BlockSpec((1, tot, HEAD_W), lambda i, h: (i, 0, h)),
                  pl.BlockSpec((1, HEAD_W, tot), lambda i, h: (i, h, 0)),
                  pl.BlockSpec((1, HEAD_W, l_lat), lambda i, h: (i, h, 0)),
                  pl.BlockSpec((4, DA_DQK), lambda i, h: (0, 0)),
                  pl.BlockSpec((HEAD_W, 1), lambda i, h: (0, 0))],
        out_specs=pl.BlockSpec((1, l_lat, HEAD_W), lambda i, h: (i, 0, h)),
        out_shape=jax.ShapeDtypeStruct((b, l_lat, di), F32),
        compiler_params=_params(("arbitrary", "arbitrary")),
        name="da_attn",
    )(k, vt, qt, lam_vecs, subln_g.reshape(HEAD_W, 1))


def _da_out_kernel(o_ref, gate_ref, x_ref, gm_ref, fg_ref, w_ref, out_ref):
    y = (o_ref[0] * gate_ref[0].astype(F32)).astype(BF16)
    x = x_ref[0] + gm_ref[0] * jnp.dot(y, w_ref[...], preferred_element_type=F32)
    out_ref[0] = x * lax.rsqrt(jnp.mean(x * x, axis=-1, keepdims=True) + EPS) * fg_ref[...]


def _da_out(o, gate, x1, gate_mod, final_g, w_out):
    b, l_lat, di = o.shape
    d = x1.shape[-1]
    tokw = pl.BlockSpec((1, TOK_TILE, di), lambda i, t: (i, t, 0))
    return pl.pallas_call(
        _da_out_kernel,
        grid=(b, l_lat // TOK_TILE),
        in_specs=[tokw, tokw,
                  pl.BlockSpec((1, TOK_TILE, d), lambda i, t: (i, t + 1, 0)),
                  pl.BlockSpec((1, 1, d), lambda i, t: (i, 0, 0)),
                  pl.BlockSpec((1, d), lambda i, t: (0, 0)),
                  _whole_vmem()],
        out_specs=pl.BlockSpec((1, TOK_TILE, d), lambda i, t: (i, t, 0)),
        out_shape=jax.ShapeDtypeStruct((b, l_lat, d), F32),
        compiler_params=_params(("arbitrary", "arbitrary")),
        name="da_out",
    )(o, gate, x1, gate_mod, final_g, w_out)


def kernel(x, c, ctx, c_ctx, w_ada, b_ada, norm_g, hg_w_in, hg_lb_logits, hg_norm_g, hg_w_out,
           da_w_in, da_lam_q1, da_lam_k1, da_lam_q2, da_lam_k2, da_subln_g, da_w_out, final_g):
    b, l_lat, d = x.shape
    assert ctx.shape[1] == TOK_TILE and l_lat % TOK_TILE == 0 and l_lat % GRID_W == 0
    assert w_ada.shape[0] == 2 and hg_w_in.shape[0] == 1 and da_w_in.shape[0] == 1
    assert hg_lb_logits.shape[0] == 2

    rows = -(-(b + 1) // 8) * 8
    cc = jnp.concatenate([c, c_ctx[None], jnp.zeros((rows - b - 1, d), F32)], axis=0)
    ada = _adaln(cc, w_ada, b_ada)

    def mods(layer):
        m = ada[layer].reshape(rows, 3, d)
        pair = jnp.stack([jnp.broadcast_to(m[b], (b, 3, d)), m[:b]], axis=1)
        return pair[:, :, 0:2], pair[:, :, 2:3]

    mod0, gmod0 = mods(0)
    mod1, gmod1 = mods(1)

    q, gf, gb, v, gate0 = _hg_proj(ctx, x, mod0, norm_g[0:1], hg_w_in[0].astype(BF16), hg_lb_logits)
    of, ob = _hg_scan(q, gf, gb, v)
    x1 = _hg_out(of, ob, gate0, ctx, x, gmod0, hg_norm_g, hg_w_out[0].astype(BF16))

    lambda_init = 0.8 - 0.6 * math.exp(-0.3 * 1)
    k, vt, qt, gate1 = _da_proj(x1, mod1, norm_g[1:2], da_w_in[0], l_lat)
    lam_vecs = jnp.concatenate([da_lam_q1, da_lam_k1, da_lam_q2, da_lam_k2], axis=0)
    o = _da_attn(k, vt, qt, lam_vecs, da_subln_g[0], lambda_init)
    return _da_out(o, gate1, x1, gmod1[:, 1], final_g[None], da_w_out[0].astype(BF16))
```

```python
import functools
import math

import numpy as np
import jax
import jax.numpy as jnp
from jax import lax
from jax.experimental import pallas as pl
from jax.experimental.pallas import tpu as pltpu

F32 = jnp.float32
BF16 = jnp.bfloat16

EPS = 1e-6
GRID_W = 64
ROPE_THETA = 10000.0
HEAD_W = 128
DA_DQK = 64
TOK_TILE = 256
SCAN_CHUNK = 32
SCAN_HEADS = 4
Q_TILE = 256
KEY_CHUNK = 256
BOUND_SLACK = 1.01
MIN_COLUMN_SUM = 2.0 ** -64
EXP_CLAMP = 80.0
LOG2E = 1.4426950408889634
VMEM_LIMIT = 56 * 1024 * 1024

NT = (((1,), (1,)), ((), ()))
TN = (((0,), (0,)), ((), ()))


def _sigmoid(x):
    return 1.0 / (1.0 + jnp.exp(-x))


def _silu(x):
    return x * _sigmoid(x)


def _params(sem):
    return pltpu.CompilerParams(dimension_semantics=sem, vmem_limit_bytes=VMEM_LIMIT)


def _whole_vmem():
    return pl.BlockSpec(memory_space=pltpu.VMEM)


def _adaln_kernel(c_ref, w_ref, b_ref, o_ref):
    s = _silu(c_ref[...]).astype(BF16)
    o_ref[0] = jnp.dot(s, w_ref[0], preferred_element_type=F32) + b_ref[0]


def _adaln(cc, w_ada, b_ada):
    depth, d, d3 = w_ada.shape
    rows = cc.shape[0]
    return pl.pallas_call(
        _adaln_kernel,
        grid=(depth,),
        in_specs=[pl.BlockSpec((rows, d), lambda i: (0, 0)),
                  pl.BlockSpec((1, d, d3), lambda i: (i, 0, 0)),
                  pl.BlockSpec((1, 1, d3), lambda i: (i, 0, 0))],
        out_specs=pl.BlockSpec((1, rows, d3), lambda i: (i, 0, 0)),
        out_shape=jax.ShapeDtypeStruct((depth, rows, d3), F32),
        compiler_params=_params(("arbitrary",)),
        name="adaln",
    )(cc, w_ada.astype(BF16), b_ada.reshape(depth, 1, d3))


def _modulated_norm(x, g, shift, scale):
    y = x * lax.rsqrt(jnp.mean(x * x, axis=-1, keepdims=True) + EPS)
    return (y * g) * (1.0 + scale) + shift


def _hg_proj_kernel(ctx_ref, x_ref, mod_ref, ng_ref, w_ref, lbl_ref,
                    q_ref, gf_ref, gb_ref, v_ref, gate_ref):
    t = pl.program_id(1)
    x = jnp.where(t == 0, ctx_ref[0], x_ref[0])
    h = _modulated_norm(x, ng_ref[...], mod_ref[0, 0, 0:1, :], mod_ref[0, 0, 1:2, :]).astype(BF16)
    di = q_ref.shape[-1]

    def seg(j):
        return jnp.dot(h, w_ref[:, j * di:(j + 1) * di], preferred_element_type=F32)

    def log_forget(z, d):
        l0 = lbl_ref[0, d:d + 1, :]
        l1 = lbl_ref[1, d:d + 1, :]
        m = jnp.maximum(l0, l1)
        e0 = jnp.exp(l0 - m)
        lb = e0 / (e0 + jnp.exp(l1 - m))
        return jnp.log(lb + (1.0 - lb) * _sigmoid(z))

    q_ref[0] = _silu(seg(0)).astype(BF16)
    gf_ref[0] = log_forget(seg(1), 0)
    gb_ref[0] = log_forget(seg(2), 1)
    v_ref[0] = seg(3).astype(BF16)
    gate_ref[0] = _silu(seg(4)).astype(BF16)


def _hg_proj(ctx, x, mod, norm_g, w_in, lb_logits):
    b, l, d = x.shape
    di = w_in.shape[1] // 5
    nt = l // TOK_TILE + 1
    tot = l + TOK_TILE
    tok = lambda dt: jax.ShapeDtypeStruct((b, tot, di), dt)
    out_blk = pl.BlockSpec((1, TOK_TILE, di), lambda i, t: (i, t, 0))
    return pl.pallas_call(
        _hg_proj_kernel,
        grid=(b, nt),
        in_specs=[pl.BlockSpec((1, TOK_TILE, d), lambda i, t: (i, 0, 0)),
                  pl.BlockSpec((1, TOK_TILE, d), lambda i, t: (i, jnp.maximum(t - 1, 0), 0)),
                  pl.BlockSpec((1, 1, 2, d), lambda i, t: (i, jnp.minimum(t, 1), 0, 0)),
                  pl.BlockSpec((1, d), lambda i, t: (0, 0)),
                  _whole_vmem(),
                  pl.BlockSpec((2, 2, di), lambda i, t: (0, 0, 0))],
        out_specs=[out_blk] * 5,
        out_shape=[tok(BF16), tok(F32), tok(F32), tok(BF16), tok(BF16)],
        compiler_params=_params(("arbitrary", "arbitrary")),
        name="hg_proj",
    )(ctx, x, mod, norm_g, w_in, lb_logits)


def _split3(g):
    hi = g.astype(BF16)
    r1 = g - hi.astype(F32)
    mid = r1.astype(BF16)
    lo = (r1 - mid.astype(F32)).astype(BF16)
    return hi, mid, lo


def _hg_scan_kernel(qf_ref, gf_ref, vf_ref, qb_ref, gb_ref, vb_ref, of_ref, ob_ref,
                    st_ref, bf_ref, bb_ref):
    t = pl.program_id(2)
    c = SCAN_CHUNK
    n_chunk = TOK_TILE // c

    @pl.when(t == 0)
    def _():
        st_ref[...] = jnp.zeros_like(st_ref)

    row = lax.broadcasted_iota(jnp.int32, (TOK_TILE, TOK_TILE), 0)
    col = lax.broadcasted_iota(jnp.int32, (TOK_TILE, TOK_TILE), 1)
    same = (row // c) == (col // c)
    tri_f = jnp.where(same & (col <= row), 1.0, 0.0).astype(BF16)
    tri_b = jnp.where(same & (col >= row), 1.0, 0.0).astype(BF16)

    def cumsum(tri, g):
        hi, mid, lo = _split3(g)
        return (jnp.dot(tri, hi, preferred_element_type=F32)
                + jnp.dot(tri, mid, preferred_element_type=F32)
                + jnp.dot(tri, lo, preferred_element_type=F32))

    bf_ref[...] = cumsum(tri_f, gf_ref[0])
    bb_ref[...] = cumsum(tri_b, gb_ref[0])

    ri = lax.broadcasted_iota(jnp.int32, (c, c), 0)
    ci = lax.broadcasted_iota(jnp.int32, (c, c), 1)
    keep_f = ci <= ri
    keep_b = ci >= ri

    def chunk(direction, base, q_ref, g_ref, v_ref, b_ref, o_ref, keep, mid_row, last_row):
        rows = pl.ds(base, c)
        for hh in range(SCAN_HEADS):
            lanes = slice(hh * HEAD_W, (hh + 1) * HEAD_W)
            q = q_ref[0, rows, lanes].astype(F32)
            g = g_ref[0, rows, lanes]
            v = v_ref[0, rows, lanes]
            bcum = b_ref[rows, lanes]
            r = bcum[mid_row:mid_row + 1, :]
            b_last = bcum[last_row:last_row + 1, :]
            d = jnp.clip(bcum - r, -EXP_CLAMP, EXP_CLAMP)
            e = jnp.exp(d)
            q_mid = q * e
            k_mid = (1.0 - jnp.exp(g)) * jnp.exp(-d)
            q_dec = (q_mid * jnp.exp(r)).astype(BF16)
            k_dec = (k_mid * jnp.exp(b_last - r)).astype(BF16)
            a = lax.dot_general(q_mid.astype(BF16), k_mid.astype(BF16), NT, preferred_element_type=F32)
            a = jnp.where(keep, a, 0.0).astype(BF16)
            st = st_ref[direction, hh]
            o = (lax.dot_general(q_dec, st.astype(BF16), NT, preferred_element_type=F32)
                 + jnp.dot(a, v, preferred_element_type=F32))
            o_ref[0, rows, lanes] = o
            st_ref[direction, hh] = jnp.exp(b_last) * st + lax.dot_general(
                v, k_dec, TN, preferred_element_type=F32)

    def body(i, carry):
        base_f = pl.multiple_of(i * c, c)
        base_b = pl.multiple_of((n_chunk - 1 - i) * c, c)
        chunk(0, base_f, qf_ref, gf_ref, vf_ref, bf_ref, of_ref, keep_f, c // 2 - 1, c - 1)
        chunk(1, base_b, qb_ref, gb_ref, vb_ref, bb_ref, ob_ref, keep_b, c // 2, 0)
        return carry

    lax.fori_loop(0, n_chunk, body, 0)


def _hg_scan(q, gf, gb, v):
    b, tot, di = q.shape
    nt = tot // TOK_TILE
    w = SCAN_HEADS * HEAD_W
    fwd = pl.BlockSpec((1, TOK_TILE, w), lambda i, h, t: (i, t, h))
    bwd = pl.BlockSpec((1, TOK_TILE, w), lambda i, h, t: (i, jnp.where(t == 0, 0, nt - t), h))
    out = jax.ShapeDtypeStruct((b, tot, di), F32)
    return pl.pallas_call(
        _hg_scan_kernel,
        grid=(b, di // w, nt),
        in_specs=[fwd, fwd, fwd, bwd, bwd, bwd],
        out_specs=[fwd, bwd],
        out_shape=[out, out],
        scratch_shapes=[pltpu.VMEM((2, SCAN_HEADS, HEAD_W, HEAD_W), F32),
                        pltpu.VMEM((TOK_TILE, w), F32),
                        pltpu.VMEM((TOK_TILE, w), F32)],
        compiler_params=_params(("arbitrary", "arbitrary", "arbitrary")),
        name="hg_scan",
    )(q, gf, v, q, gb, v)


def _hg_out_kernel(of_ref, ob_ref, gate_ref, ctx_ref, x_ref, gm_ref, hng_ref, w_ref, o_ref, y_ref):
    t = pl.program_id(1)
    n_head = of_ref.shape[-1] // HEAD_W
    for hh in range(n_head):
        lanes = slice(hh * HEAD_W, (hh + 1) * HEAD_W)
        o = of_ref[0, :, lanes] + ob_ref[0, :, lanes]
        o = o * lax.rsqrt(jnp.mean(o * o, axis=-1, keepdims=True) + EPS) * hng_ref[:, lanes]
        y_ref[:, lanes] = (o * gate_ref[0, :, lanes].astype(F32)).astype(BF16)
    y = jnp.dot(y_ref[...], w_ref[...], preferred_element_type=F32)
    x = jnp.where(t == 0, ctx_ref[0], x_ref[0])
    o_ref[0] = x + gm_ref[0, 0] * y


def _hg_out(of, ob, gate, ctx, x, gate_mod, hg_norm_g, w_out):
    b, tot, di = of.shape
    d = x.shape[-1]
    nt = tot // TOK_TILE
    tokw = pl.BlockSpec((1, TOK_TILE, di), lambda i, t: (i, t, 0))
    return pl.pallas_call(
        _hg_out_kernel,
        grid=(b, nt),
        in_specs=[tokw, tokw, tokw,
                  pl.BlockSpec((1, TOK_TILE, d), lambda i, t: (i, 0, 0)),
                  pl.BlockSpec((1, TOK_TILE, d), lambda i, t: (i, jnp.maximum(t - 1, 0), 0)),
                  pl.BlockSpec((1, 1, 1, d), lambda i, t: (i, jnp.minimum(t, 1), 0, 0)),
                  pl.BlockSpec((1, di), lambda i, t: (0, 0)),
                  _whole_vmem()],
        out_specs=pl.BlockSpec((1, TOK_TILE, d), lambda i, t: (i, t, 0)),
        out_shape=jax.ShapeDtypeStruct((b, tot, d), F32),
        scratch_shapes=[pltpu.VMEM((TOK_TILE, di), BF16)],
        compiler_params=_params(("arbitrary", "arbitrary")),
        name="hg_out",
    )(of, ob, gate, ctx, x, gate_mod, hg_norm_g, w_out)


def _rope_lane_tables(l_lat):
    ax = DA_DQK // 2
    inv = 1.0 / (ROPE_THETA ** (np.arange(0, ax, 2, dtype=np.float64) / ax))
    pos = np.arange(l_lat)
    ang_r = (pos // GRID_W)[:, None] * inv
    ang_c = (pos % GRID_W)[:, None] * inv
    zero = np.zeros_like(ang_r)
    cos64 = np.concatenate([np.cos(ang_r), np.cos(ang_r), np.cos(ang_c), np.cos(ang_c)], axis=1)
    up64 = np.concatenate([-np.sin(ang_r), zero, -np.sin(ang_c), zero], axis=1)
    dn64 = np.concatenate([zero, np.sin(ang_r), zero, np.sin(ang_c)], axis=1)
    lat = [np.concatenate([t, t], axis=1) for t in (cos64, up64, dn64)]
    ident = [np.ones((TOK_TILE, HEAD_W)), np.zeros((TOK_TILE, HEAD_W)), np.zeros((TOK_TILE, HEAD_W))]
    return [np.concatenate([i, t], axis=0).astype(np.float32) for i, t in zip(ident, lat)], (cos64, up64, dn64)


def _da_proj_kernel(x_ref, mod_ref, ng_ref, wq_ref, wk_ref, wv_ref, wg_ref,
                    kc_ref, ku_ref, kd_ref, qc_ref, qs_ref,
                    k_ref, vt_ref, qt_ref, gate_ref):
    t = pl.program_id(1)
    h = _modulated_norm(x_ref[0], ng_ref[...], mod_ref[0, 0, 0:1, :], mod_ref[0, 0, 1:2, :]).astype(BF16)
    n_head = k_ref.shape[-1] // HEAD_W

    k = jnp.dot(h, wk_ref[...], preferred_element_type=F32)
    cos, s_up, s_dn = kc_ref[...], ku_ref[...], kd_ref[...]
    for hh in range(n_head):
        lanes = slice(hh * HEAD_W, (hh + 1) * HEAD_W)
        kh = k[:, lanes]
        rot = (kh * cos + pltpu.roll(kh, HEAD_W - 16, axis=1) * s_up
               + pltpu.roll(kh, 16, axis=1) * s_dn)
        k_ref[0, :, lanes] = rot.astype(BF16)

    vt_ref[0] = lax.dot_general(wv_ref[...], h, NT, preferred_element_type=F32).astype(BF16)

    @pl.when(t > 0)
    def _():
        qt = lax.dot_general(wq_ref[...], h, NT, preferred_element_type=F32)
        qc, qs = qc_ref[...], qs_ref[...]
        for grp in range(qt.shape[0] // DA_DQK):
            x = qt[grp * DA_DQK:(grp + 1) * DA_DQK]
            partner = jnp.concatenate([x[16:32], x[0:16], x[48:64], x[32:48]], axis=0)
            qt_ref[0, grp * DA_DQK:(grp + 1) * DA_DQK, :] = (x * qc + partner * qs).astype(BF16)
        gate_ref[0] = _silu(jnp.dot(h, wg_ref[...], preferred_element_type=F32)).astype(BF16)


def _da_proj(x1, mod, norm_g, w_in, l_lat):
    b, tot, d = x1.shape
    di = w_in.shape[1] // 4
    nt = tot // TOK_TILE
    wq, wk, wv, wg = (w_in[:, j * di:(j + 1) * di] for j in range(4))
    ktabs, (cos64, up64, dn64) = _rope_lane_tables(l_lat)
    q_scale = DA_DQK ** -0.5 * LOG2E
    qc = (cos64.T * q_scale).astype(np.float32)
    qs = ((up64 + dn64).T * q_scale).astype(np.float32)
    ktab = pl.BlockSpec((TOK_TILE, HEAD_W), lambda i, t: (t, 0))
    qtab = pl.BlockSpec((DA_DQK, TOK_TILE), lambda i, t: (0, jnp.maximum(t - 1, 0)))
    return pl.pallas_call(
        _da_proj_kernel,
        grid=(b, nt),
        in_specs=[pl.BlockSpec((1, TOK_TILE, d), lambda i, t: (i, t, 0)),
                  pl.BlockSpec((1, 1, 2, d), lambda i, t: (i, jnp.minimum(t, 1), 0, 0)),
                  pl.BlockSpec((1, d), lambda i, t: (0, 0)),
                  _whole_vmem(), _whole_vmem(), _whole_vmem(), _whole_vmem(),
                  ktab, ktab, ktab, qtab, qtab],
        out_specs=[pl.BlockSpec((1, TOK_TILE, di), lambda i, t: (i, t, 0)),
                   pl.BlockSpec((1, di, TOK_TILE), lambda i, t: (i, 0, t)),
                   pl.BlockSpec((1, di, TOK_TILE), lambda i, t: (i, 0, jnp.maximum(t - 1, 0))),
                   pl.BlockSpec((1, TOK_TILE, di), lambda i, t: (i, jnp.maximum(t - 1, 0), 0))],
        out_shape=[jax.ShapeDtypeStruct((b, tot, di), BF16),
                   jax.ShapeDtypeStruct((b, di, tot), BF16),
                   jax.ShapeDtypeStruct((b, di, l_lat), BF16),
                   jax.ShapeDtypeStruct((b, l_lat, di), BF16)],
        compiler_params=_params(("arbitrary", "arbitrary")),
        name="da_proj",
    )(x1, mod, norm_g, wq.T.astype(BF16), wk.astype(BF16), wv.T.astype(BF16), wg.astype(BF16),
      *[jnp.asarray(a) for a in ktabs], jnp.asarray(qc), jnp.asarray(qs))


def _da_attn_kernel(k_ref, vt_ref, qt_ref, lam_ref, sg_ref, o_ref, p_ref, a_ref, l_ref, *, lambda_init):
    lq1, lk1, lq2, lk2 = (lam_ref[i:i + 1, :] for i in range(4))
    lam = (jnp.exp(jnp.sum(lq1 * lk1, axis=-1, keepdims=True))
           - jnp.exp(jnp.sum(lq2 * lk2, axis=-1, keepdims=True)) + lambda_init)
    out_gain = sg_ref[...] * (1.0 - lambda_init)
    n_chunk = k_ref.shape[1] // KEY_CHUNK
    n_q = qt_ref.shape[-1] // Q_TILE
    w2 = 2 * Q_TILE

    kf = k_ref[0].astype(F32)
    sel_r = lax.broadcasted_iota(jnp.int32, (HEAD_W, HEAD_W), 0) // DA_DQK
    sel_c = lax.broadcasted_iota(jnp.int32, (HEAD_W, HEAD_W), 1)
    sel = jnp.where(sel_r == sel_c, 1.0, 0.0).astype(BF16)
    norm2 = jnp.dot((kf * kf).astype(BF16), sel, preferred_element_type=F32)
    kmax = jnp.sqrt(jnp.max(norm2, axis=0, keepdims=True)) * BOUND_SLACK
    kmax = jnp.concatenate([jnp.broadcast_to(kmax[:, 0:1], (1, Q_TILE)),
                            jnp.broadcast_to(kmax[:, 1:2], (1, Q_TILE))], axis=1)
    half0 = lax.broadcasted_iota(jnp.int32, (HEAD_W, Q_TILE), 0) < DA_DQK

    def body(i, carry):
        start = pl.multiple_of(i * Q_TILE, Q_TILE)
        qt = qt_ref[0, :, pl.ds(start, Q_TILE)]
        zero = jnp.zeros_like(qt)
        rhs = jnp.concatenate([jnp.where(half0, qt, zero), jnp.where(half0, zero, qt)], axis=1)
        rf = rhs.astype(F32)
        m_bound = jnp.sqrt(jnp.sum(rf * rf, axis=0, keepdims=True)) * kmax

        def scores(c):
            return jnp.dot(k_ref[0, c * KEY_CHUNK:(c + 1) * KEY_CHUNK, :], rhs, preferred_element_type=F32)

        def fold(x):
            return x.reshape(KEY_CHUNK // 8, 8, w2)

        def exp_pass(m):
            l8 = jnp.zeros((8, w2), F32)
            for c in range(n_chunk):
                p = jnp.exp2(scores(c) - m)
                l8 = l8 + jnp.sum(fold(p), axis=0)
                p_ref[c * KEY_CHUNK:(c + 1) * KEY_CHUNK, :] = p.astype(BF16)
            return l8

        l8 = exp_pass(m_bound)
        l_ref[...] = l8
        healthy = jnp.min(l8) > MIN_COLUMN_SUM

        @pl.when(jnp.logical_not(healthy))
        def _():
            mx = jnp.full((8, w2), -jnp.inf, F32)
            for c in range(n_chunk):
                mx = jnp.maximum(mx, jnp.max(fold(scores(c)), axis=0))
            l_ref[...] = exp_pass(jnp.max(mx, axis=0, keepdims=True))

        l = jnp.sum(l_ref[...], axis=0, keepdims=True)
        l0, l1 = l[:, :Q_TILE], l[:, Q_TILE:]
        rho = (lam * l0 / l1).astype(BF16)
        for c in range(n_chunk):
            rows = slice(c * KEY_CHUNK, (c + 1) * KEY_CHUNK)
            a_ref[rows, :] = p_ref[rows, :Q_TILE] - p_ref[rows, Q_TILE:] * rho
        ot = jnp.dot(vt_ref[0], a_ref[...], preferred_element_type=F32) * (1.0 / l0)
        ot = ot * lax.rsqrt(jnp.mean(ot * ot, axis=0, keepdims=True) + EPS) * out_gain
        o_ref[0, pl.ds(start, Q_TILE), :] = ot.T
        return carry

    lax.fori_loop(0, n_q, body, 0)


def _da_attn(k, vt, qt, lam_vecs, subln_g, lambda_init):
    b, tot, di = k.shape
    l_lat = qt.shape[-1]
    n_head = di // HEAD_W
    return pl.pallas_call(
        functools.partial(_da_attn_kernel, lambda_init=lambda_init),
        grid=(b, n_head),
        in_specs=[pl.BlockSpec((1, tot, HEAD_W), lambda i, h: (i, 0, h)),
                  pl.BlockSpec((1, HEAD_W, tot), lambda i, h: (i, h, 0)),
                  pl.BlockSpec((1, HEAD_W, l_lat), lambda i, h: (i, h, 0)),
                  pl.BlockSpec((4, DA_DQK), lambda i, h: (0, 0)),
                  pl.BlockSpec((HEAD_W, 1), lambda i, h: (0, 0))],
        out_specs=pl.BlockSpec((1, l_lat, HEAD_W), lambda i, h: (i, 0, h)),
        out_shape=jax.ShapeDtypeStruct((b, l_lat, di), F32),
        scratch_shapes=[pltpu.VMEM((tot, 2 * Q_TILE), BF16),
                        pltpu.VMEM((tot, Q_TILE), BF16),
                        pltpu.VMEM((8, 2 * Q_TILE), F32)],
        compiler_params=_params(("arbitrary", "arbitrary")),
        name="da_attn",
    )(k, vt, qt, lam_vecs, subln_g.reshape(HEAD_W, 1))


def _da_out_kernel(o_ref, gate_ref, x_ref, gm_ref, fg_ref, w_ref, out_ref):
    y = (o_ref[0] * gate_ref[0].astype(F32)).astype(BF16)
    x = x_ref[0] + gm_ref[0] * jnp.dot(y, w_ref[...], preferred_element_type=F32)
    out_ref[0] = x * lax.rsqrt(jnp.mean(x * x, axis=-1, keepdims=True) + EPS) * fg_ref[...]


def _da_out(o, gate, x1, gate_mod, final_g, w_out):
    b, l_lat, di = o.shape
    d = x1.shape[-1]
    tokw = pl.BlockSpec((1, TOK_TILE, di), lambda i, t: (i, t, 0))
    return pl.pallas_call(
        _da_out_kernel,
        grid=(b, l_lat // TOK_TILE),
        in_specs=[tokw, tokw,
                  pl.BlockSpec((1, TOK_TILE, d), lambda i, t: (i, t + 1, 0)),
                  pl.BlockSpec((1, 1, d), lambda i, t: (i, 0, 0)),
                  pl.BlockSpec((1, d), lambda i, t: (0, 0)),
                  _whole_vmem()],
        out_specs=pl.BlockSpec((1, TOK_TILE, d), lambda i, t: (i, t, 0)),
        out_shape=jax.ShapeDtypeStruct((b, l_lat, d), F32),
        compiler_params=_params(("arbitrary", "arbitrary")),
        name="da_out",
    )(o, gate, x1, gate_mod, final_g, w_out)


def kernel(x, c, ctx, c_ctx, w_ada, b_ada, norm_g, hg_w_in, hg_lb_logits, hg_norm_g, hg_w_out,
           da_w_in, da_lam_q1, da_lam_k1, da_lam_q2, da_lam_k2, da_subln_g, da_w_out, final_g):
    b, l_lat, d = x.shape
    assert ctx.shape[1] == TOK_TILE and l_lat % TOK_TILE == 0 and l_lat % GRID_W == 0
    assert w_ada.shape[0] == 2 and hg_w_in.shape[0] == 1 and da_w_in.shape[0] == 1
    assert hg_lb_logits.shape[0] == 2

    rows = -(-(b + 1) // 8) * 8
    cc = jnp.concatenate([c, c_ctx[None], jnp.zeros((rows - b - 1, d), F32)], axis=0)
    ada = _adaln(cc, w_ada, b_ada)

    def mods(layer):
        m = ada[layer].reshape(rows, 3, d)
        pair = jnp.stack([jnp.broadcast_to(m[b], (b, 3, d)), m[:b]], axis=1)
        return pair[:, :, 0:2], pair[:, :, 2:3]

    mod0, gmod0 = mods(0)
    mod1, gmod1 = mods(1)

    q, gf, gb, v, gate0 = _hg_proj(ctx, x, mod0, norm_g[0:1], hg_w_in[0].astype(BF16), hg_lb_logits)
    of, ob = _hg_scan(q, gf, gb, v)
    x1 = _hg_out(of, ob, gate0, ctx, x, gmod0, hg_norm_g, hg_w_out[0].astype(BF16))

    lambda_init = 0.8 - 0.6 * math.exp(-0.3 * 1)
    k, vt, qt, gate1 = _da_proj(x1, mod1, norm_g[1:2], da_w_in[0], l_lat)
    lam_vecs = jnp.concatenate([da_lam_q1, da_lam_k1, da_lam_q2, da_lam_k2], axis=0)
    o = _da_attn(k, vt, qt, lam_vecs, da_subln_g[0], lambda_init)
    return _da_out(o, gate1, x1, gmod1[:, 1], final_g[None], da_w_out[0].astype(BF16))
```

```python
import functools
import math

import numpy as np
import jax
import jax.numpy as jnp
from jax import lax
from jax.experimental import pallas as pl
from jax.experimental.pallas import tpu as pltpu

F32 = jnp.float32
BF16 = jnp.bfloat16

EPS = 1e-6
GRID_W = 64
ROPE_THETA = 10000.0
HEAD_W = 128
DA_DQK = 64
TOK_TILE = 256
SCAN_CHUNK = 64
SCAN_HEADS = 4
Q_TILE = 256
KEY_CHUNK = 256
BOUND_SLACK = 1.01
MIN_COLUMN_SUM = 2.0 ** -64
EXP_CLAMP = 80.0
LOG2E = 1.4426950408889634
VMEM_LIMIT = 56 * 1024 * 1024

NT = (((1,), (1,)), ((), ()))
TN = (((0,), (0,)), ((), ()))


def _sigmoid(x):
    return 1.0 / (1.0 + jnp.exp(-x))


def _silu(x):
    return x * _sigmoid(x)


def _params(sem):
    return pltpu.CompilerParams(dimension_semantics=sem, vmem_limit_bytes=VMEM_LIMIT)


def _whole_vmem():
    return pl.BlockSpec(memory_space=pltpu.VMEM)


def _adaln_kernel(c_ref, w_ref, b_ref, o_ref):
    s = _silu(c_ref[...]).astype(BF16)
    o_ref[0] = jnp.dot(s, w_ref[0], preferred_element_type=F32) + b_ref[0]


def _adaln(cc, w_ada, b_ada):
    depth, d, d3 = w_ada.shape
    rows = cc.shape[0]
    return pl.pallas_call(
        _adaln_kernel,
        grid=(depth,),
        in_specs=[pl.BlockSpec((rows, d), lambda i: (0, 0)),
                  pl.BlockSpec((1, d, d3), lambda i: (i, 0, 0)),
                  pl.BlockSpec((1, 1, d3), lambda i: (i, 0, 0))],
        out_specs=pl.BlockSpec((1, rows, d3), lambda i: (i, 0, 0)),
        out_shape=jax.ShapeDtypeStruct((depth, rows, d3), F32),
        compiler_params=_params(("arbitrary",)),
        name="adaln",
    )(cc, w_ada.astype(BF16), b_ada.reshape(depth, 1, d3))


def _modulated_norm(x, g, shift, scale):
    y = x * lax.rsqrt(jnp.mean(x * x, axis=-1, keepdims=True) + EPS)
    return (y * g) * (1.0 + scale) + shift


def _hg_proj_kernel(ctx_ref, x_ref, mod_ref, ng_ref, w_ref, lbl_ref,
                    q_ref, gf_ref, gb_ref, v_ref, gate_ref):
    t = pl.program_id(1)
    x = jnp.where(t == 0, ctx_ref[0], x_ref[0])
    h = _modulated_norm(x, ng_ref[...], mod_ref[0, 0, 0:1, :], mod_ref[0, 0, 1:2, :]).astype(BF16)
    di = q_ref.shape[-1]

    def seg(j):
        return jnp.dot(h, w_ref[:, j * di:(j + 1) * di], preferred_element_type=F32)

    def log_forget(z, d):
        l0 = lbl_ref[0, d:d + 1, :]
        l1 = lbl_ref[1, d:d + 1, :]
        m = jnp.maximum(l0, l1)
        e0 = jnp.exp(l0 - m)
        lb = e0 / (e0 + jnp.exp(l1 - m))
        return jnp.log(lb + (1.0 - lb) * _sigmoid(z))

    q_ref[0] = _silu(seg(0)).astype(BF16)
    gf_ref[0] = log_forget(seg(1), 0)
    gb_ref[0] = log_forget(seg(2), 1)
    v_ref[0] = seg(3).astype(BF16)
    gate_ref[0] = _silu(seg(4)).astype(BF16)


def _hg_proj(ctx, x, mod, norm_g, w_in, lb_logits):
    b, l, d = x.shape
    di = w_in.shape[1] // 5
    nt = l // TOK_TILE + 1
    tot = l + TOK_TILE
    tok = lambda dt: jax.ShapeDtypeStruct((b, tot, di), dt)
    out_blk = pl.BlockSpec((1, TOK_TILE, di), lambda i, t: (i, t, 0))
    return pl.pallas_call(
        _hg_proj_kernel,
        grid=(b, nt),
        in_specs=[pl.BlockSpec((1, TOK_TILE, d), lambda i, t: (i, 0, 0)),
                  pl.BlockSpec((1, TOK_TILE, d), lambda i, t: (i, jnp.maximum(t - 1, 0), 0)),
                  pl.BlockSpec((1, 1, 2, d), lambda i, t: (i, jnp.minimum(t, 1), 0, 0)),
                  pl.BlockSpec((1, d), lambda i, t: (0, 0)),
                  _whole_vmem(),
                  pl.BlockSpec((2, 2, di), lambda i, t: (0, 0, 0))],
        out_specs=[out_blk] * 5,
        out_shape=[tok(BF16), tok(F32), tok(F32), tok(BF16), tok(BF16)],
        compiler_params=_params(("arbitrary", "arbitrary")),
        name="hg_proj",
    )(ctx, x, mod, norm_g, w_in, lb_logits)


def _hg_scan_kernel(qf_ref, gf_ref, vf_ref, qb_ref, gb_ref, vb_ref, of_ref, ob_ref,
                    s_ref, bf_ref, bb_ref, tri_ref):
    t = pl.program_id(2)
    c = SCAN_CHUNK
    n_chunk = TOK_TILE // c
    w = SCAN_HEADS * HEAD_W

    @pl.when((pl.program_id(0) == 0) & (pl.program_id(1) == 0) & (t == 0))
    def _():
        row = lax.broadcasted_iota(jnp.int32, (TOK_TILE, TOK_TILE), 0)
        col = lax.broadcasted_iota(jnp.int32, (TOK_TILE, TOK_TILE), 1)
        same = (row // c) == (col // c)
        tri_ref[0] = jnp.where(same & (col <= row), 1.0, 0.0).astype(BF16)
        tri_ref[1] = jnp.where(same & (col >= row), 1.0, 0.0).astype(BF16)

    @pl.when(t == 0)
    def _():
        s_ref[...] = jnp.zeros_like(s_ref)

    def cumsum(tri, g):
        hi = g.astype(BF16)
        lo = (g - hi.astype(F32)).astype(BF16)
        return jnp.dot(tri, hi, preferred_element_type=F32) + jnp.dot(tri, lo, preferred_element_type=F32)

    bf_ref[...] = cumsum(tri_ref[0], gf_ref[0])
    bb_ref[...] = cumsum(tri_ref[1], gb_ref[0])

    def chunk_decay_columns(b_ref, last_row):
        rows = [b_ref[j * c + last_row:j * c + last_row + 1, :] for j in range(n_chunk)]
        rows.append(jnp.zeros((8 - n_chunk, w), F32))
        return jnp.exp(jnp.concatenate(rows, axis=0).T)

    dec_f = chunk_decay_columns(bf_ref, c - 1)
    dec_b = chunk_decay_columns(bb_ref, 0)

    ri = lax.broadcasted_iota(jnp.int32, (c, c), 0)
    ci = lax.broadcasted_iota(jnp.int32, (c, c), 1)
    keep_f = ci <= ri
    keep_b = ci >= ri

    def chunk_head(direction, j, hh, q_ref, g_ref, v_ref, b_ref, o_ref, dec, keep, mid_row, last_row):
        rows = slice(j * c, (j + 1) * c)
        lanes = slice(hh * HEAD_W, (hh + 1) * HEAD_W)
        q = q_ref[0, rows, lanes].astype(F32)
        g = g_ref[0, rows, lanes]
        v = v_ref[0, rows, lanes]
        bcum = b_ref[rows, lanes]
        r = bcum[mid_row:mid_row + 1, :]
        b_last = bcum[last_row:last_row + 1, :]
        d = jnp.clip(bcum - r, -EXP_CLAMP, EXP_CLAMP)
        q_mid = q * jnp.exp(d)
        k_mid = (1.0 - jnp.exp(g)) * jnp.exp(-d)
        q_dec = (q_mid * jnp.exp(r)).astype(BF16)
        k_dec = (k_mid * jnp.exp(b_last - r)).astype(BF16)
        a = lax.dot_general(q_mid.astype(BF16), k_mid.astype(BF16), NT, preferred_element_type=F32)
        a = jnp.where(keep, a, 0.0).astype(BF16)
        s = s_ref[direction, hh]
        o_ref[0, rows, lanes] = jnp.dot(jnp.concatenate([q_dec, a], axis=1),
                                        jnp.concatenate([s.astype(BF16), v], axis=0),
                                        preferred_element_type=F32)
        s_ref[direction, hh] = s * dec[lanes, j:j + 1] + lax.dot_general(
            k_dec, v, TN, preferred_element_type=F32)

    for i in range(n_chunk):
        for hh in range(SCAN_HEADS):
            chunk_head(0, i, hh, qf_ref, gf_ref, vf_ref, bf_ref, of_ref, dec_f, keep_f, c // 2 - 1, c - 1)
            chunk_head(1, n_chunk - 1 - i, hh, qb_ref, gb_ref, vb_ref, bb_ref, ob_ref, dec_b, keep_b, c // 2, 0)


def _hg_scan(q, gf, gb, v):
    b, tot, di = q.shape
    nt = tot // TOK_TILE
    w = SCAN_HEADS * HEAD_W
    fwd = pl.BlockSpec((1, TOK_TILE, w), lambda i, h, t: (i, t, h))
    bwd = pl.BlockSpec((1, TOK_TILE, w), lambda i, h, t: (i, jnp.where(t == 0, 0, nt - t), h))
    out = jax.ShapeDtypeStruct((b, tot, di), F32)
    return pl.pallas_call(
        _hg_scan_kernel,
        grid=(b, di // w, nt),
        in_specs=[fwd, fwd, fwd, bwd, bwd, bwd],
        out_specs=[fwd, bwd],
        out_shape=[out, out],
        scratch_shapes=[pltpu.VMEM((2, SCAN_HEADS, HEAD_W, HEAD_W), F32),
                        pltpu.VMEM((TOK_TILE, w), F32),
                        pltpu.VMEM((TOK_TILE, w), F32),
                        pltpu.VMEM((2, TOK_TILE, TOK_TILE), BF16)],
        compiler_params=_params(("arbitrary", "arbitrary", "arbitrary")),
        name="hg_scan",
    )(q, gf, v, q, gb, v)


def _hg_out_kernel(of_ref, ob_ref, gate_ref, ctx_ref, x_ref, gm_ref, hng_ref, w_ref, o_ref, y_ref):
    t = pl.program_id(1)
    n_head = of_ref.shape[-1] // HEAD_W
    for hh in range(n_head):
        lanes = slice(hh * HEAD_W, (hh + 1) * HEAD_W)
        o = of_ref[0, :, lanes] + ob_ref[0, :, lanes]
        o = o * lax.rsqrt(jnp.mean(o * o, axis=-1, keepdims=True) + EPS) * hng_ref[:, lanes]
        y_ref[:, lanes] = (o * gate_ref[0, :, lanes].astype(F32)).astype(BF16)
    y = jnp.dot(y_ref[...], w_ref[...], preferred_element_type=F32)
    x = jnp.where(t == 0, ctx_ref[0], x_ref[0])
    o_ref[0] = x + gm_ref[0, 0] * y


def _hg_out(of, ob, gate, ctx, x, gate_mod, hg_norm_g, w_out):
    b, tot, di = of.shape
    d = x.shape[-1]
    nt = tot // TOK_TILE
    tokw = pl.BlockSpec((1, TOK_TILE, di), lambda i, t: (i, t, 0))
    return pl.pallas_call(
        _hg_out_kernel,
        grid=(b, nt),
        in_specs=[tokw, tokw, tokw,
                  pl.BlockSpec((1, TOK_TILE, d), lambda i, t: (i, 0, 0)),
                  pl.BlockSpec((1, TOK_TILE, d), lambda i, t: (i, jnp.maximum(t - 1, 0), 0)),
                  pl.BlockSpec((1, 1, 1, d), lambda i, t: (i, jnp.minimum(t, 1), 0, 0)),
                  pl.BlockSpec((1, di), lambda i, t: (0, 0)),
                  _whole_vmem()],
        out_specs=pl.BlockSpec((1, TOK_TILE, d), lambda i, t: (i, t, 0)),
        out_shape=jax.ShapeDtypeStruct((b, tot, d), F32),
        scratch_shapes=[pltpu.VMEM((TOK_TILE, di), BF16)],
        compiler_params=_params(("arbitrary", "arbitrary")),
        name="hg_out",
    )(of, ob, gate, ctx, x, gate_mod, hg_norm_g, w_out)


def _rope_lane_tables(l_lat):
    ax = DA_DQK // 2
    inv = 1.0 / (ROPE_THETA ** (np.arange(0, ax, 2, dtype=np.float64) / ax))
    pos = np.arange(l_lat)
    ang_r = (pos // GRID_W)[:, None] * inv
    ang_c = (pos % GRID_W)[:, None] * inv
    zero = np.zeros_like(ang_r)
    cos64 = np.concatenate([np.cos(ang_r), np.cos(ang_r), np.cos(ang_c), np.cos(ang_c)], axis=1)
    up64 = np.concatenate([-np.sin(ang_r), zero, -np.sin(ang_c), zero], axis=1)
    dn64 = np.concatenate([zero, np.sin(ang_r), zero, np.sin(ang_c)], axis=1)
    lat = [np.concatenate([t, t], axis=1) for t in (cos64, up64, dn64)]
    ident = [np.ones((TOK_TILE, HEAD_W)), np.zeros((TOK_TILE, HEAD_W)), np.zeros((TOK_TILE, HEAD_W))]
    return [np.concatenate([i, t], axis=0).astype(np.float32) for i, t in zip(ident, lat)], (cos64, up64, dn64)


def _da_proj_kernel(x_ref, mod_ref, ng_ref, wq_ref, wk_ref, wv_ref, wg_ref,
                    kc_ref, ku_ref, kd_ref, qc_ref, qs_ref,
                    k_ref, vt_ref, qt_ref, gate_ref):
    t = pl.program_id(1)
    h = _modulated_norm(x_ref[0], ng_ref[...], mod_ref[0, 0, 0:1, :], mod_ref[0, 0, 1:2, :]).astype(BF16)
    n_head = k_ref.shape[-1] // HEAD_W

    k = jnp.dot(h, wk_ref[...], preferred_element_type=F32)
    cos, s_up, s_dn = kc_ref[...], ku_ref[...], kd_ref[...]
    for hh in range(n_head):
        lanes = slice(hh * HEAD_W, (hh + 1) * HEAD_W)
        kh = k[:, lanes]
        rot = (kh * cos + pltpu.roll(kh, HEAD_W - 16, axis=1) * s_up
               + pltpu.roll(kh, 16, axis=1) * s_dn)
        k_ref[0, :, lanes] = rot.astype(BF16)

    vt_ref[0] = lax.dot_general(wv_ref[...], h, NT, preferred_element_type=F32).astype(BF16)

    @pl.when(t > 0)
    def _():
        qt = lax.dot_general(wq_ref[...], h, NT, preferred_element_type=F32)
        qc, qs = qc_ref[...], qs_ref[...]
        for grp in range(qt.shape[0] // DA_DQK):
            x = qt[grp * DA_DQK:(grp + 1) * DA_DQK]
            partner = jnp.concatenate([x[16:32], x[0:16], x[48:64], x[32:48]], axis=0)
            qt_ref[0, grp * DA_DQK:(grp + 1) * DA_DQK, :] = (x * qc + partner * qs).astype(BF16)
        gate_ref[0] = _silu(jnp.dot(h, wg_ref[...], preferred_element_type=F32)).astype(BF16)


def _da_proj(x1, mod, norm_g, w_in, l_lat):
    b, tot, d = x1.shape
    di = w_in.shape[1] // 4
    nt = tot // TOK_TILE
    wq, wk, wv, wg = (w_in[:, j * di:(j + 1) * di] for j in range(4))
    ktabs, (cos64, up64, dn64) = _rope_lane_tables(l_lat)
    q_scale = DA_DQK ** -0.5 * LOG2E
    qc = (cos64.T * q_scale).astype(np.float32)
    qs = ((up64 + dn64).T * q_scale).astype(np.float32)
    ktab = pl.BlockSpec((TOK_TILE, HEAD_W), lambda i, t: (t, 0))
    qtab = pl.BlockSpec((DA_DQK, TOK_TILE), lambda i, t: (0, jnp.maximum(t - 1, 0)))
    return pl.pallas_call(
        _da_proj_kernel,
        grid=(b, nt),
        in_specs=[pl.BlockSpec((1, TOK_TILE, d), lambda i, t: (i, t, 0)),
                  pl.BlockSpec((1, 1, 2, d), lambda i, t: (i, jnp.minimum(t, 1), 0, 0)),
                  pl.BlockSpec((1, d), lambda i, t: (0, 0)),
                  _whole_vmem(), _whole_vmem(), _whole_vmem(), _whole_vmem(),
                  ktab, ktab, ktab, qtab, qtab],
        out_specs=[pl.BlockSpec((1, TOK_TILE, di), lambda i, t: (i, t, 0)),
                   pl.BlockSpec((1, di, TOK_TILE), lambda i, t: (i, 0, t)),
                   pl.BlockSpec((1, di, TOK_TILE), lambda i, t: (i, 0, jnp.maximum(t - 1, 0))),
                   pl.BlockSpec((1, TOK_TILE, di), lambda i, t: (i, jnp.maximum(t - 1, 0), 0))],
        out_shape=[jax.ShapeDtypeStruct((b, tot, di), BF16),
                   jax.ShapeDtypeStruct((b, di, tot), BF16),
                   jax.ShapeDtypeStruct((b, di, l_lat), BF16),
                   jax.ShapeDtypeStruct((b, l_lat, di), BF16)],
        compiler_params=_params(("arbitrary", "arbitrary")),
        name="da_proj",
    )(x1, mod, norm_g, wq.T.astype(BF16), wk.astype(BF16), wv.T.astype(BF16), wg.astype(BF16),
      *[jnp.asarray(a) for a in ktabs], jnp.asarray(qc), jnp.asarray(qs))


def _da_attn_kernel(k_ref, vt_ref, qt_ref, lam_ref, sg_ref, o_ref, p_ref, a_ref, l_ref, *, lambda_init):
    lq1, lk1, lq2, lk2 = (lam_ref[i:i + 1, :] for i in range(4))
    lam = (jnp.exp(jnp.sum(lq1 * lk1, axis=-1, keepdims=True))
           - jnp.exp(jnp.sum(lq2 * lk2, axis=-1, keepdims=True)) + lambda_init)
    out_gain = sg_ref[...] * (1.0 - lambda_init)
    n_chunk = k_ref.shape[1] // KEY_CHUNK
    n_q = qt_ref.shape[-1] // Q_TILE
    w2 = 2 * Q_TILE

    kf = k_ref[0].astype(F32)
    sel_r = lax.broadcasted_iota(jnp.int32, (HEAD_W, HEAD_W), 0) // DA_DQK
    sel_c = lax.broadcasted_iota(jnp.int32, (HEAD_W, HEAD_W), 1)
    sel = jnp.where(sel_r == sel_c, 1.0, 0.0).astype(BF16)
    norm2 = jnp.dot((kf * kf).astype(BF16), sel, preferred_element_type=F32)
    kmax = jnp.sqrt(jnp.max(norm2, axis=0, keepdims=True)) * BOUND_SLACK
    kmax = jnp.concatenate([jnp.broadcast_to(kmax[:, 0:1], (1, Q_TILE)),
                            jnp.broadcast_to(kmax[:, 1:2], (1, Q_TILE))], axis=1)
    half0 = lax.broadcasted_iota(jnp.int32, (HEAD_W, Q_TILE), 0) < DA_DQK

    def body(i, carry):
        start = pl.multiple_of(i * Q_TILE, Q_TILE)
        qt = qt_ref[0, :, pl.ds(start, Q_TILE)]
        zero = jnp.zeros_like(qt)
        rhs = jnp.concatenate([jnp.where(half0, qt, zero), jnp.where(half0, zero, qt)], axis=1)
        rf = rhs.astype(F32)
        m_bound = jnp.sqrt(jnp.sum(rf * rf, axis=0, keepdims=True)) * kmax

        def scores(c):
            return jnp.dot(k_ref[0, c * KEY_CHUNK:(c + 1) * KEY_CHUNK, :], rhs, preferred_element_type=F32)

        def fold(x):
            return x.reshape(KEY_CHUNK // 8, 8, w2)

        def exp_pass(m):
            l8 = jnp.zeros((8, w2), F32)
            for c in range(n_chunk):
                p = jnp.exp2(scores(c) - m)
                l8 = l8 + jnp.sum(fold(p), axis=0)
                p_ref[c * KEY_CHUNK:(c + 1) * KEY_CHUNK, :] = p.astype(BF16)
            return l8

        l8 = exp_pass(m_bound)
        l_ref[...] = l8
        healthy = jnp.min(l8) > MIN_COLUMN_SUM

        @pl.when(jnp.logical_not(healthy))
        def _():
            mx = jnp.full((8, w2), -jnp.inf, F32)
            for c in range(n_chunk):
                mx = jnp.maximum(mx, jnp.max(fold(scores(c)), axis=0))
            l_ref[...] = exp_pass(jnp.max(mx, axis=0, keepdims=True))

        l = jnp.sum(l_ref[...], axis=0, keepdims=True)
        l0, l1 = l[:, :Q_TILE], l[:, Q_TILE:]
        rho = (lam * l0 / l1).astype(BF16)
        for c in range(n_chunk):
            rows = slice(c * KEY_CHUNK, (c + 1) * KEY_CHUNK)
            a_ref[rows, :] = p_ref[rows, :Q_TILE] - p_ref[rows, Q_TILE:] * rho
        ot = jnp.dot(vt_ref[0], a_ref[...], preferred_element_type=F32) * (1.0 / l0)
        ot = ot * lax.rsqrt(jnp.mean(ot * ot, axis=0, keepdims=True) + EPS) * out_gain
        o_ref[0, pl.ds(start, Q_TILE), :] = ot.T
        return carry

    lax.fori_loop(0, n_q, body, 0)


def _da_attn(k, vt, qt, lam_vecs, subln_g, lambda_init):
    b, tot, di = k.shape
    l_lat = qt.shape[-1]
    n_head = di // HEAD_W
    return pl.pallas_call(
        functools.partial(_da_attn_kernel, lambda_init=lambda_init),
        grid=(b, n_head),
        in_specs=[pl.BlockSpec((1, tot, HEAD_W), lambda i, h: (i, 0, h)),
                  pl.BlockSpec((1, HEAD_W, tot), lambda i, h: (i, h, 0)),
                  pl.BlockSpec((1, HEAD_W, l_lat), lambda i, h: (i, h, 0)),
                  pl.BlockSpec((4, DA_DQK), lambda i, h: (0, 0)),
                  pl.BlockSpec((HEAD_W, 1), lambda i, h: (0, 0))],
        out_specs=pl.BlockSpec((1, l_lat, HEAD_W), lambda i, h: (i, 0, h)),
        out_shape=jax.ShapeDtypeStruct((b, l_lat, di), F32),
        scratch_shapes=[pltpu.VMEM((tot, 2 * Q_TILE), BF16),
                        pltpu.VMEM((tot, Q_TILE), BF16),
                        pltpu.VMEM((8, 2 * Q_TILE), F32)],
        compiler_params=_params(("arbitrary", "arbitrary")),
        name="da_attn",
    )(k, vt, qt, lam_vecs, subln_g.reshape(HEAD_W, 1))


def _da_out_kernel(o_ref, gate_ref, x_ref, gm_ref, fg_ref, w_ref, out_ref):
    y = (o_ref[0] * gate_ref[0].astype(F32)).astype(BF16)
    x = x_ref[0] + gm_ref[0] * jnp.dot(y, w_ref[...], preferred_element_type=F32)
    out_ref[0] = x * lax.rsqrt(jnp.mean(x * x, axis=-1, keepdims=True) + EPS) * fg_ref[...]


def _da_out(o, gate, x1, gate_mod, final_g, w_out):
    b, l_lat, di = o.shape
    d = x1.shape[-1]
    tokw = pl.BlockSpec((1, TOK_TILE, di), lambda i, t: (i, t, 0))
    return pl.pallas_call(
        _da_out_kernel,
        grid=(b, l_lat // TOK_TILE),
        in_specs=[tokw, tokw,
                  pl.BlockSpec((1, TOK_TILE, d), lambda i, t: (i, t + 1, 0)),
                  pl.BlockSpec((1, 1, d), lambda i, t: (i, 0, 0)),
                  pl.BlockSpec((1, d), lambda i, t: (0, 0)),
                  _whole_vmem()],
        out_specs=pl.BlockSpec((1, TOK_TILE, d), lambda i, t: (i, t, 0)),
        out_shape=jax.ShapeDtypeStruct((b, l_lat, d), F32),
        compiler_params=_params(("arbitrary", "arbitrary")),
        name="da_out",
    )(o, gate, x1, gate_mod, final_g, w_out)


def kernel(x, c, ctx, c_ctx, w_ada, b_ada, norm_g, hg_w_in, hg_lb_logits, hg_norm_g, hg_w_out,
           da_w_in, da_lam_q1, da_lam_k1, da_lam_q2, da_lam_k2, da_subln_g, da_w_out, final_g):
    b, l_lat, d = x.shape
    assert ctx.shape[1] == TOK_TILE and l_lat % TOK_TILE == 0 and l_lat % GRID_W == 0
    assert w_ada.shape[0] == 2 and hg_w_in.shape[0] == 1 and da_w_in.shape[0] == 1
    assert hg_lb_logits.shape[0] == 2

    rows = -(-(b + 1) // 8) * 8
    cc = jnp.concatenate([c, c_ctx[None], jnp.zeros((rows - b - 1, d), F32)], axis=0)
    ada = _adaln(cc, w_ada, b_ada)

    def mods(layer):
        m = ada[layer].reshape(rows, 3, d)
        pair = jnp.stack([jnp.broadcast_to(m[b], (b, 3, d)), m[:b]], axis=1)
        return pair[:, :, 0:2], pair[:, :, 2:3]

    mod0, gmod0 = mods(0)
    mod1, gmod1 = mods(1)

    q, gf, gb, v, gate0 = _hg_proj(ctx, x, mod0, norm_g[0:1], hg_w_in[0].astype(BF16), hg_lb_logits)
    of, ob = _hg_scan(q, gf, gb, v)
    x1 = _hg_out(of, ob, gate0, ctx, x, gmod0, hg_norm_g, hg_w_out[0].astype(BF16))

    lambda_init = 0.8 - 0.6 * math.exp(-0.3 * 1)
    k, vt, qt, gate1 = _da_proj(x1, mod1, norm_g[1:2], da_w_in[0], l_lat)
    lam_vecs = jnp.concatenate([da_lam_q1, da_lam_k1, da_lam_q2, da_lam_k2], axis=0)
    o = _da_attn(k, vt, qt, lam_vecs, da_subln_g[0], lambda_init)
    return _da_out(o, gate1, x1, gmod1[:, 1], final_g[None], da_w_out[0].astype(BF16))
```

```python
import functools
import math

import numpy as np
import jax
import jax.numpy as jnp
from jax import lax
from jax.experimental import pallas as pl
from jax.experimental.pallas import tpu as pltpu

F32 = jnp.float32
BF16 = jnp.bfloat16

EPS = 1e-6
GRID_W = 64
ROPE_THETA = 10000.0
HEAD_W = 128
DA_DQK = 64
TOK_TILE = 256
SCAN_CHUNK = 64
SCAN_HEADS = 4
Q_TILE = 256
KEY_CHUNK = 256
BOUND_SLACK = 1.01
MIN_COLUMN_SUM = 2.0 ** -64
EXP_CLAMP = 80.0
LOG2E = 1.4426950408889634
VMEM_LIMIT = 56 * 1024 * 1024

NT = (((1,), (1,)), ((), ()))
TN = (((0,), (0,)), ((), ()))


def _sigmoid(x):
    return 1.0 / (1.0 + jnp.exp(-x))


def _silu(x):
    return x * _sigmoid(x)


def _params(sem):
    return pltpu.CompilerParams(dimension_semantics=sem, vmem_limit_bytes=VMEM_LIMIT)


def _whole_vmem():
    return pl.BlockSpec(memory_space=pltpu.VMEM)


def _adaln_kernel(c_ref, w_ref, b_ref, o_ref):
    s = _silu(c_ref[...]).astype(BF16)
    o_ref[0] = jnp.dot(s, w_ref[0], preferred_element_type=F32) + b_ref[0]


def _adaln(cc, w_ada, b_ada):
    depth, d, d3 = w_ada.shape
    rows = cc.shape[0]
    return pl.pallas_call(
        _adaln_kernel,
        grid=(depth,),
        in_specs=[pl.BlockSpec((rows, d), lambda i: (0, 0)),
                  pl.BlockSpec((1, d, d3), lambda i: (i, 0, 0)),
                  pl.BlockSpec((1, 1, d3), lambda i: (i, 0, 0))],
        out_specs=pl.BlockSpec((1, rows, d3), lambda i: (i, 0, 0)),
        out_shape=jax.ShapeDtypeStruct((depth, rows, d3), F32),
        compiler_params=_params(("arbitrary",)),
        name="adaln",
    )(cc, w_ada.astype(BF16), b_ada.reshape(depth, 1, d3))


def _modulated_norm(x, g, shift, scale):
    y = x * lax.rsqrt(jnp.mean(x * x, axis=-1, keepdims=True) + EPS)
    return (y * g) * (1.0 + scale) + shift


def _hg_proj_kernel(ctx_ref, x_ref, mod_ref, ng_ref, w_ref, lbl_ref,
                    q_ref, gf_ref, gb_ref, v_ref, gate_ref):
    t = pl.program_id(1)
    x = jnp.where(t == 0, ctx_ref[0], x_ref[0])
    h = _modulated_norm(x, ng_ref[...], mod_ref[0, 0, 0:1, :], mod_ref[0, 0, 1:2, :]).astype(BF16)
    di = q_ref.shape[-1]

    def seg(j):
        return jnp.dot(h, w_ref[:, j * di:(j + 1) * di], preferred_element_type=F32)

    def log_forget(z, d):
        l0 = lbl_ref[0, d:d + 1, :]
        l1 = lbl_ref[1, d:d + 1, :]
        m = jnp.maximum(l0, l1)
        e0 = jnp.exp(l0 - m)
        lb = e0 / (e0 + jnp.exp(l1 - m))
        return jnp.log(lb + (1.0 - lb) * _sigmoid(z))

    q_ref[0] = _silu(seg(0)).astype(BF16)
    gf_ref[0] = log_forget(seg(1), 0)
    gb_ref[0] = log_forget(seg(2), 1)
    v_ref[0] = seg(3).astype(BF16)
    gate_ref[0] = _silu(seg(4)).astype(BF16)


def _hg_proj(ctx, x, mod, norm_g, w_in, lb_logits):
    b, l, d = x.shape
    di = w_in.shape[1] // 5
    nt = l // TOK_TILE + 1
    tot = l + TOK_TILE
    tok = lambda dt: jax.ShapeDtypeStruct((b, tot, di), dt)
    out_blk = pl.BlockSpec((1, TOK_TILE, di), lambda i, t: (i, t, 0))
    return pl.pallas_call(
        _hg_proj_kernel,
        grid=(b, nt),
        in_specs=[pl.BlockSpec((1, TOK_TILE, d), lambda i, t: (i, 0, 0)),
                  pl.BlockSpec((1, TOK_TILE, d), lambda i, t: (i, jnp.maximum(t - 1, 0), 0)),
                  pl.BlockSpec((1, 1, 2, d), lambda i, t: (i, jnp.minimum(t, 1), 0, 0)),
                  pl.BlockSpec((1, d), lambda i, t: (0, 0)),
                  _whole_vmem(),
                  pl.BlockSpec((2, 2, di), lambda i, t: (0, 0, 0))],
        out_specs=[out_blk] * 5,
        out_shape=[tok(BF16), tok(F32), tok(F32), tok(BF16), tok(BF16)],
        compiler_params=_params(("arbitrary", "arbitrary")),
        name="hg_proj",
    )(ctx, x, mod, norm_g, w_in, lb_logits)


def _hg_scan_kernel(qf_ref, gf_ref, vf_ref, qb_ref, gb_ref, vb_ref, of_ref, ob_ref,
                    s_ref, bf_ref, bb_ref, tri_ref):
    t = pl.program_id(2)
    c = SCAN_CHUNK
    n_chunk = TOK_TILE // c
    w = SCAN_HEADS * HEAD_W

    @pl.when((pl.program_id(0) == 0) & (pl.program_id(1) == 0) & (t == 0))
    def _():
        row = lax.broadcasted_iota(jnp.int32, (TOK_TILE, TOK_TILE), 0)
        col = lax.broadcasted_iota(jnp.int32, (TOK_TILE, TOK_TILE), 1)
        same = (row // c) == (col // c)
        tri_ref[0] = jnp.where(same & (col <= row), 1.0, 0.0).astype(BF16)
        tri_ref[1] = jnp.where(same & (col >= row), 1.0, 0.0).astype(BF16)

    @pl.when(t == 0)
    def _():
        s_ref[...] = jnp.zeros_like(s_ref)

    def cumsum(tri, g):
        hi = g.astype(BF16)
        lo = (g - hi.astype(F32)).astype(BF16)
        return jnp.dot(tri, hi, preferred_element_type=F32) + jnp.dot(tri, lo, preferred_element_type=F32)

    bf_ref[...] = cumsum(tri_ref[0], gf_ref[0])
    bb_ref[...] = cumsum(tri_ref[1], gb_ref[0])

    def chunk_decay_columns(b_ref, last_row):
        rows = [b_ref[j * c + last_row:j * c + last_row + 1, :] for j in range(n_chunk)]
        rows.append(jnp.zeros((8 - n_chunk, w), F32))
        return jnp.exp(jnp.concatenate(rows, axis=0).T)

    dec_f = chunk_decay_columns(bf_ref, c - 1)
    dec_b = chunk_decay_columns(bb_ref, 0)

    ri = lax.broadcasted_iota(jnp.int32, (c, c), 0)
    ci = lax.broadcasted_iota(jnp.int32, (c, c), 1)
    keep_f = ci <= ri
    keep_b = ci >= ri

    def chunk_head(direction, j, hh, q_ref, g_ref, v_ref, b_ref, o_ref, dec, keep, mid_row, last_row):
        rows = slice(j * c, (j + 1) * c)
        lanes = slice(hh * HEAD_W, (hh + 1) * HEAD_W)
        q = q_ref[0, rows, lanes].astype(F32)
        g = g_ref[0, rows, lanes]
        v = v_ref[0, rows, lanes]
        bcum = b_ref[rows, lanes]
        r = bcum[mid_row:mid_row + 1, :]
        b_last = bcum[last_row:last_row + 1, :]
        d = jnp.clip(bcum - r, -EXP_CLAMP, EXP_CLAMP)
        q_mid = q * jnp.exp(d)
        k_mid = (1.0 - jnp.exp(g)) * jnp.exp(-d)
        q_dec = (q_mid * jnp.exp(r)).astype(BF16)
        k_dec = (k_mid * jnp.exp(b_last - r)).astype(BF16)
        a = lax.dot_general(q_mid.astype(BF16), k_mid.astype(BF16), NT, preferred_element_type=F32)
        a = jnp.where(keep, a, 0.0).astype(BF16)
        s = s_ref[direction, hh]
        o_ref[0, rows, lanes] = jnp.dot(jnp.concatenate([q_dec, a], axis=1),
                                        jnp.concatenate([s.astype(BF16), v], axis=0),
                                        preferred_element_type=F32)
        s_ref[direction, hh] = s * dec[lanes, j:j + 1] + lax.dot_general(
            k_dec, v, TN, preferred_element_type=F32)

    for i in range(n_chunk):
        for hh in range(SCAN_HEADS):
            chunk_head(0, i, hh, qf_ref, gf_ref, vf_ref, bf_ref, of_ref, dec_f, keep_f, c // 2 - 1, c - 1)
            chunk_head(1, n_chunk - 1 - i, hh, qb_ref, gb_ref, vb_ref, bb_ref, ob_ref, dec_b, keep_b, c // 2, 0)


def _hg_scan(q, gf, gb, v):
    b, tot, di = q.shape
    nt = tot // TOK_TILE
    w = SCAN_HEADS * HEAD_W
    fwd = pl.BlockSpec((1, TOK_TILE, w), lambda i, h, t: (i, t, h))
    bwd = pl.BlockSpec((1, TOK_TILE, w), lambda i, h, t: (i, jnp.where(t == 0, 0, nt - t), h))
    out = jax.ShapeDtypeStruct((b, tot, di), F32)
    return pl.pallas_call(
        _hg_scan_kernel,
        grid=(b, di // w, nt),
        in_specs=[fwd, fwd, fwd, bwd, bwd, bwd],
        out_specs=[fwd, bwd],
        out_shape=[out, out],
        scratch_shapes=[pltpu.VMEM((2, SCAN_HEADS, HEAD_W, HEAD_W), F32),
                        pltpu.VMEM((TOK_TILE, w), F32),
                        pltpu.VMEM((TOK_TILE, w), F32),
                        pltpu.VMEM((2, TOK_TILE, TOK_TILE), BF16)],
        compiler_params=_params(("arbitrary", "arbitrary", "arbitrary")),
        name="hg_scan",
    )(q, gf, v, q, gb, v)


def _hg_out_kernel(of_ref, ob_ref, gate_ref, ctx_ref, x_ref, gm_ref, hng_ref, w_ref, o_ref, y_ref):
    t = pl.program_id(1)
    n_head = of_ref.shape[-1] // HEAD_W
    for hh in range(n_head):
        lanes = slice(hh * HEAD_W, (hh + 1) * HEAD_W)
        o = of_ref[0, :, lanes] + ob_ref[0, :, lanes]
        o = o * lax.rsqrt(jnp.mean(o * o, axis=-1, keepdims=True) + EPS) * hng_ref[:, lanes]
        y_ref[:, lanes] = (o * gate_ref[0, :, lanes].astype(F32)).astype(BF16)
    y = jnp.dot(y_ref[...], w_ref[...], preferred_element_type=F32)
    x = jnp.where(t == 0, ctx_ref[0], x_ref[0])
    o_ref[0] = x + gm_ref[0, 0] * y


def _hg_out(of, ob, gate, ctx, x, gate_mod, hg_norm_g, w_out):
    b, tot, di = of.shape
    d = x.shape[-1]
    nt = tot // TOK_TILE
    tokw = pl.BlockSpec((1, TOK_TILE, di), lambda i, t: (i, t, 0))
    return pl.pallas_call(
        _hg_out_kernel,
        grid=(b, nt),
        in_specs=[tokw, tokw, tokw,
                  pl.BlockSpec((1, TOK_TILE, d), lambda i, t: (i, 0, 0)),
                  pl.BlockSpec((1, TOK_TILE, d), lambda i, t: (i, jnp.maximum(t - 1, 0), 0)),
                  pl.BlockSpec((1, 1, 1, d), lambda i, t: (i, jnp.minimum(t, 1), 0, 0)),
                  pl.BlockSpec((1, di), lambda i, t: (0, 0)),
                  _whole_vmem()],
        out_specs=pl.BlockSpec((1, TOK_TILE, d), lambda i, t: (i, t, 0)),
        out_shape=jax.ShapeDtypeStruct((b, tot, d), F32),
        scratch_shapes=[pltpu.VMEM((TOK_TILE, di), BF16)],
        compiler_params=_params(("arbitrary", "arbitrary")),
        name="hg_out",
    )(of, ob, gate, ctx, x, gate_mod, hg_norm_g, w_out)


def _rope_lane_tables(l_lat):
    ax = DA_DQK // 2
    inv = 1.0 / (ROPE_THETA ** (np.arange(0, ax, 2, dtype=np.float64) / ax))
    pos = np.arange(l_lat)
    ang_r = (pos // GRID_W)[:, None] * inv
    ang_c = (pos % GRID_W)[:, None] * inv
    zero = np.zeros_like(ang_r)
    cos64 = np.concatenate([np.cos(ang_r), np.cos(ang_r), np.cos(ang_c), np.cos(ang_c)], axis=1)
    up64 = np.concatenate([-np.sin(ang_r), zero, -np.sin(ang_c), zero], axis=1)
    dn64 = np.concatenate([zero, np.sin(ang_r), zero, np.sin(ang_c)], axis=1)
    lat = [np.concatenate([t, t], axis=1) for t in (cos64, up64, dn64)]
    ident = [np.ones((TOK_TILE, HEAD_W)), np.zeros((TOK_TILE, HEAD_W)), np.zeros((TOK_TILE, HEAD_W))]
    return [np.concatenate([i, t], axis=0).astype(np.float32) for i, t in zip(ident, lat)], (cos64, up64, dn64)


def _da_proj_kernel(x_ref, mod_ref, ng_ref, wq_ref, wk_ref, wv_ref, wg_ref,
                    kc_ref, ku_ref, kd_ref, qc_ref, qs_ref,
                    k_ref, vt_ref, qt_ref, gate_ref):
    t = pl.program_id(1)
    h = _modulated_norm(x_ref[0], ng_ref[...], mod_ref[0, 0, 0:1, :], mod_ref[0, 0, 1:2, :]).astype(BF16)
    n_head = k_ref.shape[-1] // HEAD_W

    k = jnp.dot(h, wk_ref[...], preferred_element_type=F32)
    cos, s_up, s_dn = kc_ref[...], ku_ref[...], kd_ref[...]
    for hh in range(n_head):
        lanes = slice(hh * HEAD_W, (hh + 1) * HEAD_W)
        kh = k[:, lanes]
        rot = (kh * cos + pltpu.roll(kh, HEAD_W - 16, axis=1) * s_up
               + pltpu.roll(kh, 16, axis=1) * s_dn)
        k_ref[0, :, lanes] = rot.astype(BF16)

    vt_ref[0] = lax.dot_general(wv_ref[...], h, NT, preferred_element_type=F32).astype(BF16)

    @pl.when(t > 0)
    def _():
        qt = lax.dot_general(wq_ref[...], h, NT, preferred_element_type=F32)
        qc, qs = qc_ref[...], qs_ref[...]
        for grp in range(qt.shape[0] // DA_DQK):
            x = qt[grp * DA_DQK:(grp + 1) * DA_DQK]
            partner = jnp.concatenate([x[16:32], x[0:16], x[48:64], x[32:48]], axis=0)
            qt_ref[0, grp * DA_DQK:(grp + 1) * DA_DQK, :] = (x * qc + partner * qs).astype(BF16)
        gate_ref[0] = _silu(jnp.dot(h, wg_ref[...], preferred_element_type=F32)).astype(BF16)


def _da_proj(x1, mod, norm_g, w_in, l_lat):
    b, tot, d = x1.shape
    di = w_in.shape[1] // 4
    nt = tot // TOK_TILE
    wq, wk, wv, wg = (w_in[:, j * di:(j + 1) * di] for j in range(4))
    ktabs, (cos64, up64, dn64) = _rope_lane_tables(l_lat)
    q_scale = DA_DQK ** -0.5 * LOG2E
    qc = (cos64.T * q_scale).astype(np.float32)
    qs = ((up64 + dn64).T * q_scale).astype(np.float32)
    ktab = pl.BlockSpec((TOK_TILE, HEAD_W), lambda i, t: (t, 0))
    qtab = pl.BlockSpec((DA_DQK, TOK_TILE), lambda i, t: (0, jnp.maximum(t - 1, 0)))
    return pl.pallas_call(
        _da_proj_kernel,
        grid=(b, nt),
        in_specs=[pl.BlockSpec((1, TOK_TILE, d), lambda i, t: (i, t, 0)),
                  pl.BlockSpec((1, 1, 2, d), lambda i, t: (i, jnp.minimum(t, 1), 0, 0)),
                  pl.BlockSpec((1, d), lambda i, t: (0, 0)),
                  _whole_vmem(), _whole_vmem(), _whole_vmem(), _whole_vmem(),
                  ktab, ktab, ktab, qtab, qtab],
        out_specs=[pl.BlockSpec((1, TOK_TILE, di), lambda i, t: (i, t, 0)),
                   pl.BlockSpec((1, di, TOK_TILE), lambda i, t: (i, 0, t)),
                   pl.BlockSpec((1, di, TOK_TILE), lambda i, t: (i, 0, jnp.maximum(t - 1, 0))),
                   pl.BlockSpec((1, TOK_TILE, di), lambda i, t: (i, jnp.maximum(t - 1, 0), 0))],
        out_shape=[jax.ShapeDtypeStruct((b, tot, di), BF16),
                   jax.ShapeDtypeStruct((b, di, tot), BF16),
                   jax.ShapeDtypeStruct((b, di, l_lat), BF16),
                   jax.ShapeDtypeStruct((b, l_lat, di), BF16)],
        compiler_params=_params(("arbitrary", "arbitrary")),
        name="da_proj",
    )(x1, mod, norm_g, wq.T.astype(BF16), wk.astype(BF16), wv.T.astype(BF16), wg.astype(BF16),
      *[jnp.asarray(a) for a in ktabs], jnp.asarray(qc), jnp.asarray(qs))


def _da_attn_kernel(k_ref, vt_ref, qt_ref, lam_ref, sg_ref, o_ref, pa_ref, pb_ref, a_ref, la_ref, lb_ref,
                    *, lambda_init):
    lq1, lk1, lq2, lk2 = (lam_ref[i:i + 1, :] for i in range(4))
    lam = (jnp.exp(jnp.sum(lq1 * lk1, axis=-1, keepdims=True))
           - jnp.exp(jnp.sum(lq2 * lk2, axis=-1, keepdims=True)) + lambda_init)
    out_gain = sg_ref[...] * (1.0 - lambda_init)
    n_chunk = k_ref.shape[1] // KEY_CHUNK
    n_q = qt_ref.shape[-1] // Q_TILE
    w2 = 2 * Q_TILE

    kf = k_ref[0].astype(F32)
    sel_r = lax.broadcasted_iota(jnp.int32, (HEAD_W, HEAD_W), 0) // DA_DQK
    sel_c = lax.broadcasted_iota(jnp.int32, (HEAD_W, HEAD_W), 1)
    sel = jnp.where(sel_r == sel_c, 1.0, 0.0).astype(BF16)
    norm2 = jnp.dot((kf * kf).astype(BF16), sel, preferred_element_type=F32)
    kmax = jnp.sqrt(jnp.max(norm2, axis=0, keepdims=True)) * BOUND_SLACK
    kmax = jnp.concatenate([jnp.broadcast_to(kmax[:, 0:1], (1, Q_TILE)),
                            jnp.broadcast_to(kmax[:, 1:2], (1, Q_TILE))], axis=1)
    half0 = lax.broadcasted_iota(jnp.int32, (HEAD_W, Q_TILE), 0) < DA_DQK

    def fold(x):
        return x.reshape(KEY_CHUNK // 8, 8, w2)

    def exp_stage(i, p_ref, l_ref, overlapped):
        start = i * Q_TILE if isinstance(i, int) else pl.multiple_of(i * Q_TILE, Q_TILE)
        qt = qt_ref[0, :, pl.ds(start, Q_TILE)]
        zero = jnp.zeros_like(qt)
        rhs = jnp.concatenate([jnp.where(half0, qt, zero), jnp.where(half0, zero, qt)], axis=1)
        rf = rhs.astype(F32)
        m_bound = jnp.sqrt(jnp.sum(rf * rf, axis=0, keepdims=True)) * kmax

        def scores(c):
            return jnp.dot(k_ref[0, c * KEY_CHUNK:(c + 1) * KEY_CHUNK, :], rhs, preferred_element_type=F32)

        def exp_pass(m):
            l8 = jnp.zeros((8, w2), F32)
            for c in range(n_chunk):
                p = jnp.exp2(scores(c) - m)
                l8 = l8 + jnp.sum(fold(p), axis=0)
                p_ref[c * KEY_CHUNK:(c + 1) * KEY_CHUNK, :] = p.astype(BF16)
            return l8

        l8 = exp_pass(m_bound)
        overlapped()
        l_ref[...] = l8

        @pl.when(jnp.logical_not(jnp.min(l8) > MIN_COLUMN_SUM))
        def _():
            mx = jnp.full((8, w2), -jnp.inf, F32)
            for c in range(n_chunk):
                mx = jnp.maximum(mx, jnp.max(fold(scores(c)), axis=0))
            l_ref[...] = exp_pass(jnp.max(mx, axis=0, keepdims=True))

    def value_stage(i, p_ref, l_ref):
        start = i * Q_TILE if isinstance(i, int) else pl.multiple_of(i * Q_TILE, Q_TILE)
        l = jnp.sum(l_ref[...], axis=0, keepdims=True)
        l0, l1 = l[:, :Q_TILE], l[:, Q_TILE:]
        rho = (lam * l0 / l1).astype(BF16)
        for c in range(n_chunk):
            rows = slice(c * KEY_CHUNK, (c + 1) * KEY_CHUNK)
            a_ref[rows, :] = p_ref[rows, :Q_TILE] - p_ref[rows, Q_TILE:] * rho
        ot = jnp.dot(vt_ref[0], a_ref[...], preferred_element_type=F32) * (1.0 / l0)
        ot = ot * lax.rsqrt(jnp.mean(ot * ot, axis=0, keepdims=True) + EPS) * out_gain
        o_ref[0, pl.ds(start, Q_TILE), :] = ot.T

    even = (pa_ref, la_ref)
    odd = (pb_ref, lb_ref)
    exp_stage(0, *even, lambda: None)

    def body(j, carry):
        i = 2 * j
        exp_stage(i + 1, *odd, lambda: value_stage(i, *even))
        exp_stage(i + 2, *even, lambda: value_stage(i + 1, *odd))
        return carry

    lax.fori_loop(0, n_q // 2 - 1, body, 0)
    exp_stage(n_q - 1, *odd, lambda: value_stage(n_q - 2, *even))
    value_stage(n_q - 1, *odd)


def _da_attn(k, vt, qt, lam_vecs, subln_g, lambda_init):
    b, tot, di = k.shape
    l_lat = qt.shape[-1]
    n_head = di // HEAD_W
    return pl.pallas_call(
        functools.partial(_da_attn_kernel, lambda_init=lambda_init),
        grid=(b, n_head),
        in_specs=[pl.BlockSpec((1, tot, HEAD_W), lambda i, h: (i, 0, h)),
                  pl.BlockSpec((1, HEAD_W, tot), lambda i, h: (i, h, 0)),
                  pl.BlockSpec((1, HEAD_W, l_lat), lambda i, h: (i, h, 0)),
                  pl.BlockSpec((4, DA_DQK), lambda i, h: (0, 0)),
                  pl.BlockSpec((HEAD_W, 1), lambda i, h: (0, 0))],
        out_specs=pl.BlockSpec((1, l_lat, HEAD_W), lambda i, h: (i, 0, h)),
        out_shape=jax.ShapeDtypeStruct((b, l_lat, di), F32),
        scratch_shapes=[pltpu.VMEM((tot, 2 * Q_TILE), BF16),
                        pltpu.VMEM((tot, 2 * Q_TILE), BF16),
                        pltpu.VMEM((tot, Q_TILE), BF16),
                        pltpu.VMEM((8, 2 * Q_TILE), F32),
                        pltpu.VMEM((8, 2 * Q_TILE), F32)],
        compiler_params=_params(("arbitrary", "arbitrary")),
        name="da_attn",
    )(k, vt, qt, lam_vecs, subln_g.reshape(HEAD_W, 1))


def _da_out_kernel(o_ref, gate_ref, x_ref, gm_ref, fg_ref, w_ref, out_ref):
    y = (o_ref[0] * gate_ref[0].astype(F32)).astype(BF16)
    x = x_ref[0] + gm_ref[0] * jnp.dot(y, w_ref[...], preferred_element_type=F32)
    out_ref[0] = x * lax.rsqrt(jnp.mean(x * x, axis=-1, keepdims=True) + EPS) * fg_ref[...]


def _da_out(o, gate, x1, gate_mod, final_g, w_out):
    b, l_lat, di = o.shape
    d = x1.shape[-1]
    tokw = pl.BlockSpec((1, TOK_TILE, di), lambda i, t: (i, t, 0))
    return pl.pallas_call(
        _da_out_kernel,
        grid=(b, l_lat // TOK_TILE),
        in_specs=[tokw, tokw,
                  pl.BlockSpec((1, TOK_TILE, d), lambda i, t: (i, t + 1, 0)),
                  pl.BlockSpec((1, 1, d), lambda i, t: (i, 0, 0)),
                  pl.BlockSpec((1, d), lambda i, t: (0, 0)),
                  _whole_vmem()],
        out_specs=pl.BlockSpec((1, TOK_TILE, d), lambda i, t: (i, t, 0)),
        out_shape=jax.ShapeDtypeStruct((b, l_lat, d), F32),
        compiler_params=_params(("arbitrary", "arbitrary")),
        name="da_out",
    )(o, gate, x1, gate_mod, final_g, w_out)


def kernel(x, c, ctx, c_ctx, w_ada, b_ada, norm_g, hg_w_in, hg_lb_logits, hg_norm_g, hg_w_out,
           da_w_in, da_lam_q1, da_lam_k1, da_lam_q2, da_lam_k2, da_subln_g, da_w_out, final_g):
    b, l_lat, d = x.shape
    assert ctx.shape[1] == TOK_TILE and l_lat % TOK_TILE == 0 and l_lat % GRID_W == 0
    assert w_ada.shape[0] == 2 and hg_w_in.shape[0] == 1 and da_w_in.shape[0] == 1
    assert hg_lb_logits.shape[0] == 2

    rows = -(-(b + 1) // 8) * 8
    cc = jnp.concatenate([c, c_ctx[None], jnp.zeros((rows - b - 1, d), F32)], axis=0)
    ada = _adaln(cc, w_ada, b_ada)

    def mods(layer):
        m = ada[layer].reshape(rows, 3, d)
        pair = jnp.stack([jnp.broadcast_to(m[b], (b, 3, d)), m[:b]], axis=1)
        return pair[:, :, 0:2], pair[:, :, 2:3]

    mod0, gmod0 = mods(0)
    mod1, gmod1 = mods(1)

    q, gf, gb, v, gate0 = _hg_proj(ctx, x, mod0, norm_g[0:1], hg_w_in[0].astype(BF16), hg_lb_logits)
    of, ob = _hg_scan(q, gf, gb, v)
    x1 = _hg_out(of, ob, gate0, ctx, x, gmod0, hg_norm_g, hg_w_out[0].astype(BF16))

    lambda_init = 0.8 - 0.6 * math.exp(-0.3 * 1)
    k, vt, qt, gate1 = _da_proj(x1, mod1, norm_g[1:2], da_w_in[0], l_lat)
    lam_vecs = jnp.concatenate([da_lam_q1, da_lam_k1, da_lam_q2, da_lam_k2], axis=0)
    o = _da_attn(k, vt, qt, lam_vecs, da_subln_g[0], lambda_init)
    return _da_out(o, gate1, x1, gmod1[:, 1], final_g[None], da_w_out[0].astype(BF16))
```

```python
import functools
import math

import numpy as np
import jax
import jax.numpy as jnp
from jax import lax
from jax.experimental import pallas as pl
from jax.experimental.pallas import tpu as pltpu

F32 = jnp.float32
BF16 = jnp.bfloat16

EPS = 1e-6
GRID_W = 64
ROPE_THETA = 10000.0
HEAD_W = 128
DA_DQK = 64
TOK_TILE = 256
SCAN_CHUNK = 64
SCAN_HEADS = 8
Q_TILE = 256
KEY_CHUNK = 256
BOUND_SLACK = 1.01
MIN_COLUMN_SUM = 2.0 ** -64
EXP2_CLAMP = 115.0
LOG2E = 1.4426950408889634
VMEM_LIMIT = 56 * 1024 * 1024

NT = (((1,), (1,)), ((), ()))
TN = (((0,), (0,)), ((), ()))


def _sigmoid(x):
    return 1.0 / (1.0 + jnp.exp(-x))


def _silu(x):
    return x * _sigmoid(x)


def _params(sem):
    return pltpu.CompilerParams(dimension_semantics=sem, vmem_limit_bytes=VMEM_LIMIT)


def _whole_vmem():
    return pl.BlockSpec(memory_space=pltpu.VMEM)


def _adaln_kernel(c_ref, w_ref, b_ref, o_ref):
    s = _silu(c_ref[...]).astype(BF16)
    o_ref[0] = jnp.dot(s, w_ref[0], preferred_element_type=F32) + b_ref[0]


def _adaln(cc, w_ada, b_ada):
    depth, d, d3 = w_ada.shape
    rows = cc.shape[0]
    return pl.pallas_call(
        _adaln_kernel,
        grid=(depth,),
        in_specs=[pl.BlockSpec((rows, d), lambda i: (0, 0)),
                  pl.BlockSpec((1, d, d3), lambda i: (i, 0, 0)),
                  pl.BlockSpec((1, 1, d3), lambda i: (i, 0, 0))],
        out_specs=pl.BlockSpec((1, rows, d3), lambda i: (i, 0, 0)),
        out_shape=jax.ShapeDtypeStruct((depth, rows, d3), F32),
        compiler_params=_params(("arbitrary",)),
        name="adaln",
    )(cc, w_ada.astype(BF16), b_ada.reshape(depth, 1, d3))


def _modulated_norm(x, g, shift, scale):
    y = x * lax.rsqrt(jnp.mean(x * x, axis=-1, keepdims=True) + EPS)
    return (y * g) * (1.0 + scale) + shift


def _hg_proj_kernel(ctx_ref, x_ref, mod_ref, ng_ref, w_ref, lbl_ref,
                    q_ref, gf_ref, gb_ref, v_ref, gate_ref):
    t = pl.program_id(1)
    x = jnp.where(t == 0, ctx_ref[0], x_ref[0])
    h = _modulated_norm(x, ng_ref[...], mod_ref[0, 0, 0:1, :], mod_ref[0, 0, 1:2, :]).astype(BF16)
    di = q_ref.shape[-1]

    def seg(j):
        return jnp.dot(h, w_ref[:, j * di:(j + 1) * di], preferred_element_type=F32)

    def log_forget(z, d):
        l0 = lbl_ref[0, d:d + 1, :]
        l1 = lbl_ref[1, d:d + 1, :]
        m = jnp.maximum(l0, l1)
        e0 = jnp.exp(l0 - m)
        lb = e0 / (e0 + jnp.exp(l1 - m))
        return jnp.log2(lb + (1.0 - lb) * _sigmoid(z))

    q_ref[0] = _silu(seg(0)).astype(BF16)
    gf_ref[0] = log_forget(seg(1), 0)
    gb_ref[0] = log_forget(seg(2), 1)
    v_ref[0] = seg(3).astype(BF16)
    gate_ref[0] = _silu(seg(4)).astype(BF16)


def _hg_proj(ctx, x, mod, norm_g, w_in, lb_logits):
    b, l, d = x.shape
    di = w_in.shape[1] // 5
    nt = l // TOK_TILE + 1
    tot = l + TOK_TILE
    tok = lambda dt: jax.ShapeDtypeStruct((b, tot, di), dt)
    out_blk = pl.BlockSpec((1, TOK_TILE, di), lambda i, t: (i, t, 0))
    return pl.pallas_call(
        _hg_proj_kernel,
        grid=(b, nt),
        in_specs=[pl.BlockSpec((1, TOK_TILE, d), lambda i, t: (i, 0, 0)),
                  pl.BlockSpec((1, TOK_TILE, d), lambda i, t: (i, jnp.maximum(t - 1, 0), 0)),
                  pl.BlockSpec((1, 1, 2, d), lambda i, t: (i, jnp.minimum(t, 1), 0, 0)),
                  pl.BlockSpec((1, d), lambda i, t: (0, 0)),
                  _whole_vmem(),
                  pl.BlockSpec((2, 2, di), lambda i, t: (0, 0, 0))],
        out_specs=[out_blk] * 5,
        out_shape=[tok(BF16), tok(F32), tok(F32), tok(BF16), tok(BF16)],
        compiler_params=_params(("arbitrary", "arbitrary")),
        name="hg_proj",
    )(ctx, x, mod, norm_g, w_in, lb_logits)


def _hg_scan_kernel(qf_ref, gf_ref, vf_ref, qb_ref, gb_ref, vb_ref, of_ref, ob_ref,
                    s_ref, bf_ref, bb_ref, tri_ref):
    t = pl.program_id(2)
    c = SCAN_CHUNK
    n_chunk = TOK_TILE // c
    w = SCAN_HEADS * HEAD_W

    @pl.when((pl.program_id(0) == 0) & (pl.program_id(1) == 0) & (t == 0))
    def _():
        row = lax.broadcasted_iota(jnp.int32, (TOK_TILE, TOK_TILE), 0)
        col = lax.broadcasted_iota(jnp.int32, (TOK_TILE, TOK_TILE), 1)
        same = (row // c) == (col // c)
        tri_ref[0] = jnp.where(same & (col <= row), 1.0, 0.0).astype(BF16)
        tri_ref[1] = jnp.where(same & (col >= row), 1.0, 0.0).astype(BF16)

    @pl.when(t == 0)
    def _():
        s_ref[...] = jnp.zeros_like(s_ref)

    def cumsum(tri, g):
        hi = g.astype(BF16)
        lo = (g - hi.astype(F32)).astype(BF16)
        return jnp.dot(tri, hi, preferred_element_type=F32) + jnp.dot(tri, lo, preferred_element_type=F32)

    bf_ref[...] = cumsum(tri_ref[0], gf_ref[0])
    bb_ref[...] = cumsum(tri_ref[1], gb_ref[0])

    def chunk_decay_columns(b_ref, last_row):
        rows = [b_ref[j * c + last_row:j * c + last_row + 1, :] for j in range(n_chunk)]
        rows.append(jnp.zeros((8 - n_chunk, w), F32))
        return jnp.exp2(jnp.concatenate(rows, axis=0).T)

    dec_f = chunk_decay_columns(bf_ref, c - 1)
    dec_b = chunk_decay_columns(bb_ref, 0)

    ri = lax.broadcasted_iota(jnp.int32, (c, c), 0)
    ci = lax.broadcasted_iota(jnp.int32, (c, c), 1)
    keep_f = ci <= ri
    keep_b = ci >= ri

    def chunk_head(direction, j, hh, q_ref, g_ref, v_ref, b_ref, o_ref, dec, keep, mid_row, last_row):
        rows = slice(j * c, (j + 1) * c)
        lanes = slice(hh * HEAD_W, (hh + 1) * HEAD_W)
        q = q_ref[0, rows, lanes]
        g = g_ref[0, rows, lanes]
        v = v_ref[0, rows, lanes]
        bcum = b_ref[rows, lanes]
        r = bcum[mid_row:mid_row + 1, :]
        b_last = bcum[last_row:last_row + 1, :]
        d = jnp.clip(bcum - r, -EXP2_CLAMP, EXP2_CLAMP)
        q_mid = q * jnp.exp2(d).astype(BF16)
        k_mid = ((1.0 - jnp.exp2(g)) * jnp.exp2(-d)).astype(BF16)
        q_dec = q_mid * jnp.exp2(r).astype(BF16)
        k_dec = k_mid * jnp.exp2(b_last - r).astype(BF16)
        a = lax.dot_general(q_mid, k_mid, NT, preferred_element_type=F32)
        a = jnp.where(keep, a, 0.0).astype(BF16)
        s = s_ref[direction, hh]
        o_ref[0, rows, lanes] = jnp.dot(jnp.concatenate([q_dec, a], axis=1),
                                        jnp.concatenate([s.astype(BF16), v], axis=0),
                                        preferred_element_type=F32).astype(o_ref.dtype)
        s_ref[direction, hh] = s * dec[lanes, j:j + 1] + lax.dot_general(
            k_dec, v, TN, preferred_element_type=F32)

    for i in range(n_chunk):
        for hh in range(SCAN_HEADS):
            chunk_head(0, i, hh, qf_ref, gf_ref, vf_ref, bf_ref, of_ref, dec_f, keep_f, c // 2 - 1, c - 1)
            chunk_head(1, n_chunk - 1 - i, hh, qb_ref, gb_ref, vb_ref, bb_ref, ob_ref, dec_b, keep_b, c // 2, 0)


def _hg_scan(q, gf, gb, v):
    b, tot, di = q.shape
    nt = tot // TOK_TILE
    w = SCAN_HEADS * HEAD_W
    fwd = pl.BlockSpec((1, TOK_TILE, w), lambda i, h, t: (i, t, h))
    bwd = pl.BlockSpec((1, TOK_TILE, w), lambda i, h, t: (i, jnp.where(t == 0, 0, nt - t), h))
    out = jax.ShapeDtypeStruct((b, tot, di), BF16)
    return pl.pallas_call(
        _hg_scan_kernel,
        grid=(b, di // w, nt),
        in_specs=[fwd, fwd, fwd, bwd, bwd, bwd],
        out_specs=[fwd, bwd],
        out_shape=[out, out],
        scratch_shapes=[pltpu.VMEM((2, SCAN_HEADS, HEAD_W, HEAD_W), F32),
                        pltpu.VMEM((TOK_TILE, w), F32),
                        pltpu.VMEM((TOK_TILE, w), F32),
                        pltpu.VMEM((2, TOK_TILE, TOK_TILE), BF16)],
        compiler_params=_params(("arbitrary", "arbitrary", "arbitrary")),
        name="hg_scan",
    )(q, gf, v, q, gb, v)


def _hg_out_kernel(of_ref, ob_ref, gate_ref, ctx_ref, x_ref, gm_ref, hng_ref, w_ref, o_ref, y_ref):
    t = pl.program_id(1)
    n_head = of_ref.shape[-1] // HEAD_W
    for hh in range(n_head):
        lanes = slice(hh * HEAD_W, (hh + 1) * HEAD_W)
        o = of_ref[0, :, lanes].astype(F32) + ob_ref[0, :, lanes].astype(F32)
        o = o * lax.rsqrt(jnp.mean(o * o, axis=-1, keepdims=True) + EPS) * hng_ref[:, lanes]
        y_ref[:, lanes] = (o * gate_ref[0, :, lanes].astype(F32)).astype(BF16)
    y = jnp.dot(y_ref[...], w_ref[...], preferred_element_type=F32)
    x = jnp.where(t == 0, ctx_ref[0], x_ref[0])
    o_ref[0] = x + gm_ref[0, 0] * y


def _hg_out(of, ob, gate, ctx, x, gate_mod, hg_norm_g, w_out):
    b, tot, di = of.shape
    d = x.shape[-1]
    nt = tot // TOK_TILE
    tokw = pl.BlockSpec((1, TOK_TILE, di), lambda i, t: (i, t, 0))
    return pl.pallas_call(
        _hg_out_kernel,
        grid=(b, nt),
        in_specs=[tokw, tokw, tokw,
                  pl.BlockSpec((1, TOK_TILE, d), lambda i, t: (i, 0, 0)),
                  pl.BlockSpec((1, TOK_TILE, d), lambda i, t: (i, jnp.maximum(t - 1, 0), 0)),
                  pl.BlockSpec((1, 1, 1, d), lambda i, t: (i, jnp.minimum(t, 1), 0, 0)),
                  pl.BlockSpec((1, di), lambda i, t: (0, 0)),
                  _whole_vmem()],
        out_specs=pl.BlockSpec((1, TOK_TILE, d), lambda i, t: (i, t, 0)),
        out_shape=jax.ShapeDtypeStruct((b, tot, d), F32),
        scratch_shapes=[pltpu.VMEM((TOK_TILE, di), BF16)],
        compiler_params=_params(("arbitrary", "arbitrary")),
        name="hg_out",
    )(of, ob, gate, ctx, x, gate_mod, hg_norm_g, w_out)


def _rope_lane_tables(l_lat):
    ax = DA_DQK // 2
    inv = 1.0 / (ROPE_THETA ** (np.arange(0, ax, 2, dtype=np.float64) / ax))
    pos = np.arange(l_lat)
    ang_r = (pos // GRID_W)[:, None] * inv
    ang_c = (pos % GRID_W)[:, None] * inv
    zero = np.zeros_like(ang_r)
    cos64 = np.concatenate([np.cos(ang_r), np.cos(ang_r), np.cos(ang_c), np.cos(ang_c)], axis=1)
    up64 = np.concatenate([-np.sin(ang_r), zero, -np.sin(ang_c), zero], axis=1)
    dn64 = np.concatenate([zero, np.sin(ang_r), zero, np.sin(ang_c)], axis=1)
    lat = [np.concatenate([t, t], axis=1) for t in (cos64, up64, dn64)]
    ident = [np.ones((TOK_TILE, HEAD_W)), np.zeros((TOK_TILE, HEAD_W)), np.zeros((TOK_TILE, HEAD_W))]
    return [np.concatenate([i, t], axis=0).astype(np.float32) for i, t in zip(ident, lat)], (cos64, up64, dn64)


def _da_proj_kernel(x_ref, mod_ref, ng_ref, wq_ref, wk_ref, wv_ref, wg_ref,
                    kc_ref, ku_ref, kd_ref, qc_ref, qs_ref,
                    k_ref, vt_ref, qt_ref, gate_ref):
    t = pl.program_id(1)
    h = _modulated_norm(x_ref[0], ng_ref[...], mod_ref[0, 0, 0:1, :], mod_ref[0, 0, 1:2, :]).astype(BF16)
    n_head = k_ref.shape[-1] // HEAD_W

    k = jnp.dot(h, wk_ref[...], preferred_element_type=F32)
    cos, s_up, s_dn = kc_ref[...], ku_ref[...], kd_ref[...]
    for hh in range(n_head):
        lanes = slice(hh * HEAD_W, (hh + 1) * HEAD_W)
        kh = k[:, lanes]
        rot = (kh * cos + pltpu.roll(kh, HEAD_W - 16, axis=1) * s_up
               + pltpu.roll(kh, 16, axis=1) * s_dn)
        k_ref[0, :, lanes] = rot.astype(BF16)

    vt_ref[0] = lax.dot_general(wv_ref[...], h, NT, preferred_element_type=F32).astype(BF16)

    @pl.when(t > 0)
    def _():
        qt = lax.dot_general(wq_ref[...], h, NT, preferred_element_type=F32)
        qc, qs = qc_ref[...], qs_ref[...]
        for grp in range(qt.shape[0] // DA_DQK):
            x = qt[grp * DA_DQK:(grp + 1) * DA_DQK]
            partner = jnp.concatenate([x[16:32], x[0:16], x[48:64], x[32:48]], axis=0)
            qt_ref[0, grp * DA_DQK:(grp + 1) * DA_DQK, :] = (x * qc + partner * qs).astype(BF16)
        gate_ref[0] = _silu(jnp.dot(h, wg_ref[...], preferred_element_type=F32)).astype(BF16)


def _da_proj(x1, mod, norm_g, w_in, l_lat):
    b, tot, d = x1.shape
    di = w_in.shape[1] // 4
    nt = tot // TOK_TILE
    wq, wk, wv, wg = (w_in[:, j * di:(j + 1) * di] for j in range(4))
    ktabs, (cos64, up64, dn64) = _rope_lane_tables(l_lat)
    q_scale = DA_DQK ** -0.5 * LOG2E
    qc = (cos64.T * q_scale).astype(np.float32)
    qs = ((up64 + dn64).T * q_scale).astype(np.float32)
    ktab = pl.BlockSpec((TOK_TILE, HEAD_W), lambda i, t: (t, 0))
    qtab = pl.BlockSpec((DA_DQK, TOK_TILE), lambda i, t: (0, jnp.maximum(t - 1, 0)))
    return pl.pallas_call(
        _da_proj_kernel,
        grid=(b, nt),
        in_specs=[pl.BlockSpec((1, TOK_TILE, d), lambda i, t: (i, t, 0)),
                  pl.BlockSpec((1, 1, 2, d), lambda i, t: (i, jnp.minimum(t, 1), 0, 0)),
                  pl.BlockSpec((1, d), lambda i, t: (0, 0)),
                  _whole_vmem(), _whole_vmem(), _whole_vmem(), _whole_vmem(),
                  ktab, ktab, ktab, qtab, qtab],
        out_specs=[pl.BlockSpec((1, TOK_TILE, di), lambda i, t: (i, t, 0)),
                   pl.BlockSpec((1, di, TOK_TILE), lambda i, t: (i, 0, t)),
                   pl.BlockSpec((1, di, TOK_TILE), lambda i, t: (i, 0, jnp.maximum(t - 1, 0))),
                   pl.BlockSpec((1, TOK_TILE, di), lambda i, t: (i, jnp.maximum(t - 1, 0), 0))],
        out_shape=[jax.ShapeDtypeStruct((b, tot, di), BF16),
                   jax.ShapeDtypeStruct((b, di, tot), BF16),
                   jax.ShapeDtypeStruct((b, di, l_lat), BF16),
                   jax.ShapeDtypeStruct((b, l_lat, di), BF16)],
        compiler_params=_params(("arbitrary", "arbitrary")),
        name="da_proj",
    )(x1, mod, norm_g, wq.T.astype(BF16), wk.astype(BF16), wv.T.astype(BF16), wg.astype(BF16),
      *[jnp.asarray(a) for a in ktabs], jnp.asarray(qc), jnp.asarray(qs))


def _da_attn_kernel(k_ref, vt_ref, qt_ref, lam_ref, sg_ref, o_ref, pa_ref, pb_ref, a_ref, la_ref, lb_ref,
                    *, lambda_init):
    lq1, lk1, lq2, lk2 = (lam_ref[i:i + 1, :] for i in range(4))
    lam = (jnp.exp(jnp.sum(lq1 * lk1, axis=-1, keepdims=True))
           - jnp.exp(jnp.sum(lq2 * lk2, axis=-1, keepdims=True)) + lambda_init)
    out_gain = sg_ref[...] * (1.0 - lambda_init)
    n_chunk = k_ref.shape[1] // KEY_CHUNK
    n_q = qt_ref.shape[-1] // Q_TILE
    w2 = 2 * Q_TILE

    kf = k_ref[0].astype(F32)
    sel_r = lax.broadcasted_iota(jnp.int32, (HEAD_W, HEAD_W), 0) // DA_DQK
    sel_c = lax.broadcasted_iota(jnp.int32, (HEAD_W, HEAD_W), 1)
    sel = jnp.where(sel_r == sel_c, 1.0, 0.0).astype(BF16)
    norm2 = jnp.dot((kf * kf).astype(BF16), sel, preferred_element_type=F32)
    kmax = jnp.sqrt(jnp.max(norm2, axis=0, keepdims=True)) * BOUND_SLACK
    kmax = jnp.concatenate([jnp.broadcast_to(kmax[:, 0:1], (1, Q_TILE)),
                            jnp.broadcast_to(kmax[:, 1:2], (1, Q_TILE))], axis=1)
    half0 = lax.broadcasted_iota(jnp.int32, (HEAD_W, Q_TILE), 0) < DA_DQK

    def fold(x):
        return x.reshape(KEY_CHUNK // 8, 8, w2)

    def exp_stage(i, p_ref, l_ref, overlapped):
        start = i * Q_TILE if isinstance(i, int) else pl.multiple_of(i * Q_TILE, Q_TILE)
        qt = qt_ref[0, :, pl.ds(start, Q_TILE)]
        zero = jnp.zeros_like(qt)
        rhs = jnp.concatenate([jnp.where(half0, qt, zero), jnp.where(half0, zero, qt)], axis=1)
        rf = rhs.astype(F32)
        m_bound = jnp.sqrt(jnp.sum(rf * rf, axis=0, keepdims=True)) * kmax

        def scores(c):
            return jnp.dot(k_ref[0, c * KEY_CHUNK:(c + 1) * KEY_CHUNK, :], rhs, preferred_element_type=F32)

        def exp_pass(m):
            l8 = jnp.zeros((8, w2), F32)
            for c in range(n_chunk):
                p = jnp.exp2(scores(c) - m)
                l8 = l8 + jnp.sum(fold(p), axis=0)
                p_ref[c * KEY_CHUNK:(c + 1) * KEY_CHUNK, :] = p.astype(BF16)
            return l8

        l8 = exp_pass(m_bound)
        overlapped()
        l_ref[...] = l8

        @pl.when(jnp.logical_not(jnp.min(l8) > MIN_COLUMN_SUM))
        def _():
            mx = jnp.full((8, w2), -jnp.inf, F32)
            for c in range(n_chunk):
                mx = jnp.maximum(mx, jnp.max(fold(scores(c)), axis=0))
            l_ref[...] = exp_pass(jnp.max(mx, axis=0, keepdims=True))

    def value_stage(i, p_ref, l_ref):
        start = i * Q_TILE if isinstance(i, int) else pl.multiple_of(i * Q_TILE, Q_TILE)
        l = jnp.sum(l_ref[...], axis=0, keepdims=True)
        l0, l1 = l[:, :Q_TILE], l[:, Q_TILE:]
        rho = (lam * l0 / l1).astype(BF16)
        for c in range(n_chunk):
            rows = slice(c * KEY_CHUNK, (c + 1) * KEY_CHUNK)
            a_ref[rows, :] = p_ref[rows, :Q_TILE] - p_ref[rows, Q_TILE:] * rho
        ot = jnp.dot(vt_ref[0], a_ref[...], preferred_element_type=F32) * (1.0 / l0)
        ot = ot * lax.rsqrt(jnp.mean(ot * ot, axis=0, keepdims=True) + EPS) * out_gain
        o_ref[0, pl.ds(start, Q_TILE), :] = ot.T.astype(o_ref.dtype)

    even = (pa_ref, la_ref)
    odd = (pb_ref, lb_ref)
    exp_stage(0, *even, lambda: None)

    def body(j, carry):
        i = 2 * j
        exp_stage(i + 1, *odd, lambda: value_stage(i, *even))
        exp_stage(i + 2, *even, lambda: value_stage(i + 1, *odd))
        return carry

    lax.fori_loop(0, n_q // 2 - 1, body, 0)
    exp_stage(n_q - 1, *odd, lambda: value_stage(n_q - 2, *even))
    value_stage(n_q - 1, *odd)


def _da_attn(k, vt, qt, lam_vecs, subln_g, lambda_init):
    b, tot, di = k.shape
    l_lat = qt.shape[-1]
    n_head = di // HEAD_W
    return pl.pallas_call(
        functools.partial(_da_attn_kernel, lambda_init=lambda_init),
        grid=(b, n_head),
        in_specs=[pl.BlockSpec((1, tot, HEAD_W), lambda i, h: (i, 0, h)),
                  pl.BlockSpec((1, HEAD_W, tot), lambda i, h: (i, h, 0)),
                  pl.BlockSpec((1, HEAD_W, l_lat), lambda i, h: (i, h, 0)),
                  pl.BlockSpec((4, DA_DQK), lambda i, h: (0, 0)),
                  pl.BlockSpec((HEAD_W, 1), lambda i, h: (0, 0))],
        out_specs=pl.BlockSpec((1, l_lat, HEAD_W), lambda i, h: (i, 0, h)),
        out_shape=jax.ShapeDtypeStruct((b, l_lat, di), BF16),
        scratch_shapes=[pltpu.VMEM((tot, 2 * Q_TILE), BF16),
                        pltpu.VMEM((tot, 2 * Q_TILE), BF16),
                        pltpu.VMEM((tot, Q_TILE), BF16),
                        pltpu.VMEM((8, 2 * Q_TILE), F32),
                        pltpu.VMEM((8, 2 * Q_TILE), F32)],
        compiler_params=_params(("arbitrary", "arbitrary")),
        name="da_attn",
    )(k, vt, qt, lam_vecs, subln_g.reshape(HEAD_W, 1))


def _da_out_kernel(o_ref, gate_ref, x_ref, gm_ref, fg_ref, w_ref, out_ref):
    y = o_ref[0] * gate_ref[0]
    x = x_ref[0] + gm_ref[0] * jnp.dot(y, w_ref[...], preferred_element_type=F32)
    out_ref[0] = x * lax.rsqrt(jnp.mean(x * x, axis=-1, keepdims=True) + EPS) * fg_ref[...]


def _da_out(o, gate, x1, gate_mod, final_g, w_out):
    b, l_lat, di = o.shape
    d = x1.shape[-1]
    tokw = pl.BlockSpec((1, TOK_TILE, di), lambda i, t: (i, t, 0))
    return pl.pallas_call(
        _da_out_kernel,
        grid=(b, l_lat // TOK_TILE),
        in_specs=[tokw, tokw,
                  pl.BlockSpec((1, TOK_TILE, d), lambda i, t: (i, t + 1, 0)),
                  pl.BlockSpec((1, 1, d), lambda i, t: (i, 0, 0)),
                  pl.BlockSpec((1, d), lambda i, t: (0, 0)),
                  _whole_vmem()],
        out_specs=pl.BlockSpec((1, TOK_TILE, d), lambda i, t: (i, t, 0)),
        out_shape=jax.ShapeDtypeStruct((b, l_lat, d), F32),
        compiler_params=_params(("arbitrary", "arbitrary")),
        name="da_out",
    )(o, gate, x1, gate_mod, final_g, w_out)


def kernel(x, c, ctx, c_ctx, w_ada, b_ada, norm_g, hg_w_in, hg_lb_logits, hg_norm_g, hg_w_out,
           da_w_in, da_lam_q1, da_lam_k1, da_lam_q2, da_lam_k2, da_subln_g, da_w_out, final_g):
    b, l_lat, d = x.shape
    assert ctx.shape[1] == TOK_TILE and l_lat % TOK_TILE == 0 and l_lat % GRID_W == 0
    assert w_ada.shape[0] == 2 and hg_w_in.shape[0] == 1 and da_w_in.shape[0] == 1
    assert hg_lb_logits.shape[0] == 2

    rows = -(-(b + 1) // 8) * 8
    cc = jnp.concatenate([c, c_ctx[None], jnp.zeros((rows - b - 1, d), F32)], axis=0)
    ada = _adaln(cc, w_ada, b_ada)

    def mods(layer):
        m = ada[layer].reshape(rows, 3, d)
        pair = jnp.stack([jnp.broadcast_to(m[b], (b, 3, d)), m[:b]], axis=1)
        return pair[:, :, 0:2], pair[:, :, 2:3]

    mod0, gmod0 = mods(0)
    mod1, gmod1 = mods(1)

    q, gf, gb, v, gate0 = _hg_proj(ctx, x, mod0, norm_g[0:1], hg_w_in[0].astype(BF16), hg_lb_logits)
    of, ob = _hg_scan(q, gf, gb, v)
    x1 = _hg_out(of, ob, gate0, ctx, x, gmod0, hg_norm_g, hg_w_out[0].astype(BF16))

    lambda_init = 0.8 - 0.6 * math.exp(-0.3 * 1)
    k, vt, qt, gate1 = _da_proj(x1, mod1, norm_g[1:2], da_w_in[0], l_lat)
    lam_vecs = jnp.concatenate([da_lam_q1, da_lam_k1, da_lam_q2, da_lam_k2], axis=0)
    o = _da_attn(k, vt, qt, lam_vecs, da_subln_g[0], lambda_init)
    return _da_out(o, gate1, x1, gmod1[:, 1], final_g[None], da_w_out[0].astype(BF16))
```

```python
import functools
import math

import numpy as np
import jax
import jax.numpy as jnp
from jax import lax
from jax.experimental import pallas as pl
from jax.experimental.pallas import tpu as pltpu

F32 = jnp.float32
BF16 = jnp.bfloat16

EPS = 1e-6
GRID_W = 64
ROPE_THETA = 10000.0
HEAD_W = 128
DA_DQK = 64
TOK_TILE = 256
SCAN_CHUNK = 64
SCAN_HEADS = 8
Q_TILE = 256
KEY_CHUNK = 256
BOUND_SLACK = 1.01
MIN_COLUMN_SUM = 2.0 ** -64
EXP2_CLAMP = 115.0
LOG2E = 1.4426950408889634
VMEM_LIMIT = 56 * 1024 * 1024

NT = (((1,), (1,)), ((), ()))
TN = (((0,), (0,)), ((), ()))


def _sigmoid(x):
    return 1.0 / (1.0 + jnp.exp(-x))


def _silu(x):
    return x * _sigmoid(x)


def _params(sem):
    return pltpu.CompilerParams(dimension_semantics=sem, vmem_limit_bytes=VMEM_LIMIT)


def _whole_vmem():
    return pl.BlockSpec(memory_space=pltpu.VMEM)


def _adaln_kernel(c_ref, w_ref, b_ref, o_ref):
    s = _silu(c_ref[...]).astype(BF16)
    o_ref[0] = jnp.dot(s, w_ref[0], preferred_element_type=F32) + b_ref[0]


def _adaln(cc, w_ada, b_ada):
    depth, d, d3 = w_ada.shape
    rows = cc.shape[0]
    return pl.pallas_call(
        _adaln_kernel,
        grid=(depth,),
        in_specs=[pl.BlockSpec((rows, d), lambda i: (0, 0)),
                  pl.BlockSpec((1, d, d3), lambda i: (i, 0, 0)),
                  pl.BlockSpec((1, 1, d3), lambda i: (i, 0, 0))],
        out_specs=pl.BlockSpec((1, rows, d3), lambda i: (i, 0, 0)),
        out_shape=jax.ShapeDtypeStruct((depth, rows, d3), F32),
        compiler_params=_params(("arbitrary",)),
        name="adaln",
    )(cc, w_ada.astype(BF16), b_ada.reshape(depth, 1, d3))


def _modulated_norm(x, g, shift, scale):
    y = x * lax.rsqrt(jnp.mean(x * x, axis=-1, keepdims=True) + EPS)
    return (y * g) * (1.0 + scale) + shift


def _hg_proj_kernel(ctx_ref, x_ref, mod_ref, ng_ref, w_ref, lbl_ref,
                    q_ref, gf_ref, gb_ref, v_ref, gate_ref):
    t = pl.program_id(1)
    x = jnp.where(t == 0, ctx_ref[0], x_ref[0])
    h = _modulated_norm(x, ng_ref[...], mod_ref[0, 0, 0:1, :], mod_ref[0, 0, 1:2, :]).astype(BF16)
    di = q_ref.shape[-1]

    def seg(j):
        return jnp.dot(h, w_ref[:, j * di:(j + 1) * di], preferred_element_type=F32)

    def log_forget(z, d):
        l0 = lbl_ref[0, d:d + 1, :]
        l1 = lbl_ref[1, d:d + 1, :]
        m = jnp.maximum(l0, l1)
        e0 = jnp.exp(l0 - m)
        lb = e0 / (e0 + jnp.exp(l1 - m))
        return jnp.log2(lb + (1.0 - lb) * _sigmoid(z))

    q_ref[0] = _silu(seg(0)).astype(BF16)
    gf_ref[0] = log_forget(seg(1), 0)
    gb_ref[0] = log_forget(seg(2), 1)
    v_ref[0] = seg(3).astype(BF16)
    gate_ref[0] = _silu(seg(4)).astype(BF16)


def _hg_proj(ctx, x, mod, norm_g, w_in, lb_logits):
    b, l, d = x.shape
    di = w_in.shape[1] // 5
    nt = l // TOK_TILE + 1
    tot = l + TOK_TILE
    tok = lambda dt: jax.ShapeDtypeStruct((b, tot, di), dt)
    out_blk = pl.BlockSpec((1, TOK_TILE, di), lambda i, t: (i, t, 0))
    return pl.pallas_call(
        _hg_proj_kernel,
        grid=(b, nt),
        in_specs=[pl.BlockSpec((1, TOK_TILE, d), lambda i, t: (i, 0, 0)),
                  pl.BlockSpec((1, TOK_TILE, d), lambda i, t: (i, jnp.maximum(t - 1, 0), 0)),
                  pl.BlockSpec((1, 1, 2, d), lambda i, t: (i, jnp.minimum(t, 1), 0, 0)),
                  pl.BlockSpec((1, d), lambda i, t: (0, 0)),
                  _whole_vmem(),
                  pl.BlockSpec((2, 2, di), lambda i, t: (0, 0, 0))],
        out_specs=[out_blk] * 5,
        out_shape=[tok(BF16), tok(F32), tok(F32), tok(BF16), tok(BF16)],
        compiler_params=_params(("arbitrary", "arbitrary")),
        name="hg_proj",
    )(ctx, x, mod, norm_g, w_in, lb_logits)


def _hg_scan_kernel(qf_ref, gf_ref, vf_ref, qb_ref, gb_ref, vb_ref, of_ref, ob_ref,
                    s_ref, bf_ref, bb_ref, tri_ref):
    t = pl.program_id(2)
    c = SCAN_CHUNK
    n_chunk = TOK_TILE // c
    w = SCAN_HEADS * HEAD_W

    @pl.when((pl.program_id(0) == 0) & (pl.program_id(1) == 0) & (t == 0))
    def _():
        row = lax.broadcasted_iota(jnp.int32, (TOK_TILE, TOK_TILE), 0)
        col = lax.broadcasted_iota(jnp.int32, (TOK_TILE, TOK_TILE), 1)
        same = (row // c) == (col // c)
        tri_ref[0] = jnp.where(same & (col <= row), 1.0, 0.0).astype(BF16)
        tri_ref[1] = jnp.where(same & (col >= row), 1.0, 0.0).astype(BF16)

    @pl.when(t == 0)
    def _():
        s_ref[...] = jnp.zeros_like(s_ref)

    def cumsum(tri, g):
        hi = g.astype(BF16)
        lo = (g - hi.astype(F32)).astype(BF16)
        return jnp.dot(tri, hi, preferred_element_type=F32) + jnp.dot(tri, lo, preferred_element_type=F32)

    bf_ref[...] = cumsum(tri_ref[0], gf_ref[0])
    bb_ref[...] = cumsum(tri_ref[1], gb_ref[0])

    def chunk_decay_columns(b_ref, last_row):
        rows = [b_ref[j * c + last_row:j * c + last_row + 1, :] for j in range(n_chunk)]
        rows.append(jnp.zeros((8 - n_chunk, w), F32))
        return jnp.exp2(jnp.concatenate(rows, axis=0).T)

    dec_f = chunk_decay_columns(bf_ref, c - 1)
    dec_b = chunk_decay_columns(bb_ref, 0)

    ri = lax.broadcasted_iota(jnp.int32, (c, c), 0)
    ci = lax.broadcasted_iota(jnp.int32, (c, c), 1)
    keep_f = ci <= ri
    keep_b = ci >= ri

    def chunk_head(direction, j, hh, q_ref, g_ref, v_ref, b_ref, o_ref, dec, keep, mid_row, last_row):
        rows = slice(j * c, (j + 1) * c)
        lanes = slice(hh * HEAD_W, (hh + 1) * HEAD_W)
        q = q_ref[0, rows, lanes]
        g = g_ref[0, rows, lanes]
        v = v_ref[0, rows, lanes]
        bcum = b_ref[rows, lanes]
        r = bcum[mid_row:mid_row + 1, :]
        b_last = bcum[last_row:last_row + 1, :]
        d = jnp.clip(bcum - r, -EXP2_CLAMP, EXP2_CLAMP)
        q_mid = q * jnp.exp2(d).astype(BF16)
        k_mid = ((1.0 - jnp.exp2(g)) * jnp.exp2(-d)).astype(BF16)
        q_dec = q_mid * jnp.exp2(r).astype(BF16)
        k_dec = k_mid * jnp.exp2(b_last - r).astype(BF16)
        a = lax.dot_general(q_mid, k_mid, NT, preferred_element_type=F32)
        a = jnp.where(keep, a, 0.0).astype(BF16)
        s = s_ref[direction, hh]
        o_ref[0, rows, lanes] = jnp.dot(jnp.concatenate([q_dec, a], axis=1),
                                        jnp.concatenate([s.astype(BF16), v], axis=0),
                                        preferred_element_type=F32).astype(o_ref.dtype)
        s_ref[direction, hh] = s * dec[lanes, j:j + 1] + lax.dot_general(
            k_dec, v, TN, preferred_element_type=F32)

    for i in range(n_chunk):
        for hh in range(SCAN_HEADS):
            chunk_head(0, i, hh, qf_ref, gf_ref, vf_ref, bf_ref, of_ref, dec_f, keep_f, c // 2 - 1, c - 1)
            chunk_head(1, n_chunk - 1 - i, hh, qb_ref, gb_ref, vb_ref, bb_ref, ob_ref, dec_b, keep_b, c // 2, 0)


def _hg_scan(q, gf, gb, v):
    b, tot, di = q.shape
    nt = tot // TOK_TILE
    w = SCAN_HEADS * HEAD_W
    fwd = pl.BlockSpec((1, TOK_TILE, w), lambda i, h, t: (i, t, h))
    bwd = pl.BlockSpec((1, TOK_TILE, w), lambda i, h, t: (i, jnp.where(t == 0, 0, nt - t), h))
    out = jax.ShapeDtypeStruct((b, tot, di), BF16)
    return pl.pallas_call(
        _hg_scan_kernel,
        grid=(b, di // w, nt),
        in_specs=[fwd, fwd, fwd, bwd, bwd, bwd],
        out_specs=[fwd, bwd],
        out_shape=[out, out],
        scratch_shapes=[pltpu.VMEM((2, SCAN_HEADS, HEAD_W, HEAD_W), F32),
                        pltpu.VMEM((TOK_TILE, w), F32),
                        pltpu.VMEM((TOK_TILE, w), F32),
                        pltpu.VMEM((2, TOK_TILE, TOK_TILE), BF16)],
        compiler_params=_params(("arbitrary", "arbitrary", "arbitrary")),
        name="hg_scan",
    )(q, gf, v, q, gb, v)


def _hg_out_kernel(of_ref, ob_ref, gate_ref, ctx_ref, x_ref, gm_ref, hng_ref, w_ref, o_ref, y_ref):
    t = pl.program_id(1)
    n_head = of_ref.shape[-1] // HEAD_W
    for hh in range(n_head):
        lanes = slice(hh * HEAD_W, (hh + 1) * HEAD_W)
        o = of_ref[0, :, lanes].astype(F32) + ob_ref[0, :, lanes].astype(F32)
        o = o * lax.rsqrt(jnp.mean(o * o, axis=-1, keepdims=True) + EPS) * hng_ref[:, lanes]
        y_ref[:, lanes] = (o * gate_ref[0, :, lanes].astype(F32)).astype(BF16)
    y = jnp.dot(y_ref[...], w_ref[...], preferred_element_type=F32)
    x = jnp.where(t == 0, ctx_ref[0], x_ref[0])
    o_ref[0] = x + gm_ref[0, 0] * y


def _hg_out(of, ob, gate, ctx, x, gate_mod, hg_norm_g, w_out):
    b, tot, di = of.shape
    d = x.shape[-1]
    nt = tot // TOK_TILE
    tokw = pl.BlockSpec((1, TOK_TILE, di), lambda i, t: (i, t, 0))
    return pl.pallas_call(
        _hg_out_kernel,
        grid=(b, nt),
        in_specs=[tokw, tokw, tokw,
                  pl.BlockSpec((1, TOK_TILE, d), lambda i, t: (i, 0, 0)),
                  pl.BlockSpec((1, TOK_TILE, d), lambda i, t: (i, jnp.maximum(t - 1, 0), 0)),
                  pl.BlockSpec((1, 1, 1, d), lambda i, t: (i, jnp.minimum(t, 1), 0, 0)),
                  pl.BlockSpec((1, di), lambda i, t: (0, 0)),
                  _whole_vmem()],
        out_specs=pl.BlockSpec((1, TOK_TILE, d), lambda i, t: (i, t, 0)),
        out_shape=jax.ShapeDtypeStruct((b, tot, d), F32),
        scratch_shapes=[pltpu.VMEM((TOK_TILE, di), BF16)],
        compiler_params=_params(("arbitrary", "arbitrary")),
        name="hg_out",
    )(of, ob, gate, ctx, x, gate_mod, hg_norm_g, w_out)


def _rope_lane_tables(l_lat):
    ax = DA_DQK // 2
    inv = 1.0 / (ROPE_THETA ** (np.arange(0, ax, 2, dtype=np.float64) / ax))
    pos = np.arange(l_lat)
    ang_r = (pos // GRID_W)[:, None] * inv
    ang_c = (pos % GRID_W)[:, None] * inv
    zero = np.zeros_like(ang_r)
    cos64 = np.concatenate([np.cos(ang_r), np.cos(ang_r), np.cos(ang_c), np.cos(ang_c)], axis=1)
    up64 = np.concatenate([-np.sin(ang_r), zero, -np.sin(ang_c), zero], axis=1)
    dn64 = np.concatenate([zero, np.sin(ang_r), zero, np.sin(ang_c)], axis=1)
    lat = [np.concatenate([t, t], axis=1) for t in (cos64, up64, dn64)]
    ident = [np.ones((TOK_TILE, HEAD_W)), np.zeros((TOK_TILE, HEAD_W)), np.zeros((TOK_TILE, HEAD_W))]
    return [np.concatenate([i, t], axis=0).astype(np.float32) for i, t in zip(ident, lat)], (cos64, up64, dn64)


def _da_proj_kernel(x_ref, mod_ref, ng_ref, wq_ref, wk_ref, wv_ref, wg_ref,
                    kc_ref, ku_ref, kd_ref, qc_ref, qs_ref,
                    k_ref, vt_ref, qt_ref, gate_ref):
    t = pl.program_id(1)
    h = _modulated_norm(x_ref[0], ng_ref[...], mod_ref[0, 0, 0:1, :], mod_ref[0, 0, 1:2, :]).astype(BF16)
    n_head = k_ref.shape[-1] // HEAD_W

    k = jnp.dot(h, wk_ref[...], preferred_element_type=F32)
    cos, s_up, s_dn = kc_ref[...], ku_ref[...], kd_ref[...]
    for hh in range(n_head):
        lanes = slice(hh * HEAD_W, (hh + 1) * HEAD_W)
        kh = k[:, lanes]
        rot = (kh * cos + pltpu.roll(kh, HEAD_W - 16, axis=1) * s_up
               + pltpu.roll(kh, 16, axis=1) * s_dn)
        k_ref[0, :, lanes] = rot.astype(BF16)

    vt_ref[0] = lax.dot_general(wv_ref[...], h, NT, preferred_element_type=F32).astype(BF16)

    @pl.when(t > 0)
    def _():
        qt = lax.dot_general(wq_ref[...], h, NT, preferred_element_type=F32)
        qc, qs = qc_ref[...], qs_ref[...]
        for grp in range(qt.shape[0] // DA_DQK):
            x = qt[grp * DA_DQK:(grp + 1) * DA_DQK]
            partner = jnp.concatenate([x[16:32], x[0:16], x[48:64], x[32:48]], axis=0)
            qt_ref[0, grp * DA_DQK:(grp + 1) * DA_DQK, :] = (x * qc + partner * qs).astype(BF16)
        gate_ref[0] = _silu(jnp.dot(h, wg_ref[...], preferred_element_type=F32)).astype(BF16)


def _da_proj(x1, mod, norm_g, w_in, l_lat):
    b, tot, d = x1.shape
    di = w_in.shape[1] // 4
    nt = tot // TOK_TILE
    wq, wk, wv, wg = (w_in[:, j * di:(j + 1) * di] for j in range(4))
    ktabs, (cos64, up64, dn64) = _rope_lane_tables(l_lat)
    q_scale = DA_DQK ** -0.5 * LOG2E
    qc = (cos64.T * q_scale).astype(np.float32)
    qs = ((up64 + dn64).T * q_scale).astype(np.float32)
    ktab = pl.BlockSpec((TOK_TILE, HEAD_W), lambda i, t: (t, 0))
    qtab = pl.BlockSpec((DA_DQK, TOK_TILE), lambda i, t: (0, jnp.maximum(t - 1, 0)))
    return pl.pallas_call(
        _da_proj_kernel,
        grid=(b, nt),
        in_specs=[pl.BlockSpec((1, TOK_TILE, d), lambda i, t: (i, t, 0)),
                  pl.BlockSpec((1, 1, 2, d), lambda i, t: (i, jnp.minimum(t, 1), 0, 0)),
                  pl.BlockSpec((1, d), lambda i, t: (0, 0)),
                  _whole_vmem(), _whole_vmem(), _whole_vmem(), _whole_vmem(),
                  ktab, ktab, ktab, qtab, qtab],
        out_specs=[pl.BlockSpec((1, TOK_TILE, di), lambda i, t: (i, t, 0)),
                   pl.BlockSpec((1, di, TOK_TILE), lambda i, t: (i, 0, t)),
                   pl.BlockSpec((1, di, TOK_TILE), lambda i, t: (i, 0, jnp.maximum(t - 1, 0))),
                   pl.BlockSpec((1, TOK_TILE, di), lambda i, t: (i, jnp.maximum(t - 1, 0), 0))],
        out_shape=[jax.ShapeDtypeStruct((b, tot, di), BF16),
                   jax.ShapeDtypeStruct((b, di, tot), BF16),
                   jax.ShapeDtypeStruct((b, di, l_lat), BF16),
                   jax.ShapeDtypeStruct((b, l_lat, di), BF16)],
        compiler_params=_params(("arbitrary", "arbitrary")),
        name="da_proj",
    )(x1, mod, norm_g, wq.T.astype(BF16), wk.astype(BF16), wv.T.astype(BF16), wg.astype(BF16),
      *[jnp.asarray(a) for a in ktabs], jnp.asarray(qc), jnp.asarray(qs))


def _da_attn_kernel(k_ref, vt_ref, qt_ref, lam_ref, sg_ref, o_ref, pa_ref, pb_ref, a_ref, la_ref, lb_ref,
                    *, lambda_init):
    lq1, lk1, lq2, lk2 = (lam_ref[i:i + 1, :] for i in range(4))
    lam = (jnp.exp(jnp.sum(lq1 * lk1, axis=-1, keepdims=True))
           - jnp.exp(jnp.sum(lq2 * lk2, axis=-1, keepdims=True)) + lambda_init)
    out_gain = sg_ref[...] * (1.0 - lambda_init)
    n_chunk = k_ref.shape[1] // KEY_CHUNK
    n_q = qt_ref.shape[-1] // Q_TILE
    w2 = 2 * Q_TILE

    kf = k_ref[0].astype(F32)
    sel_r = lax.broadcasted_iota(jnp.int32, (HEAD_W, HEAD_W), 0) // DA_DQK
    sel_c = lax.broadcasted_iota(jnp.int32, (HEAD_W, HEAD_W), 1)
    sel = jnp.where(sel_r == sel_c, 1.0, 0.0).astype(BF16)
    norm2 = jnp.dot((kf * kf).astype(BF16), sel, preferred_element_type=F32)
    kmax = jnp.sqrt(jnp.max(norm2, axis=0, keepdims=True)) * BOUND_SLACK
    kmax = jnp.concatenate([jnp.broadcast_to(kmax[:, 0:1], (1, Q_TILE)),
                            jnp.broadcast_to(kmax[:, 1:2], (1, Q_TILE))], axis=1)
    half0 = lax.broadcasted_iota(jnp.int32, (HEAD_W, Q_TILE), 0) < DA_DQK

    def fold(x):
        return x.reshape(KEY_CHUNK // 8, 8, w2)

    def exp_stage(i, p_ref, l_ref, exact):
        start = i * Q_TILE if isinstance(i, int) else pl.multiple_of(i * Q_TILE, Q_TILE)
        qt = qt_ref[0, :, pl.ds(start, Q_TILE)]
        zero = jnp.zeros_like(qt)
        rhs = jnp.concatenate([jnp.where(half0, qt, zero), jnp.where(half0, zero, qt)], axis=1)

        def scores(c):
            return jnp.dot(k_ref[0, c * KEY_CHUNK:(c + 1) * KEY_CHUNK, :], rhs, preferred_element_type=F32)

        if exact:
            mx = jnp.full((8, w2), -jnp.inf, F32)
            for c in range(n_chunk):
                mx = jnp.maximum(mx, jnp.max(fold(scores(c)), axis=0))
            m = jnp.max(mx, axis=0, keepdims=True)
        else:
            rf = rhs.astype(F32)
            m = jnp.sqrt(jnp.sum(rf * rf, axis=0, keepdims=True)) * kmax
        l8 = jnp.zeros((8, w2), F32)
        for c in range(n_chunk):
            p = jnp.exp2(scores(c) - m)
            l8 = l8 + jnp.sum(fold(p), axis=0)
            p_ref[c * KEY_CHUNK:(c + 1) * KEY_CHUNK, :] = p.astype(BF16)
        l_ref[...] = l8
        return jnp.where(l8 > MIN_COLUMN_SUM, 0.0, 1.0)

    def value_stage(i, p_ref, l_ref):
        start = i * Q_TILE if isinstance(i, int) else pl.multiple_of(i * Q_TILE, Q_TILE)
        l = jnp.sum(l_ref[...], axis=0, keepdims=True)
        l0, l1 = l[:, :Q_TILE], l[:, Q_TILE:]
        rho = (lam * l0 / l1).astype(BF16)
        for c in range(n_chunk):
            rows = slice(c * KEY_CHUNK, (c + 1) * KEY_CHUNK)
            a_ref[rows, :] = p_ref[rows, :Q_TILE] - p_ref[rows, Q_TILE:] * rho
        ot = jnp.dot(vt_ref[0], a_ref[...], preferred_element_type=F32) * (1.0 / l0)
        ot = ot * lax.rsqrt(jnp.mean(ot * ot, axis=0, keepdims=True) + EPS) * out_gain
        o_ref[0, pl.ds(start, Q_TILE), :] = ot.T.astype(o_ref.dtype)

    even = (pa_ref, la_ref)
    odd = (pb_ref, lb_ref)
    bad = exp_stage(0, *even, exact=False)

    def body(j, bad):
        i = 2 * j
        bad = jnp.maximum(bad, exp_stage(i + 1, *odd, exact=False))
        value_stage(i, *even)
        bad = jnp.maximum(bad, exp_stage(i + 2, *even, exact=False))
        value_stage(i + 1, *odd)
        return bad

    bad = lax.fori_loop(0, n_q // 2 - 1, body, bad)
    bad = jnp.maximum(bad, exp_stage(n_q - 1, *odd, exact=False))
    value_stage(n_q - 2, *even)
    value_stage(n_q - 1, *odd)

    @pl.when(jnp.max(bad) > 0.0)
    def _():
        def redo(i, carry):
            exp_stage(i, *even, exact=True)
            value_stage(i, *even)
            return carry

        lax.fori_loop(0, n_q, redo, 0)


def _da_attn(k, vt, qt, lam_vecs, subln_g, lambda_init):
    b, tot, di = k.shape
    l_lat = qt.shape[-1]
    n_head = di // HEAD_W
    return pl.pallas_call(
        functools.partial(_da_attn_kernel, lambda_init=lambda_init),
        grid=(b, n_head),
        in_specs=[pl.BlockSpec((1, tot, HEAD_W), lambda i, h: (i, 0, h)),
                  pl.BlockSpec((1, HEAD_W, tot), lambda i, h: (i, h, 0)),
                  pl.BlockSpec((1, HEAD_W, l_lat), lambda i, h: (i, h, 0)),
                  pl.BlockSpec((4, DA_DQK), lambda i, h: (0, 0)),
                  pl.BlockSpec((HEAD_W, 1), lambda i, h: (0, 0))],
        out_specs=pl.BlockSpec((1, l_lat, HEAD_W), lambda i, h: (i, 0, h)),
        out_shape=jax.ShapeDtypeStruct((b, l_lat, di), BF16),
        scratch_shapes=[pltpu.VMEM((tot, 2 * Q_TILE), BF16),
                        pltpu.VMEM((tot, 2 * Q_TILE), BF16),
                        pltpu.VMEM((tot, Q_TILE), BF16),
                        pltpu.VMEM((8, 2 * Q_TILE), F32),
                        pltpu.VMEM((8, 2 * Q_TILE), F32)],
        compiler_params=_params(("arbitrary", "arbitrary")),
        name="da_attn",
    )(k, vt, qt, lam_vecs, subln_g.reshape(HEAD_W, 1))


def _da_out_kernel(o_ref, gate_ref, x_ref, gm_ref, fg_ref, w_ref, out_ref):
    y = o_ref[0] * gate_ref[0]
    x = x_ref[0] + gm_ref[0] * jnp.dot(y, w_ref[...], preferred_element_type=F32)
    out_ref[0] = x * lax.rsqrt(jnp.mean(x * x, axis=-1, keepdims=True) + EPS) * fg_ref[...]


def _da_out(o, gate, x1, gate_mod, final_g, w_out):
    b, l_lat, di = o.shape
    d = x1.shape[-1]
    tokw = pl.BlockSpec((1, TOK_TILE, di), lambda i, t: (i, t, 0))
    return pl.pallas_call(
        _da_out_kernel,
        grid=(b, l_lat // TOK_TILE),
        in_specs=[tokw, tokw,
                  pl.BlockSpec((1, TOK_TILE, d), lambda i, t: (i, t + 1, 0)),
                  pl.BlockSpec((1, 1, d), lambda i, t: (i, 0, 0)),
                  pl.BlockSpec((1, d), lambda i, t: (0, 0)),
                  _whole_vmem()],
        out_specs=pl.BlockSpec((1, TOK_TILE, d), lambda i, t: (i, t, 0)),
        out_shape=jax.ShapeDtypeStruct((b, l_lat, d), F32),
        compiler_params=_params(("arbitrary", "arbitrary")),
        name="da_out",
    )(o, gate, x1, gate_mod, final_g, w_out)


def kernel(x, c, ctx, c_ctx, w_ada, b_ada, norm_g, hg_w_in, hg_lb_logits, hg_norm_g, hg_w_out,
           da_w_in, da_lam_q1, da_lam_k1, da_lam_q2, da_lam_k2, da_subln_g, da_w_out, final_g):
    b, l_lat, d = x.shape
    assert ctx.shape[1] == TOK_TILE and l_lat % TOK_TILE == 0 and l_lat % GRID_W == 0
    assert w_ada.shape[0] == 2 and hg_w_in.shape[0] == 1 and da_w_in.shape[0] == 1
    assert hg_lb_logits.shape[0] == 2

    rows = -(-(b + 1) // 8) * 8
    cc = jnp.concatenate([c, c_ctx[None], jnp.zeros((rows - b - 1, d), F32)], axis=0)
    ada = _adaln(cc, w_ada, b_ada)

    def mods(layer):
        m = ada[layer].reshape(rows, 3, d)
        pair = jnp.stack([jnp.broadcast_to(m[b], (b, 3, d)), m[:b]], axis=1)
        return pair[:, :, 0:2], pair[:, :, 2:3]

    mod0, gmod0 = mods(0)
    mod1, gmod1 = mods(1)

    q, gf, gb, v, gate0 = _hg_proj(ctx, x, mod0, norm_g[0:1], hg_w_in[0].astype(BF16), hg_lb_logits)
    of, ob = _hg_scan(q, gf, gb, v)
    x1 = _hg_out(of, ob, gate0, ctx, x, gmod0, hg_norm_g, hg_w_out[0].astype(BF16))

    lambda_init = 0.8 - 0.6 * math.exp(-0.3 * 1)
    k, vt, qt, gate1 = _da_proj(x1, mod1, norm_g[1:2], da_w_in[0], l_lat)
    lam_vecs = jnp.concatenate([da_lam_q1, da_lam_k1, da_lam_q2, da_lam_k2], axis=0)
    o = _da_attn(k, vt, qt, lam_vecs, da_subln_g[0], lambda_init)
    return _da_out(o, gate1, x1, gmod1[:, 1], final_g[None], da_w_out[0].astype(BF16))
```

```python
import functools
import math

import numpy as np
import jax
import jax.numpy as jnp
from jax import lax
from jax.experimental import pallas as pl
from jax.experimental.pallas import tpu as pltpu

F32 = jnp.float32
BF16 = jnp.bfloat16

EPS = 1e-6
GRID_W = 64
ROPE_THETA = 10000.0
HEAD_W = 128
DA_DQK = 64
TOK_TILE = 256
SCAN_CHUNK = 64
SCAN_HEADS = 8
Q_TILE = 256
KEY_CHUNK = 256
BOUND_SLACK = 1.01
MIN_COLUMN_SUM = 2.0 ** -64
EXP2_CLAMP = 115.0
LOG2E = 1.4426950408889634
VMEM_LIMIT = 56 * 1024 * 1024

NT = (((1,), (1,)), ((), ()))
TN = (((0,), (0,)), ((), ()))


def _sigmoid(x):
    return 1.0 / (1.0 + jnp.exp(-x))


def _silu(x):
    return x * _sigmoid(x)


def _params(sem):
    return pltpu.CompilerParams(dimension_semantics=sem, vmem_limit_bytes=VMEM_LIMIT)


def _whole_vmem():
    return pl.BlockSpec(memory_space=pltpu.VMEM)


def _adaln_kernel(c_ref, w_ref, b_ref, o_ref):
    s = _silu(c_ref[...]).astype(BF16)
    o_ref[0] = jnp.dot(s, w_ref[0], preferred_element_type=F32) + b_ref[0]


def _adaln(cc, w_ada, b_ada):
    depth, d, d3 = w_ada.shape
    rows = cc.shape[0]
    return pl.pallas_call(
        _adaln_kernel,
        grid=(depth,),
        in_specs=[pl.BlockSpec((rows, d), lambda i: (0, 0)),
                  pl.BlockSpec((1, d, d3), lambda i: (i, 0, 0)),
                  pl.BlockSpec((1, 1, d3), lambda i: (i, 0, 0))],
        out_specs=pl.BlockSpec((1, rows, d3), lambda i: (i, 0, 0)),
        out_shape=jax.ShapeDtypeStruct((depth, rows, d3), F32),
        compiler_params=_params(("arbitrary",)),
        name="adaln",
    )(cc, w_ada.astype(BF16), b_ada.reshape(depth, 1, d3))


def _modulated_norm(x, g, shift, scale):
    y = x * lax.rsqrt(jnp.mean(x * x, axis=-1, keepdims=True) + EPS)
    return (y * g) * (1.0 + scale) + shift


def _hg_proj_kernel(ctx_ref, x_ref, mod_ref, ng_ref, w_ref, lbl_ref,
                    q_ref, gf_ref, gb_ref, v_ref, gate_ref):
    t = pl.program_id(1)
    x = jnp.where(t == 0, ctx_ref[0], x_ref[0])
    h = _modulated_norm(x, ng_ref[...], mod_ref[0, 0, 0:1, :], mod_ref[0, 0, 1:2, :]).astype(BF16)
    di = q_ref.shape[-1]

    def seg(j):
        return jnp.dot(h, w_ref[:, j * di:(j + 1) * di], preferred_element_type=F32)

    def log_forget(z, d):
        l0 = lbl_ref[0, d:d + 1, :]
        l1 = lbl_ref[1, d:d + 1, :]
        m = jnp.maximum(l0, l1)
        e0 = jnp.exp(l0 - m)
        lb = e0 / (e0 + jnp.exp(l1 - m))
        return jnp.log2(lb + (1.0 - lb) * _sigmoid(z))

    q_ref[0] = _silu(seg(0)).astype(BF16)
    gf_ref[0] = log_forget(seg(1), 0)
    gb_ref[0] = log_forget(seg(2), 1)
    v_ref[0] = seg(3).astype(BF16)
    gate_ref[0] = _silu(seg(4)).astype(BF16)


def _hg_proj(ctx, x, mod, norm_g, w_in, lb_logits):
    b, l, d = x.shape
    di = w_in.shape[1] // 5
    nt = l // TOK_TILE + 1
    tot = l + TOK_TILE
    tok = lambda dt: jax.ShapeDtypeStruct((b, tot, di), dt)
    out_blk = pl.BlockSpec((1, TOK_TILE, di), lambda i, t: (i, t, 0))
    return pl.pallas_call(
        _hg_proj_kernel,
        grid=(b, nt),
        in_specs=[pl.BlockSpec((1, TOK_TILE, d), lambda i, t: (i, 0, 0)),
                  pl.BlockSpec((1, TOK_TILE, d), lambda i, t: (i, jnp.maximum(t - 1, 0), 0)),
                  pl.BlockSpec((1, 1, 2, d), lambda i, t: (i, jnp.minimum(t, 1), 0, 0)),
                  pl.BlockSpec((1, d), lambda i, t: (0, 0)),
                  _whole_vmem(),
                  pl.BlockSpec((2, 2, di), lambda i, t: (0, 0, 0))],
        out_specs=[out_blk] * 5,
        out_shape=[tok(BF16), tok(F32), tok(F32), tok(BF16), tok(BF16)],
        compiler_params=_params(("arbitrary", "arbitrary")),
        name="hg_proj",
    )(ctx, x, mod, norm_g, w_in, lb_logits)


def _hg_scan_kernel(qf_ref, gf_ref, vf_ref, qb_ref, gb_ref, vb_ref, of_ref, ob_ref,
                    s_ref, bf_ref, bb_ref, tri_ref):
    t = pl.program_id(2)
    c = SCAN_CHUNK
    n_chunk = TOK_TILE // c
    w = SCAN_HEADS * HEAD_W

    @pl.when((pl.program_id(0) == 0) & (pl.program_id(1) == 0) & (t == 0))
    def _():
        row = lax.broadcasted_iota(jnp.int32, (TOK_TILE, TOK_TILE), 0)
        col = lax.broadcasted_iota(jnp.int32, (TOK_TILE, TOK_TILE), 1)
        same = (row // c) == (col // c)
        tri_ref[0] = jnp.where(same & (col <= row), 1.0, 0.0).astype(BF16)
        tri_ref[1] = jnp.where(same & (col >= row), 1.0, 0.0).astype(BF16)

    @pl.when(t == 0)
    def _():
        s_ref[...] = jnp.zeros_like(s_ref)

    def cumsum(tri, g):
        hi = g.astype(BF16)
        lo = (g - hi.astype(F32)).astype(BF16)
        return jnp.dot(tri, hi, preferred_element_type=F32) + jnp.dot(tri, lo, preferred_element_type=F32)

    bf_ref[...] = cumsum(tri_ref[0], gf_ref[0])
    bb_ref[...] = cumsum(tri_ref[1], gb_ref[0])

    def chunk_decay_columns(b_ref, last_row):
        rows = [b_ref[j * c + last_row:j * c + last_row + 1, :] for j in range(n_chunk)]
        rows.append(jnp.zeros((8 - n_chunk, w), F32))
        return jnp.exp2(jnp.concatenate(rows, axis=0).T)

    dec_f = chunk_decay_columns(bf_ref, c - 1)
    dec_b = chunk_decay_columns(bb_ref, 0)

    ri = lax.broadcasted_iota(jnp.int32, (c, c), 0)
    ci = lax.broadcasted_iota(jnp.int32, (c, c), 1)
    keep_f = ci <= ri
    keep_b = ci >= ri

    def chunk_head(direction, j, hh, q_ref, g_ref, v_ref, b_ref, o_ref, dec, keep, mid_row, last_row):
        rows = slice(j * c, (j + 1) * c)
        lanes = slice(hh * HEAD_W, (hh + 1) * HEAD_W)
        q = q_ref[0, rows, lanes]
        g = g_ref[0, rows, lanes]
        v = v_ref[0, rows, lanes]
        bcum = b_ref[rows, lanes]
        r = bcum[mid_row:mid_row + 1, :]
        b_last = bcum[last_row:last_row + 1, :]
        d = jnp.clip(bcum - r, -EXP2_CLAMP, EXP2_CLAMP)
        q_mid = q * jnp.exp2(d).astype(BF16)
        k_mid = ((1.0 - jnp.exp2(g)) * jnp.exp2(-d)).astype(BF16)
        q_dec = q_mid * jnp.exp2(r).astype(BF16)
        k_dec = k_mid * jnp.exp2(b_last - r).astype(BF16)
        a = lax.dot_general(q_mid, k_mid, NT, preferred_element_type=F32)
        a = jnp.where(keep, a, 0.0).astype(BF16)
        s = s_ref[direction, hh]
        o_ref[0, rows, lanes] = jnp.dot(jnp.concatenate([q_dec, a], axis=1),
                                        jnp.concatenate([s.astype(BF16), v], axis=0),
                                        preferred_element_type=F32).astype(o_ref.dtype)
        s_ref[direction, hh] = s * dec[lanes, j:j + 1] + lax.dot_general(
            k_dec, v, TN, preferred_element_type=F32)

    for i in range(n_chunk):
        for hh in range(SCAN_HEADS):
            chunk_head(0, i, hh, qf_ref, gf_ref, vf_ref, bf_ref, of_ref, dec_f, keep_f, c // 2 - 1, c - 1)
            chunk_head(1, n_chunk - 1 - i, hh, qb_ref, gb_ref, vb_ref, bb_ref, ob_ref, dec_b, keep_b, c // 2, 0)


def _hg_scan(q, gf, gb, v):
    b, tot, di = q.shape
    nt = tot // TOK_TILE
    w = SCAN_HEADS * HEAD_W
    fwd = pl.BlockSpec((1, TOK_TILE, w), lambda i, h, t: (i, t, h))
    bwd = pl.BlockSpec((1, TOK_TILE, w), lambda i, h, t: (i, jnp.where(t == 0, 0, nt - t), h))
    out = jax.ShapeDtypeStruct((b, tot, di), BF16)
    return pl.pallas_call(
        _hg_scan_kernel,
        grid=(b, di // w, nt),
        in_specs=[fwd, fwd, fwd, bwd, bwd, bwd],
        out_specs=[fwd, bwd],
        out_shape=[out, out],
        scratch_shapes=[pltpu.VMEM((2, SCAN_HEADS, HEAD_W, HEAD_W), F32),
                        pltpu.VMEM((TOK_TILE, w), F32),
                        pltpu.VMEM((TOK_TILE, w), F32),
                        pltpu.VMEM((2, TOK_TILE, TOK_TILE), BF16)],
        compiler_params=_params(("arbitrary", "arbitrary", "arbitrary")),
        name="hg_scan",
    )(q, gf, v, q, gb, v)


def _rope_lane_tables(l_lat):
    ax = DA_DQK // 2
    inv = 1.0 / (ROPE_THETA ** (np.arange(0, ax, 2, dtype=np.float64) / ax))
    pos = np.arange(l_lat)
    ang_r = (pos // GRID_W)[:, None] * inv
    ang_c = (pos % GRID_W)[:, None] * inv
    zero = np.zeros_like(ang_r)
    cos64 = np.concatenate([np.cos(ang_r), np.cos(ang_r), np.cos(ang_c), np.cos(ang_c)], axis=1)
    up64 = np.concatenate([-np.sin(ang_r), zero, -np.sin(ang_c), zero], axis=1)
    dn64 = np.concatenate([zero, np.sin(ang_r), zero, np.sin(ang_c)], axis=1)
    lat = [np.concatenate([t, t], axis=1) for t in (cos64, up64, dn64)]
    ident = [np.ones((TOK_TILE, HEAD_W)), np.zeros((TOK_TILE, HEAD_W)), np.zeros((TOK_TILE, HEAD_W))]
    return [np.concatenate([i, t], axis=0).astype(np.float32) for i, t in zip(ident, lat)], (cos64, up64, dn64)


def _mid_kernel(of_ref, ob_ref, gate0_ref, ctx_ref, x_ref, gm_ref, hng_ref, wo_ref,
                mod_ref, ng_ref, wq_ref, wk_ref, wv_ref, wg_ref,
                kc_ref, ku_ref, kd_ref, qc_ref, qs_ref,
                x1_ref, k_ref, vt_ref, qt_ref, gate_ref, y_ref):
    t = pl.program_id(1)
    n_head = k_ref.shape[-1] // HEAD_W

    for hh in range(n_head):
        lanes = slice(hh * HEAD_W, (hh + 1) * HEAD_W)
        o = of_ref[0, :, lanes].astype(F32) + ob_ref[0, :, lanes].astype(F32)
        o = o * lax.rsqrt(jnp.mean(o * o, axis=-1, keepdims=True) + EPS) * hng_ref[:, lanes]
        y_ref[:, lanes] = (o * gate0_ref[0, :, lanes].astype(F32)).astype(BF16)
    y = jnp.dot(y_ref[...], wo_ref[...], preferred_element_type=F32)
    x1 = jnp.where(t == 0, ctx_ref[0], x_ref[0]) + gm_ref[0, 0] * y

    h = _modulated_norm(x1, ng_ref[...], mod_ref[0, 0, 0:1, :], mod_ref[0, 0, 1:2, :]).astype(BF16)
    k = jnp.dot(h, wk_ref[...], preferred_element_type=F32)
    cos, s_up, s_dn = kc_ref[...], ku_ref[...], kd_ref[...]
    for hh in range(n_head):
        lanes = slice(hh * HEAD_W, (hh + 1) * HEAD_W)
        kh = k[:, lanes]
        rot = (kh * cos + pltpu.roll(kh, HEAD_W - 16, axis=1) * s_up
               + pltpu.roll(kh, 16, axis=1) * s_dn)
        k_ref[0, :, lanes] = rot.astype(BF16)

    vt_ref[0] = lax.dot_general(wv_ref[...], h, NT, preferred_element_type=F32).astype(BF16)

    @pl.when(t > 0)
    def _():
        x1_ref[0] = x1
        qt = lax.dot_general(wq_ref[...], h, NT, preferred_element_type=F32)
        qc, qs = qc_ref[...], qs_ref[...]
        for grp in range(qt.shape[0] // DA_DQK):
            x = qt[grp * DA_DQK:(grp + 1) * DA_DQK]
            partner = jnp.concatenate([x[16:32], x[0:16], x[48:64], x[32:48]], axis=0)
            qt_ref[0, grp * DA_DQK:(grp + 1) * DA_DQK, :] = (x * qc + partner * qs).astype(BF16)
        gate_ref[0] = _silu(jnp.dot(h, wg_ref[...], preferred_element_type=F32)).astype(BF16)


def _mid(of, ob, gate0, ctx, x, gate_mod, hg_norm_g, w_out, mod, norm_g, w_in):
    b, tot, di = of.shape
    l_lat, d = x.shape[1], x.shape[2]
    nt = tot // TOK_TILE
    wq, wk, wv, wg = (w_in[:, j * di:(j + 1) * di] for j in range(4))
    ktabs, (cos64, up64, dn64) = _rope_lane_tables(l_lat)
    q_scale = DA_DQK ** -0.5 * LOG2E
    qc = (cos64.T * q_scale).astype(np.float32)
    qs = ((up64 + dn64).T * q_scale).astype(np.float32)
    lat_t = lambda t: jnp.maximum(t - 1, 0)
    tokw = pl.BlockSpec((1, TOK_TILE, di), lambda i, t: (i, t, 0))
    ktab = pl.BlockSpec((TOK_TILE, HEAD_W), lambda i, t: (t, 0))
    qtab = pl.BlockSpec((DA_DQK, TOK_TILE), lambda i, t: (0, lat_t(t)))
    return pl.pallas_call(
        _mid_kernel,
        grid=(b, nt),
        in_specs=[tokw, tokw, tokw,
                  pl.BlockSpec((1, TOK_TILE, d), lambda i, t: (i, 0, 0)),
                  pl.BlockSpec((1, TOK_TILE, d), lambda i, t: (i, lat_t(t), 0)),
                  pl.BlockSpec((1, 1, 1, d), lambda i, t: (i, jnp.minimum(t, 1), 0, 0)),
                  pl.BlockSpec((1, di), lambda i, t: (0, 0)),
                  _whole_vmem(),
                  pl.BlockSpec((1, 1, 2, d), lambda i, t: (i, jnp.minimum(t, 1), 0, 0)),
                  pl.BlockSpec((1, d), lambda i, t: (0, 0)),
                  _whole_vmem(), _whole_vmem(), _whole_vmem(), _whole_vmem(),
                  ktab, ktab, ktab, qtab, qtab],
        out_specs=[pl.BlockSpec((1, TOK_TILE, d), lambda i, t: (i, lat_t(t), 0)),
                   pl.BlockSpec((1, TOK_TILE, di), lambda i, t: (i, t, 0)),
                   pl.BlockSpec((1, di, TOK_TILE), lambda i, t: (i, 0, t)),
                   pl.BlockSpec((1, di, TOK_TILE), lambda i, t: (i, 0, lat_t(t))),
                   pl.BlockSpec((1, TOK_TILE, di), lambda i, t: (i, lat_t(t), 0))],
        out_shape=[jax.ShapeDtypeStruct((b, l_lat, d), F32),
                   jax.ShapeDtypeStruct((b, tot, di), BF16),
                   jax.ShapeDtypeStruct((b, di, tot), BF16),
                   jax.ShapeDtypeStruct((b, di, l_lat), BF16),
                   jax.ShapeDtypeStruct((b, l_lat, di), BF16)],
        scratch_shapes=[pltpu.VMEM((TOK_TILE, di), BF16)],
        compiler_params=_params(("arbitrary", "arbitrary")),
        name="hg_out_da_proj",
    )(of, ob, gate0, ctx, x, gate_mod, hg_norm_g, w_out, mod, norm_g,
      wq.T.astype(BF16), wk.astype(BF16), wv.T.astype(BF16), wg.astype(BF16),
      *[jnp.asarray(a) for a in ktabs], jnp.asarray(qc), jnp.asarray(qs))


def _da_attn_kernel(k_ref, vt_ref, qt_ref, lam_ref, sg_ref, o_ref, pa_ref, pb_ref, a_ref, la_ref, lb_ref,
                    *, lambda_init):
    lq1, lk1, lq2, lk2 = (lam_ref[i:i + 1, :] for i in range(4))
    lam = (jnp.exp(jnp.sum(lq1 * lk1, axis=-1, keepdims=True))
           - jnp.exp(jnp.sum(lq2 * lk2, axis=-1, keepdims=True)) + lambda_init)
    out_gain = sg_ref[...] * (1.0 - lambda_init)
    n_chunk = k_ref.shape[1] // KEY_CHUNK
    n_q = qt_ref.shape[-1] // Q_TILE
    w2 = 2 * Q_TILE

    kf = k_ref[0].astype(F32)
    sel_r = lax.broadcasted_iota(jnp.int32, (HEAD_W, HEAD_W), 0) // DA_DQK
    sel_c = lax.broadcasted_iota(jnp.int32, (HEAD_W, HEAD_W), 1)
    sel = jnp.where(sel_r == sel_c, 1.0, 0.0).astype(BF16)
    norm2 = jnp.dot((kf * kf).astype(BF16), sel, preferred_element_type=F32)
    kmax = jnp.sqrt(jnp.max(norm2, axis=0, keepdims=True)) * BOUND_SLACK
    kmax = jnp.concatenate([jnp.broadcast_to(kmax[:, 0:1], (1, Q_TILE)),
                            jnp.broadcast_to(kmax[:, 1:2], (1, Q_TILE))], axis=1)
    half0 = lax.broadcasted_iota(jnp.int32, (HEAD_W, Q_TILE), 0) < DA_DQK

    def fold(x):
        return x.reshape(KEY_CHUNK // 8, 8, w2)

    def exp_stage(i, p_ref, l_ref, exact):
        start = i * Q_TILE if isinstance(i, int) else pl.multiple_of(i * Q_TILE, Q_TILE)
        qt = qt_ref[0, :, pl.ds(start, Q_TILE)]
        zero = jnp.zeros_like(qt)
        rhs = jnp.concatenate([jnp.where(half0, qt, zero), jnp.where(half0, zero, qt)], axis=1)

        def scores(c):
            return jnp.dot(k_ref[0, c * KEY_CHUNK:(c + 1) * KEY_CHUNK, :], rhs, preferred_element_type=F32)

        if exact:
            mx = jnp.full((8, w2), -jnp.inf, F32)
            for c in range(n_chunk):
                mx = jnp.maximum(mx, jnp.max(fold(scores(c)), axis=0))
            m = jnp.max(mx, axis=0, keepdims=True)
        else:
            rf = rhs.astype(F32)
            m = jnp.sqrt(jnp.sum(rf * rf, axis=0, keepdims=True)) * kmax
        l8 = jnp.zeros((8, w2), F32)
        for c in range(n_chunk):
            p = jnp.exp2(scores(c) - m)
            l8 = l8 + jnp.sum(fold(p), axis=0)
            p_ref[c * KEY_CHUNK:(c + 1) * KEY_CHUNK, :] = p.astype(BF16)
        l_ref[...] = l8
        return jnp.where(l8 > MIN_COLUMN_SUM, 0.0, 1.0)

    def value_stage(i, p_ref, l_ref):
        start = i * Q_TILE if isinstance(i, int) else pl.multiple_of(i * Q_TILE, Q_TILE)
        l = jnp.sum(l_ref[...], axis=0, keepdims=True)
        l0, l1 = l[:, :Q_TILE], l[:, Q_TILE:]
        rho = (lam * l0 / l1).astype(BF16)
        for c in range(n_chunk):
            rows = slice(c * KEY_CHUNK, (c + 1) * KEY_CHUNK)
            a_ref[rows, :] = p_ref[rows, :Q_TILE] - p_ref[rows, Q_TILE:] * rho
        ot = jnp.dot(vt_ref[0], a_ref[...], preferred_element_type=F32) * (1.0 / l0)
        ot = ot * lax.rsqrt(jnp.mean(ot * ot, axis=0, keepdims=True) + EPS) * out_gain
        o_ref[0, pl.ds(start, Q_TILE), :] = ot.T.astype(o_ref.dtype)

    even = (pa_ref, la_ref)
    odd = (pb_ref, lb_ref)
    bad = exp_stage(0, *even, exact=False)

    def body(j, bad):
        i = 2 * j
        bad = jnp.maximum(bad, exp_stage(i + 1, *odd, exact=False))
        value_stage(i, *even)
        bad = jnp.maximum(bad, exp_stage(i + 2, *even, exact=False))
        value_stage(i + 1, *odd)
        return bad

    bad = lax.fori_loop(0, n_q // 2 - 1, body, bad)
    bad = jnp.maximum(bad, exp_stage(n_q - 1, *odd, exact=False))
    value_stage(n_q - 2, *even)
    value_stage(n_q - 1, *odd)

    @pl.when(jnp.max(bad) > 0.0)
    def _():
        def redo(i, carry):
            exp_stage(i, *even, exact=True)
            value_stage(i, *even)
            return carry

        lax.fori_loop(0, n_q, redo, 0)


def _da_attn(k, vt, qt, lam_vecs, subln_g, lambda_init):
    b, tot, di = k.shape
    l_lat = qt.shape[-1]
    n_head = di // HEAD_W
    return pl.pallas_call(
        functools.partial(_da_attn_kernel, lambda_init=lambda_init),
        grid=(b, n_head),
        in_specs=[pl.BlockSpec((1, tot, HEAD_W), lambda i, h: (i, 0, h)),
                  pl.BlockSpec((1, HEAD_W, tot), lambda i, h: (i, h, 0)),
                  pl.BlockSpec((1, HEAD_W, l_lat), lambda i, h: (i, h, 0)),
                  pl.BlockSpec((4, DA_DQK), lambda i, h: (0, 0)),
                  pl.BlockSpec((HEAD_W, 1), lambda i, h: (0, 0))],
        out_specs=pl.BlockSpec((1, l_lat, HEAD_W), lambda i, h: (i, 0, h)),
        out_shape=jax.ShapeDtypeStruct((b, l_lat, di), BF16),
        scratch_shapes=[pltpu.VMEM((tot, 2 * Q_TILE), BF16),
                        pltpu.VMEM((tot, 2 * Q_TILE), BF16),
                        pltpu.VMEM((tot, Q_TILE), BF16),
                        pltpu.VMEM((8, 2 * Q_TILE), F32),
                        pltpu.VMEM((8, 2 * Q_TILE), F32)],
        compiler_params=_params(("arbitrary", "arbitrary")),
        name="da_attn",
    )(k, vt, qt, lam_vecs, subln_g.reshape(HEAD_W, 1))


def _da_out_kernel(o_ref, gate_ref, x_ref, gm_ref, fg_ref, w_ref, out_ref):
    y = o_ref[0] * gate_ref[0]
    x = x_ref[0] + gm_ref[0] * jnp.dot(y, w_ref[...], preferred_element_type=F32)
    out_ref[0] = x * lax.rsqrt(jnp.mean(x * x, axis=-1, keepdims=True) + EPS) * fg_ref[...]


def _da_out(o, gate, x1, gate_mod, final_g, w_out):
    b, l_lat, di = o.shape
    d = x1.shape[-1]
    tokw = pl.BlockSpec((1, TOK_TILE, di), lambda i, t: (i, t, 0))
    return pl.pallas_call(
        _da_out_kernel,
        grid=(b, l_lat // TOK_TILE),
        in_specs=[tokw, tokw,
                  pl.BlockSpec((1, TOK_TILE, d), lambda i, t: (i, t, 0)),
                  pl.BlockSpec((1, 1, d), lambda i, t: (i, 0, 0)),
                  pl.BlockSpec((1, d), lambda i, t: (0, 0)),
                  _whole_vmem()],
        out_specs=pl.BlockSpec((1, TOK_TILE, d), lambda i, t: (i, t, 0)),
        out_shape=jax.ShapeDtypeStruct((b, l_lat, d), F32),
        compiler_params=_params(("arbitrary", "arbitrary")),
        name="da_out",
    )(o, gate, x1, gate_mod, final_g, w_out)


def kernel(x, c, ctx, c_ctx, w_ada, b_ada, norm_g, hg_w_in, hg_lb_logits, hg_norm_g, hg_w_out,
           da_w_in, da_lam_q1, da_lam_k1, da_lam_q2, da_lam_k2, da_subln_g, da_w_out, final_g):
    b, l_lat, d = x.shape
    assert ctx.shape[1] == TOK_TILE and l_lat % TOK_TILE == 0 and l_lat % GRID_W == 0
    assert w_ada.shape[0] == 2 and hg_w_in.shape[0] == 1 and da_w_in.shape[0] == 1
    assert hg_lb_logits.shape[0] == 2

    rows = -(-(b + 1) // 8) * 8
    cc = jnp.concatenate([c, c_ctx[None], jnp.zeros((rows - b - 1, d), F32)], axis=0)
    ada = _adaln(cc, w_ada, b_ada)

    def mods(layer):
        m = ada[layer].reshape(rows, 3, d)
        pair = jnp.stack([jnp.broadcast_to(m[b], (b, 3, d)), m[:b]], axis=1)
        return pair[:, :, 0:2], pair[:, :, 2:3]

    mod0, gmod0 = mods(0)
    mod1, gmod1 = mods(1)

    q, gf, gb, v, gate0 = _hg_proj(ctx, x, mod0, norm_g[0:1], hg_w_in[0].astype(BF16), hg_lb_logits)
    of, ob = _hg_scan(q, gf, gb, v)
    x1, k, vt, qt, gate1 = _mid(of, ob, gate0, ctx, x, gmod0, hg_norm_g, hg_w_out[0].astype(BF16),
                                mod1, norm_g[1:2], da_w_in[0])

    lambda_init = 0.8 - 0.6 * math.exp(-0.3 * 1)
    lam_vecs = jnp.concatenate([da_lam_q1, da_lam_k1, da_lam_q2, da_lam_k2], axis=0)
    o = _da_attn(k, vt, qt, lam_vecs, da_subln_g[0], lambda_init)
    return _da_out(o, gate1, x1, gmod1[:, 1], final_g[None], da_w_out[0].astype(BF16))
```

```python
import functools
import math

import numpy as np
import jax
import jax.numpy as jnp
from jax import lax
from jax.experimental import pallas as pl
from jax.experimental.pallas import tpu as pltpu

F32 = jnp.float32
BF16 = jnp.bfloat16

EPS = 1e-6
GRID_W = 64
ROPE_THETA = 10000.0
HEAD_W = 128
DA_DQK = 64
TOK_TILE = 256
OUT_TILE = 1024
SCAN_CHUNK = 64
SCAN_HEADS = 8
Q_TILE = 256
KEY_CHUNK = 256
BOUND_SLACK = 1.01
MIN_COLUMN_SUM = 2.0 ** -64
EXP2_CLAMP = 115.0
LOG2E = 1.4426950408889634
VMEM_LIMIT = 56 * 1024 * 1024

NT = (((1,), (1,)), ((), ()))
TN = (((0,), (0,)), ((), ()))


def _sigmoid(x):
    return 1.0 / (1.0 + jnp.exp(-x))


def _silu(x):
    return x * _sigmoid(x)


def _params(sem):
    return pltpu.CompilerParams(dimension_semantics=sem, vmem_limit_bytes=VMEM_LIMIT)


def _whole_vmem():
    return pl.BlockSpec(memory_space=pltpu.VMEM)


def _adaln_kernel(c_ref, w_ref, b_ref, o_ref):
    s = _silu(c_ref[...]).astype(BF16)
    o_ref[0] = jnp.dot(s, w_ref[0], preferred_element_type=F32) + b_ref[0]


def _adaln(cc, w_ada, b_ada):
    depth, d, d3 = w_ada.shape
    rows = cc.shape[0]
    return pl.pallas_call(
        _adaln_kernel,
        grid=(depth,),
        in_specs=[pl.BlockSpec((rows, d), lambda i: (0, 0)),
                  pl.BlockSpec((1, d, d3), lambda i: (i, 0, 0)),
                  pl.BlockSpec((1, 1, d3), lambda i: (i, 0, 0))],
        out_specs=pl.BlockSpec((1, rows, d3), lambda i: (i, 0, 0)),
        out_shape=jax.ShapeDtypeStruct((depth, rows, d3), F32),
        compiler_params=_params(("arbitrary",)),
        name="adaln",
    )(cc, w_ada.astype(BF16), b_ada.reshape(depth, 1, d3))


def _modulated_norm(x, g, shift, scale):
    y = x * lax.rsqrt(jnp.mean(x * x, axis=-1, keepdims=True) + EPS)
    return (y * g) * (1.0 + scale) + shift


def _hg_proj_kernel(ctx_ref, x_ref, mod_ref, ng_ref, w_ref, lbl_ref,
                    q_ref, gf_ref, gb_ref, v_ref, gate_ref):
    t = pl.program_id(1)
    x = jnp.where(t == 0, ctx_ref[0], x_ref[0])
    h = _modulated_norm(x, ng_ref[...], mod_ref[0, 0, 0:1, :], mod_ref[0, 0, 1:2, :]).astype(BF16)
    di = q_ref.shape[-1]

    def seg(j):
        return jnp.dot(h, w_ref[:, j * di:(j + 1) * di], preferred_element_type=F32)

    def log_forget(z, d):
        l0 = lbl_ref[0, d:d + 1, :]
        l1 = lbl_ref[1, d:d + 1, :]
        m = jnp.maximum(l0, l1)
        e0 = jnp.exp(l0 - m)
        lb = e0 / (e0 + jnp.exp(l1 - m))
        return jnp.log2(lb + (1.0 - lb) * _sigmoid(z))

    q_ref[0] = _silu(seg(0)).astype(BF16)
    gf_ref[0] = log_forget(seg(1), 0).astype(BF16)
    gb_ref[0] = log_forget(seg(2), 1).astype(BF16)
    v_ref[0] = seg(3).astype(BF16)
    gate_ref[0] = _silu(seg(4)).astype(BF16)


def _hg_proj(ctx, x, mod, norm_g, w_in, lb_logits):
    b, l, d = x.shape
    di = w_in.shape[1] // 5
    nt = l // TOK_TILE + 1
    tot = l + TOK_TILE
    tok = lambda dt: jax.ShapeDtypeStruct((b, tot, di), dt)
    out_blk = pl.BlockSpec((1, TOK_TILE, di), lambda i, t: (i, t, 0))
    return pl.pallas_call(
        _hg_proj_kernel,
        grid=(b, nt),
        in_specs=[pl.BlockSpec((1, TOK_TILE, d), lambda i, t: (i, 0, 0)),
                  pl.BlockSpec((1, TOK_TILE, d), lambda i, t: (i, jnp.maximum(t - 1, 0), 0)),
                  pl.BlockSpec((1, 1, 2, d), lambda i, t: (i, jnp.minimum(t, 1), 0, 0)),
                  pl.BlockSpec((1, d), lambda i, t: (0, 0)),
                  _whole_vmem(),
                  pl.BlockSpec((2, 2, di), lambda i, t: (0, 0, 0))],
        out_specs=[out_blk] * 5,
        out_shape=[tok(BF16)] * 5,
        compiler_params=_params(("arbitrary", "arbitrary")),
        name="hg_proj",
    )(ctx, x, mod, norm_g, w_in, lb_logits)


def _hg_scan_kernel(qf_ref, gf_ref, vf_ref, qb_ref, gb_ref, vb_ref, of_ref, ob_ref,
                    s_ref, bf_ref, bb_ref, tri_ref):
    t = pl.program_id(2)
    c = SCAN_CHUNK
    n_chunk = TOK_TILE // c
    w = SCAN_HEADS * HEAD_W

    @pl.when((pl.program_id(0) == 0) & (pl.program_id(1) == 0) & (t == 0))
    def _():
        row = lax.broadcasted_iota(jnp.int32, (TOK_TILE, TOK_TILE), 0)
        col = lax.broadcasted_iota(jnp.int32, (TOK_TILE, TOK_TILE), 1)
        same = (row // c) == (col // c)
        tri_ref[0] = jnp.where(same & (col <= row), 1.0, 0.0).astype(BF16)
        tri_ref[1] = jnp.where(same & (col >= row), 1.0, 0.0).astype(BF16)

    @pl.when(t == 0)
    def _():
        s_ref[...] = jnp.zeros_like(s_ref)

    def cumsum(tri, g):
        return jnp.dot(tri, g, preferred_element_type=F32)

    bf_ref[...] = cumsum(tri_ref[0], gf_ref[0])
    bb_ref[...] = cumsum(tri_ref[1], gb_ref[0])

    def chunk_decay_columns(b_ref, last_row):
        rows = [b_ref[j * c + last_row:j * c + last_row + 1, :] for j in range(n_chunk)]
        rows.append(jnp.zeros((8 - n_chunk, w), F32))
        return jnp.exp2(jnp.concatenate(rows, axis=0).T)

    dec_f = chunk_decay_columns(bf_ref, c - 1)
    dec_b = chunk_decay_columns(bb_ref, 0)

    ri = lax.broadcasted_iota(jnp.int32, (c, c), 0)
    ci = lax.broadcasted_iota(jnp.int32, (c, c), 1)
    keep_f = ci <= ri
    keep_b = ci >= ri

    def chunk_head(direction, j, hh, q_ref, g_ref, v_ref, b_ref, o_ref, dec, keep, mid_row, last_row):
        rows = slice(j * c, (j + 1) * c)
        lanes = slice(hh * HEAD_W, (hh + 1) * HEAD_W)
        q = q_ref[0, rows, lanes]
        g = g_ref[0, rows, lanes].astype(F32)
        v = v_ref[0, rows, lanes]
        bcum = b_ref[rows, lanes]
        r = bcum[mid_row:mid_row + 1, :]
        b_last = bcum[last_row:last_row + 1, :]
        d = jnp.clip(bcum - r, -EXP2_CLAMP, EXP2_CLAMP)
        q_mid = q * jnp.exp2(d).astype(BF16)
        k_mid = ((1.0 - jnp.exp2(g)) * jnp.exp2(-d)).astype(BF16)
        q_dec = q_mid * jnp.exp2(r).astype(BF16)
        k_dec = k_mid * jnp.exp2(b_last - r).astype(BF16)
        a = lax.dot_general(q_mid, k_mid, NT, preferred_element_type=F32)
        a = jnp.where(keep, a, 0.0).astype(BF16)
        s = s_ref[direction, hh]
        o_ref[0, rows, lanes] = jnp.dot(jnp.concatenate([q_dec, a], axis=1),
                                        jnp.concatenate([s.astype(BF16), v], axis=0),
                                        preferred_element_type=F32).astype(o_ref.dtype)
        s_ref[direction, hh] = s * dec[lanes, j:j + 1] + lax.dot_general(
            k_dec, v, TN, preferred_element_type=F32)

    for i in range(n_chunk):
        for hh in range(SCAN_HEADS):
            chunk_head(0, i, hh, qf_ref, gf_ref, vf_ref, bf_ref, of_ref, dec_f, keep_f, c // 2 - 1, c - 1)
            chunk_head(1, n_chunk - 1 - i, hh, qb_ref, gb_ref, vb_ref, bb_ref, ob_ref, dec_b, keep_b, c // 2, 0)


def _hg_scan(q, gf, gb, v):
    b, tot, di = q.shape
    nt = tot // TOK_TILE
    w = SCAN_HEADS * HEAD_W
    fwd = pl.BlockSpec((1, TOK_TILE, w), lambda i, h, t: (i, t, h))
    bwd = pl.BlockSpec((1, TOK_TILE, w), lambda i, h, t: (i, jnp.where(t == 0, 0, nt - t), h))
    out = jax.ShapeDtypeStruct((b, tot, di), BF16)
    return pl.pallas_call(
        _hg_scan_kernel,
        grid=(b, di // w, nt),
        in_specs=[fwd, fwd, fwd, bwd, bwd, bwd],
        out_specs=[fwd, bwd],
        out_shape=[out, out],
        scratch_shapes=[pltpu.VMEM((2, SCAN_HEADS, HEAD_W, HEAD_W), F32),
                        pltpu.VMEM((TOK_TILE, w), F32),
                        pltpu.VMEM((TOK_TILE, w), F32),
                        pltpu.VMEM((2, TOK_TILE, TOK_TILE), BF16)],
        compiler_params=_params(("arbitrary", "arbitrary", "arbitrary")),
        name="hg_scan",
    )(q, gf, v, q, gb, v)


def _rope_lane_tables(l_lat):
    ax = DA_DQK // 2
    inv = 1.0 / (ROPE_THETA ** (np.arange(0, ax, 2, dtype=np.float64) / ax))
    pos = np.arange(l_lat)
    ang_r = (pos // GRID_W)[:, None] * inv
    ang_c = (pos % GRID_W)[:, None] * inv
    zero = np.zeros_like(ang_r)
    cos64 = np.concatenate([np.cos(ang_r), np.cos(ang_r), np.cos(ang_c), np.cos(ang_c)], axis=1)
    up64 = np.concatenate([-np.sin(ang_r), zero, -np.sin(ang_c), zero], axis=1)
    dn64 = np.concatenate([zero, np.sin(ang_r), zero, np.sin(ang_c)], axis=1)
    lat = [np.concatenate([t, t], axis=1) for t in (cos64, up64, dn64)]
    ident = [np.ones((TOK_TILE, HEAD_W)), np.zeros((TOK_TILE, HEAD_W)), np.zeros((TOK_TILE, HEAD_W))]
    return [np.concatenate([i, t], axis=0).astype(np.float32) for i, t in zip(ident, lat)], (cos64, up64, dn64)


def _mid_kernel(of_ref, ob_ref, gate0_ref, ctx_ref, x_ref, gm_ref, hng_ref, wo_ref,
                mod_ref, ng_ref, wq_ref, wk_ref, wv_ref, wg_ref,
                kc_ref, ku_ref, kd_ref, qc_ref, qs_ref,
                x1_ref, k_ref, vt_ref, qt_ref, gate_ref, y_ref):
    t = pl.program_id(1)
    n_head = k_ref.shape[-1] // HEAD_W

    for hh in range(n_head):
        lanes = slice(hh * HEAD_W, (hh + 1) * HEAD_W)
        o = of_ref[0, :, lanes].astype(F32) + ob_ref[0, :, lanes].astype(F32)
        o = o * lax.rsqrt(jnp.mean(o * o, axis=-1, keepdims=True) + EPS) * hng_ref[:, lanes]
        y_ref[:, lanes] = (o * gate0_ref[0, :, lanes].astype(F32)).astype(BF16)
    y = jnp.dot(y_ref[...], wo_ref[...], preferred_element_type=F32)
    x1 = jnp.where(t == 0, ctx_ref[0], x_ref[0]) + gm_ref[0, 0] * y

    h = _modulated_norm(x1, ng_ref[...], mod_ref[0, 0, 0:1, :], mod_ref[0, 0, 1:2, :]).astype(BF16)
    k = jnp.dot(h, wk_ref[...], preferred_element_type=F32)
    cos, s_up, s_dn = kc_ref[...], ku_ref[...], kd_ref[...]
    for hh in range(n_head):
        lanes = slice(hh * HEAD_W, (hh + 1) * HEAD_W)
        kh = k[:, lanes]
        rot = (kh * cos + pltpu.roll(kh, HEAD_W - 16, axis=1) * s_up
               + pltpu.roll(kh, 16, axis=1) * s_dn)
        k_ref[0, :, lanes] = rot.astype(BF16)

    vt_ref[0] = lax.dot_general(wv_ref[...], h, NT, preferred_element_type=F32).astype(BF16)

    @pl.when(t > 0)
    def _():
        x1_ref[0] = x1
        qt = lax.dot_general(wq_ref[...], h, NT, preferred_element_type=F32)
        qc, qs = qc_ref[...], qs_ref[...]
        for grp in range(qt.shape[0] // DA_DQK):
            x = qt[grp * DA_DQK:(grp + 1) * DA_DQK]
            partner = jnp.concatenate([x[16:32], x[0:16], x[48:64], x[32:48]], axis=0)
            qt_ref[0, grp * DA_DQK:(grp + 1) * DA_DQK, :] = (x * qc + partner * qs).astype(BF16)
        gate_ref[0] = _silu(jnp.dot(h, wg_ref[...], preferred_element_type=F32)).astype(BF16)


def _mid(of, ob, gate0, ctx, x, gate_mod, hg_norm_g, w_out, mod, norm_g, w_in):
    b, tot, di = of.shape
    l_lat, d = x.shape[1], x.shape[2]
    nt = tot // TOK_TILE
    wq, wk, wv, wg = (w_in[:, j * di:(j + 1) * di] for j in range(4))
    ktabs, (cos64, up64, dn64) = _rope_lane_tables(l_lat)
    q_scale = DA_DQK ** -0.5 * LOG2E
    qc = (cos64.T * q_scale).astype(np.float32)
    qs = ((up64 + dn64).T * q_scale).astype(np.float32)
    lat_t = lambda t: jnp.maximum(t - 1, 0)
    tokw = pl.BlockSpec((1, TOK_TILE, di), lambda i, t: (i, t, 0))
    ktab = pl.BlockSpec((TOK_TILE, HEAD_W), lambda i, t: (t, 0))
    qtab = pl.BlockSpec((DA_DQK, TOK_TILE), lambda i, t: (0, lat_t(t)))
    return pl.pallas_call(
        _mid_kernel,
        grid=(b, nt),
        in_specs=[tokw, tokw, tokw,
                  pl.BlockSpec((1, TOK_TILE, d), lambda i, t: (i, 0, 0)),
                  pl.BlockSpec((1, TOK_TILE, d), lambda i, t: (i, lat_t(t), 0)),
                  pl.BlockSpec((1, 1, 1, d), lambda i, t: (i, jnp.minimum(t, 1), 0, 0)),
                  pl.BlockSpec((1, di), lambda i, t: (0, 0)),
                  _whole_vmem(),
                  pl.BlockSpec((1, 1, 2, d), lambda i, t: (i, jnp.minimum(t, 1), 0, 0)),
                  pl.BlockSpec((1, d), lambda i, t: (0, 0)),
                  _whole_vmem(), _whole_vmem(), _whole_vmem(), _whole_vmem(),
                  ktab, ktab, ktab, qtab, qtab],
        out_specs=[pl.BlockSpec((1, TOK_TILE, d), lambda i, t: (i, lat_t(t), 0)),
                   pl.BlockSpec((1, TOK_TILE, di), lambda i, t: (i, t, 0)),
                   pl.BlockSpec((1, di, TOK_TILE), lambda i, t: (i, 0, t)),
                   pl.BlockSpec((1, di, TOK_TILE), lambda i, t: (i, 0, lat_t(t))),
                   pl.BlockSpec((1, TOK_TILE, di), lambda i, t: (i, lat_t(t), 0))],
        out_shape=[jax.ShapeDtypeStruct((b, l_lat, d), F32),
                   jax.ShapeDtypeStruct((b, tot, di), BF16),
                   jax.ShapeDtypeStruct((b, di, tot), BF16),
                   jax.ShapeDtypeStruct((b, di, l_lat), BF16),
                   jax.ShapeDtypeStruct((b, l_lat, di), BF16)],
        scratch_shapes=[pltpu.VMEM((TOK_TILE, di), BF16)],
        compiler_params=_params(("arbitrary", "arbitrary")),
        name="hg_out_da_proj",
    )(of, ob, gate0, ctx, x, gate_mod, hg_norm_g, w_out, mod, norm_g,
      wq.T.astype(BF16), wk.astype(BF16), wv.T.astype(BF16), wg.astype(BF16),
      *[jnp.asarray(a) for a in ktabs], jnp.asarray(qc), jnp.asarray(qs))


def _da_attn_kernel(k_ref, vt_ref, qt_ref, lam_ref, sg_ref, o_ref, pa_ref, pb_ref, a_ref, la_ref, lb_ref,
                    *, lambda_init):
    lq1, lk1, lq2, lk2 = (lam_ref[i:i + 1, :] for i in range(4))
    lam = (jnp.exp(jnp.sum(lq1 * lk1, axis=-1, keepdims=True))
           - jnp.exp(jnp.sum(lq2 * lk2, axis=-1, keepdims=True)) + lambda_init)
    out_gain = sg_ref[...] * (1.0 - lambda_init)
    n_chunk = k_ref.shape[1] // KEY_CHUNK
    n_q = qt_ref.shape[-1] // Q_TILE
    w2 = 2 * Q_TILE

    kf = k_ref[0].astype(F32)
    sel_r = lax.broadcasted_iota(jnp.int32, (HEAD_W, HEAD_W), 0) // DA_DQK
    sel_c = lax.broadcasted_iota(jnp.int32, (HEAD_W, HEAD_W), 1)
    sel = jnp.where(sel_r == sel_c, 1.0, 0.0).astype(BF16)
    norm2 = jnp.dot((kf * kf).astype(BF16), sel, preferred_element_type=F32)
    kmax = jnp.sqrt(jnp.max(norm2, axis=0, keepdims=True)) * BOUND_SLACK
    kmax = jnp.concatenate([jnp.broadcast_to(kmax[:, 0:1], (1, Q_TILE)),
                            jnp.broadcast_to(kmax[:, 1:2], (1, Q_TILE))], axis=1)
    half0 = lax.broadcasted_iota(jnp.int32, (HEAD_W, Q_TILE), 0) < DA_DQK

    def fold(x):
        return x.reshape(KEY_CHUNK // 8, 8, w2)

    def exp_stage(i, p_ref, l_ref, exact):
        start = i * Q_TILE if isinstance(i, int) else pl.multiple_of(i * Q_TILE, Q_TILE)
        qt = qt_ref[0, :, pl.ds(start, Q_TILE)]
        zero = jnp.zeros_like(qt)
        rhs = jnp.concatenate([jnp.where(half0, qt, zero), jnp.where(half0, zero, qt)], axis=1)

        def scores(c):
            return jnp.dot(k_ref[0, c * KEY_CHUNK:(c + 1) * KEY_CHUNK, :], rhs, preferred_element_type=F32)

        if exact:
            mx = jnp.full((8, w2), -jnp.inf, F32)
            for c in range(n_chunk):
                mx = jnp.maximum(mx, jnp.max(fold(scores(c)), axis=0))
            m = jnp.max(mx, axis=0, keepdims=True)
        else:
            rf = rhs.astype(F32)
            m = jnp.sqrt(jnp.sum(rf * rf, axis=0, keepdims=True)) * kmax
        l8 = jnp.zeros((8, w2), F32)
        for c in range(n_chunk):
            p = jnp.exp2(scores(c) - m)
            l8 = l8 + jnp.sum(fold(p), axis=0)
            p_ref[c * KEY_CHUNK:(c + 1) * KEY_CHUNK, :] = p.astype(BF16)
        l_ref[...] = l8
        return jnp.where(l8 > MIN_COLUMN_SUM, 0.0, 1.0)

    def value_stage(i, p_ref, l_ref):
        start = i * Q_TILE if isinstance(i, int) else pl.multiple_of(i * Q_TILE, Q_TILE)
        l = jnp.sum(l_ref[...], axis=0, keepdims=True)
        l0, l1 = l[:, :Q_TILE], l[:, Q_TILE:]
        rho = (lam * l0 / l1).astype(BF16)
        for c in range(n_chunk):
            rows = slice(c * KEY_CHUNK, (c + 1) * KEY_CHUNK)
            a_ref[rows, :] = p_ref[rows, :Q_TILE] - p_ref[rows, Q_TILE:] * rho
        ot = jnp.dot(vt_ref[0], a_ref[...], preferred_element_type=F32) * (1.0 / l0)
        ot = ot * lax.rsqrt(jnp.mean(ot * ot, axis=0, keepdims=True) + EPS) * out_gain
        o_ref[0, pl.ds(start, Q_TILE), :] = ot.T.astype(o_ref.dtype)

    even = (pa_ref, la_ref)
    odd = (pb_ref, lb_ref)
    bad = exp_stage(0, *even, exact=False)

    def body(j, bad):
        i = 2 * j
        bad = jnp.maximum(bad, exp_stage(i + 1, *odd, exact=False))
        value_stage(i, *even)
        bad = jnp.maximum(bad, exp_stage(i + 2, *even, exact=False))
        value_stage(i + 1, *odd)
        return bad

    bad = lax.fori_loop(0, n_q // 2 - 1, body, bad)
    bad = jnp.maximum(bad, exp_stage(n_q - 1, *odd, exact=False))
    value_stage(n_q - 2, *even)
    value_stage(n_q - 1, *odd)

    @pl.when(jnp.max(bad) > 0.0)
    def _():
        def redo(i, carry):
            exp_stage(i, *even, exact=True)
            value_stage(i, *even)
            return carry

        lax.fori_loop(0, n_q, redo, 0)


def _da_attn(k, vt, qt, lam_vecs, subln_g, lambda_init):
    b, tot, di = k.shape
    l_lat = qt.shape[-1]
    n_head = di // HEAD_W
    return pl.pallas_call(
        functools.partial(_da_attn_kernel, lambda_init=lambda_init),
        grid=(b, n_head),
        in_specs=[pl.BlockSpec((1, tot, HEAD_W), lambda i, h: (i, 0, h)),
                  pl.BlockSpec((1, HEAD_W, tot), lambda i, h: (i, h, 0)),
                  pl.BlockSpec((1, HEAD_W, l_lat), lambda i, h: (i, h, 0)),
                  pl.BlockSpec((4, DA_DQK), lambda i, h: (0, 0)),
                  pl.BlockSpec((HEAD_W, 1), lambda i, h: (0, 0))],
        out_specs=pl.BlockSpec((1, l_lat, HEAD_W), lambda i, h: (i, 0, h)),
        out_shape=jax.ShapeDtypeStruct((b, l_lat, di), BF16),
        scratch_shapes=[pltpu.VMEM((tot, 2 * Q_TILE), BF16),
                        pltpu.VMEM((tot, 2 * Q_TILE), BF16),
                        pltpu.VMEM((tot, Q_TILE), BF16),
                        pltpu.VMEM((8, 2 * Q_TILE), F32),
                        pltpu.VMEM((8, 2 * Q_TILE), F32)],
        compiler_params=_params(("arbitrary", "arbitrary")),
        name="da_attn",
    )(k, vt, qt, lam_vecs, subln_g.reshape(HEAD_W, 1))


def _da_out_kernel(o_ref, gate_ref, x_ref, gm_ref, fg_ref, w_ref, out_ref):
    y = o_ref[0] * gate_ref[0]
    x = x_ref[0] + gm_ref[0] * jnp.dot(y, w_ref[...], preferred_element_type=F32)
    out_ref[0] = x * lax.rsqrt(jnp.mean(x * x, axis=-1, keepdims=True) + EPS) * fg_ref[...]


def _da_out(o, gate, x1, gate_mod, final_g, w_out):
    b, l_lat, di = o.shape
    d = x1.shape[-1]
    tile = math.gcd(l_lat, OUT_TILE)
    tokw = pl.BlockSpec((1, tile, di), lambda i, t: (i, t, 0))
    return pl.pallas_call(
        _da_out_kernel,
        grid=(b, l_lat // tile),
        in_specs=[tokw, tokw,
                  pl.BlockSpec((1, tile, d), lambda i, t: (i, t, 0)),
                  pl.BlockSpec((1, 1, d), lambda i, t: (i, 0, 0)),
                  pl.BlockSpec((1, d), lambda i, t: (0, 0)),
                  _whole_vmem()],
        out_specs=pl.BlockSpec((1, tile, d), lambda i, t: (i, t, 0)),
        out_shape=jax.ShapeDtypeStruct((b, l_lat, d), F32),
        compiler_params=_params(("arbitrary", "arbitrary")),
        name="da_out",
    )(o, gate, x1, gate_mod, final_g, w_out)


def kernel(x, c, ctx, c_ctx, w_ada, b_ada, norm_g, hg_w_in, hg_lb_logits, hg_norm_g, hg_w_out,
           da_w_in, da_lam_q1, da_lam_k1, da_lam_q2, da_lam_k2, da_subln_g, da_w_out, final_g):
    b, l_lat, d = x.shape
    assert ctx.shape[1] == TOK_TILE and l_lat % TOK_TILE == 0 and l_lat % GRID_W == 0
    assert w_ada.shape[0] == 2 and hg_w_in.shape[0] == 1 and da_w_in.shape[0] == 1
    assert hg_lb_logits.shape[0] == 2

    rows = -(-(b + 1) // 8) * 8
    cc = jnp.concatenate([c, c_ctx[None], jnp.zeros((rows - b - 1, d), F32)], axis=0)
    ada = _adaln(cc, w_ada, b_ada)

    def mods(layer):
        m = ada[layer].reshape(rows, 3, d)
        pair = jnp.stack([jnp.broadcast_to(m[b], (b, 3, d)), m[:b]], axis=1)
        return pair[:, :, 0:2], pair[:, :, 2:3]

    mod0, gmod0 = mods(0)
    mod1, gmod1 = mods(1)

    q, gf, gb, v, gate0 = _hg_proj(ctx, x, mod0, norm_g[0:1], hg_w_in[0].astype(BF16), hg_lb_logits)
    of, ob = _hg_scan(q, gf, gb, v)
    x1, k, vt, qt, gate1 = _mid(of, ob, gate0, ctx, x, gmod0, hg_norm_g, hg_w_out[0].astype(BF16),
                                mod1, norm_g[1:2], da_w_in[0])

    lambda_init = 0.8 - 0.6 * math.exp(-0.3 * 1)
    lam_vecs = jnp.concatenate([da_lam_q1, da_lam_k1, da_lam_q2, da_lam_k2], axis=0)
    o = _da_attn(k, vt, qt, lam_vecs, da_subln_g[0], lambda_init)
    return _da_out(o, gate1, x1, gmod1[:, 1], final_g[None], da_w_out[0].astype(BF16))
```

```python
import functools
import math

import numpy as np
import jax
import jax.numpy as jnp
from jax import lax
from jax.experimental import pallas as pl
from jax.experimental.pallas import tpu as pltpu

F32 = jnp.float32
BF16 = jnp.bfloat16

EPS = 1e-6
GRID_W = 64
ROPE_THETA = 10000.0
HEAD_W = 128
DA_DQK = 64
TOK_TILE = 256
OUT_TILE = 1024
SCAN_CHUNK = 64
SCAN_HEADS = 16
Q_TILE = 256
KEY_CHUNK = 256
BOUND_SLACK = 1.01
MIN_COLUMN_SUM = 2.0 ** -64
EXP2_CLAMP = 115.0
LOG2E = 1.4426950408889634
VMEM_LIMIT = 56 * 1024 * 1024

NT = (((1,), (1,)), ((), ()))
TN = (((0,), (0,)), ((), ()))


def _sigmoid(x):
    return 1.0 / (1.0 + jnp.exp(-x))


def _silu(x):
    return x * _sigmoid(x)


def _params(sem):
    return pltpu.CompilerParams(dimension_semantics=sem, vmem_limit_bytes=VMEM_LIMIT)


def _whole_vmem():
    return pl.BlockSpec(memory_space=pltpu.VMEM)


def _adaln_kernel(c_ref, w_ref, b_ref, o_ref):
    s = _silu(c_ref[...]).astype(BF16)
    o_ref[0] = jnp.dot(s, w_ref[0], preferred_element_type=F32) + b_ref[0]


def _adaln(cc, w_ada, b_ada):
    depth, d, d3 = w_ada.shape
    rows = cc.shape[0]
    return pl.pallas_call(
        _adaln_kernel,
        grid=(depth,),
        in_specs=[pl.BlockSpec((rows, d), lambda i: (0, 0)),
                  pl.BlockSpec((1, d, d3), lambda i: (i, 0, 0)),
                  pl.BlockSpec((1, 1, d3), lambda i: (i, 0, 0))],
        out_specs=pl.BlockSpec((1, rows, d3), lambda i: (i, 0, 0)),
        out_shape=jax.ShapeDtypeStruct((depth, rows, d3), F32),
        compiler_params=_params(("arbitrary",)),
        name="adaln",
    )(cc, w_ada.astype(BF16), b_ada.reshape(depth, 1, d3))


def _modulated_norm(x, g, shift, scale):
    y = x * lax.rsqrt(jnp.mean(x * x, axis=-1, keepdims=True) + EPS)
    return (y * g) * (1.0 + scale) + shift


def _hg_proj_kernel(ctx_ref, x_ref, mod_ref, ng_ref, w_ref, lbl_ref,
                    q_ref, gf_ref, gb_ref, v_ref, gate_ref):
    t = pl.program_id(1)
    x = jnp.where(t == 0, ctx_ref[0], x_ref[0])
    h = _modulated_norm(x, ng_ref[...], mod_ref[0, 0, 0:1, :], mod_ref[0, 0, 1:2, :]).astype(BF16)
    di = q_ref.shape[-1]

    def seg(j):
        return jnp.dot(h, w_ref[:, j * di:(j + 1) * di], preferred_element_type=F32)

    def log_forget(z, d):
        l0 = lbl_ref[0, d:d + 1, :]
        l1 = lbl_ref[1, d:d + 1, :]
        m = jnp.maximum(l0, l1)
        e0 = jnp.exp(l0 - m)
        lb = e0 / (e0 + jnp.exp(l1 - m))
        return jnp.log2(lb + (1.0 - lb) * _sigmoid(z))

    q_ref[0] = _silu(seg(0)).astype(BF16)
    gf_ref[0] = log_forget(seg(1), 0).astype(BF16)
    gb_ref[0] = log_forget(seg(2), 1).astype(BF16)
    v_ref[0] = seg(3).astype(BF16)
    gate_ref[0] = _silu(seg(4)).astype(BF16)


def _hg_proj(ctx, x, mod, norm_g, w_in, lb_logits):
    b, l, d = x.shape
    di = w_in.shape[1] // 5
    nt = l // TOK_TILE + 1
    tot = l + TOK_TILE
    tok = lambda dt: jax.ShapeDtypeStruct((b, tot, di), dt)
    out_blk = pl.BlockSpec((1, TOK_TILE, di), lambda i, t: (i, t, 0))
    return pl.pallas_call(
        _hg_proj_kernel,
        grid=(b, nt),
        in_specs=[pl.BlockSpec((1, TOK_TILE, d), lambda i, t: (i, 0, 0)),
                  pl.BlockSpec((1, TOK_TILE, d), lambda i, t: (i, jnp.maximum(t - 1, 0), 0)),
                  pl.BlockSpec((1, 1, 2, d), lambda i, t: (i, jnp.minimum(t, 1), 0, 0)),
                  pl.BlockSpec((1, d), lambda i, t: (0, 0)),
                  _whole_vmem(),
                  pl.BlockSpec((2, 2, di), lambda i, t: (0, 0, 0))],
        out_specs=[out_blk] * 5,
        out_shape=[tok(BF16)] * 5,
        compiler_params=_params(("arbitrary", "arbitrary")),
        name="hg_proj",
    )(ctx, x, mod, norm_g, w_in, lb_logits)


def _hg_scan_kernel(qf_ref, gf_ref, vf_ref, qb_ref, gb_ref, vb_ref, of_ref, ob_ref,
                    s_ref, bf_ref, bb_ref, tri_ref):
    t = pl.program_id(2)
    c = SCAN_CHUNK
    n_chunk = TOK_TILE // c
    w = SCAN_HEADS * HEAD_W

    @pl.when((pl.program_id(0) == 0) & (pl.program_id(1) == 0) & (t == 0))
    def _():
        row = lax.broadcasted_iota(jnp.int32, (TOK_TILE, TOK_TILE), 0)
        col = lax.broadcasted_iota(jnp.int32, (TOK_TILE, TOK_TILE), 1)
        same = (row // c) == (col // c)
        tri_ref[0] = jnp.where(same & (col <= row), 1.0, 0.0).astype(BF16)
        tri_ref[1] = jnp.where(same & (col >= row), 1.0, 0.0).astype(BF16)

    @pl.when(t == 0)
    def _():
        s_ref[...] = jnp.zeros_like(s_ref)

    def cumsum(tri, g):
        return jnp.dot(tri, g, preferred_element_type=F32)

    bf_ref[...] = cumsum(tri_ref[0], gf_ref[0])
    bb_ref[...] = cumsum(tri_ref[1], gb_ref[0])

    def chunk_decay_columns(b_ref, last_row):
        rows = [b_ref[j * c + last_row:j * c + last_row + 1, :] for j in range(n_chunk)]
        rows.append(jnp.zeros((8 - n_chunk, w), F32))
        return jnp.exp2(jnp.concatenate(rows, axis=0).T)

    dec_f = chunk_decay_columns(bf_ref, c - 1)
    dec_b = chunk_decay_columns(bb_ref, 0)

    ri = lax.broadcasted_iota(jnp.int32, (c, c), 0)
    ci = lax.broadcasted_iota(jnp.int32, (c, c), 1)
    keep_f = ci <= ri
    keep_b = ci >= ri

    def chunk_head(direction, j, hh, q_ref, g_ref, v_ref, b_ref, o_ref, dec, keep, mid_row, last_row):
        rows = slice(j * c, (j + 1) * c)
        lanes = slice(hh * HEAD_W, (hh + 1) * HEAD_W)
        q = q_ref[0, rows, lanes]
        g = g_ref[0, rows, lanes].astype(F32)
        v = v_ref[0, rows, lanes]
        bcum = b_ref[rows, lanes]
        r = bcum[mid_row:mid_row + 1, :]
        b_last = bcum[last_row:last_row + 1, :]
        d = jnp.clip(bcum - r, -EXP2_CLAMP, EXP2_CLAMP)
        q_mid = q * jnp.exp2(d).astype(BF16)
        k_mid = ((1.0 - jnp.exp2(g)) * jnp.exp2(-d)).astype(BF16)
        q_dec = q_mid * jnp.exp2(r).astype(BF16)
        k_dec = k_mid * jnp.exp2(b_last - r).astype(BF16)
        a = lax.dot_general(q_mid, k_mid, NT, preferred_element_type=F32)
        a = jnp.where(keep, a, 0.0).astype(BF16)
        s = s_ref[direction, hh]
        o_ref[0, rows, lanes] = jnp.dot(jnp.concatenate([q_dec, a], axis=1),
                                        jnp.concatenate([s.astype(BF16), v], axis=0),
                                        preferred_element_type=F32).astype(o_ref.dtype)
        s_ref[direction, hh] = s * dec[lanes, j:j + 1] + lax.dot_general(
            k_dec, v, TN, preferred_element_type=F32)

    for i in range(n_chunk):
        for hh in range(SCAN_HEADS):
            chunk_head(0, i, hh, qf_ref, gf_ref, vf_ref, bf_ref, of_ref, dec_f, keep_f, c // 2 - 1, c - 1)
            chunk_head(1, n_chunk - 1 - i, hh, qb_ref, gb_ref, vb_ref, bb_ref, ob_ref, dec_b, keep_b, c // 2, 0)


def _hg_scan(q, gf, gb, v):
    b, tot, di = q.shape
    nt = tot // TOK_TILE
    w = SCAN_HEADS * HEAD_W
    fwd = pl.BlockSpec((1, TOK_TILE, w), lambda i, h, t: (i, t, h))
    bwd = pl.BlockSpec((1, TOK_TILE, w), lambda i, h, t: (i, jnp.where(t == 0, 0, nt - t), h))
    out = jax.ShapeDtypeStruct((b, tot, di), BF16)
    return pl.pallas_call(
        _hg_scan_kernel,
        grid=(b, di // w, nt),
        in_specs=[fwd, fwd, fwd, bwd, bwd, bwd],
        out_specs=[fwd, bwd],
        out_shape=[out, out],
        scratch_shapes=[pltpu.VMEM((2, SCAN_HEADS, HEAD_W, HEAD_W), F32),
                        pltpu.VMEM((TOK_TILE, w), F32),
                        pltpu.VMEM((TOK_TILE, w), F32),
                        pltpu.VMEM((2, TOK_TILE, TOK_TILE), BF16)],
        compiler_params=_params(("arbitrary", "arbitrary", "arbitrary")),
        name="hg_scan",
    )(q, gf, v, q, gb, v)


def _rope_lane_tables(l_lat):
    ax = DA_DQK // 2
    inv = 1.0 / (ROPE_THETA ** (np.arange(0, ax, 2, dtype=np.float64) / ax))
    pos = np.arange(l_lat)
    ang_r = (pos // GRID_W)[:, None] * inv
    ang_c = (pos % GRID_W)[:, None] * inv
    zero = np.zeros_like(ang_r)
    cos64 = np.concatenate([np.cos(ang_r), np.cos(ang_r), np.cos(ang_c), np.cos(ang_c)], axis=1)
    up64 = np.concatenate([-np.sin(ang_r), zero, -np.sin(ang_c), zero], axis=1)
    dn64 = np.concatenate([zero, np.sin(ang_r), zero, np.sin(ang_c)], axis=1)
    lat = [np.concatenate([t, t], axis=1) for t in (cos64, up64, dn64)]
    ident = [np.ones((TOK_TILE, HEAD_W)), np.zeros((TOK_TILE, HEAD_W)), np.zeros((TOK_TILE, HEAD_W))]
    return [np.concatenate([i, t], axis=0).astype(np.float32) for i, t in zip(ident, lat)], (cos64, up64, dn64)


def _mid_kernel(of_ref, ob_ref, gate0_ref, ctx_ref, x_ref, gm_ref, hng_ref, wo_ref,
                mod_ref, ng_ref, wq_ref, wk_ref, wv_ref, wg_ref,
                kc_ref, ku_ref, kd_ref, qc_ref, qs_ref,
                x1_ref, k_ref, vt_ref, qt_ref, gate_ref, y_ref):
    t = pl.program_id(1)
    n_head = k_ref.shape[-1] // HEAD_W

    for hh in range(n_head):
        lanes = slice(hh * HEAD_W, (hh + 1) * HEAD_W)
        o = of_ref[0, :, lanes].astype(F32) + ob_ref[0, :, lanes].astype(F32)
        o = o * lax.rsqrt(jnp.mean(o * o, axis=-1, keepdims=True) + EPS) * hng_ref[:, lanes]
        y_ref[:, lanes] = (o * gate0_ref[0, :, lanes].astype(F32)).astype(BF16)
    y = jnp.dot(y_ref[...], wo_ref[...], preferred_element_type=F32)
    x1 = jnp.where(t == 0, ctx_ref[0], x_ref[0]) + gm_ref[0, 0] * y

    h = _modulated_norm(x1, ng_ref[...], mod_ref[0, 0, 0:1, :], mod_ref[0, 0, 1:2, :]).astype(BF16)
    k = jnp.dot(h, wk_ref[...], preferred_element_type=F32)
    cos, s_up, s_dn = kc_ref[...], ku_ref[...], kd_ref[...]
    for hh in range(n_head):
        lanes = slice(hh * HEAD_W, (hh + 1) * HEAD_W)
        kh = k[:, lanes]
        rot = (kh * cos + pltpu.roll(kh, HEAD_W - 16, axis=1) * s_up
               + pltpu.roll(kh, 16, axis=1) * s_dn)
        k_ref[0, :, lanes] = rot.astype(BF16)

    vt_ref[0] = lax.dot_general(wv_ref[...], h, NT, preferred_element_type=F32).astype(BF16)

    @pl.when(t > 0)
    def _():
        x1_ref[0] = x1
        qt = lax.dot_general(wq_ref[...], h, NT, preferred_element_type=F32)
        qc, qs = qc_ref[...], qs_ref[...]
        for grp in range(qt.shape[0] // DA_DQK):
            x = qt[grp * DA_DQK:(grp + 1) * DA_DQK]
            partner = jnp.concatenate([x[16:32], x[0:16], x[48:64], x[32:48]], axis=0)
            qt_ref[0, grp * DA_DQK:(grp + 1) * DA_DQK, :] = (x * qc + partner * qs).astype(BF16)
        gate_ref[0] = _silu(jnp.dot(h, wg_ref[...], preferred_element_type=F32)).astype(BF16)


def _mid(of, ob, gate0, ctx, x, gate_mod, hg_norm_g, w_out, mod, norm_g, w_in):
    b, tot, di = of.shape
    l_lat, d = x.shape[1], x.shape[2]
    nt = tot // TOK_TILE
    wq, wk, wv, wg = (w_in[:, j * di:(j + 1) * di] for j in range(4))
    ktabs, (cos64, up64, dn64) = _rope_lane_tables(l_lat)
    q_scale = DA_DQK ** -0.5 * LOG2E
    qc = (cos64.T * q_scale).astype(np.float32)
    qs = ((up64 + dn64).T * q_scale).astype(np.float32)
    lat_t = lambda t: jnp.maximum(t - 1, 0)
    tokw = pl.BlockSpec((1, TOK_TILE, di), lambda i, t: (i, t, 0))
    ktab = pl.BlockSpec((TOK_TILE, HEAD_W), lambda i, t: (t, 0))
    qtab = pl.BlockSpec((DA_DQK, TOK_TILE), lambda i, t: (0, lat_t(t)))
    return pl.pallas_call(
        _mid_kernel,
        grid=(b, nt),
        in_specs=[tokw, tokw, tokw,
                  pl.BlockSpec((1, TOK_TILE, d), lambda i, t: (i, 0, 0)),
                  pl.BlockSpec((1, TOK_TILE, d), lambda i, t: (i, lat_t(t), 0)),
                  pl.BlockSpec((1, 1, 1, d), lambda i, t: (i, jnp.minimum(t, 1), 0, 0)),
                  pl.BlockSpec((1, di), lambda i, t: (0, 0)),
                  _whole_vmem(),
                  pl.BlockSpec((1, 1, 2, d), lambda i, t: (i, jnp.minimum(t, 1), 0, 0)),
                  pl.BlockSpec((1, d), lambda i, t: (0, 0)),
                  _whole_vmem(), _whole_vmem(), _whole_vmem(), _whole_vmem(),
                  ktab, ktab, ktab, qtab, qtab],
        out_specs=[pl.BlockSpec((1, TOK_TILE, d), lambda i, t: (i, lat_t(t), 0)),
                   pl.BlockSpec((1, TOK_TILE, di), lambda i, t: (i, t, 0)),
                   pl.BlockSpec((1, di, TOK_TILE), lambda i, t: (i, 0, t)),
                   pl.BlockSpec((1, di, TOK_TILE), lambda i, t: (i, 0, lat_t(t))),
                   pl.BlockSpec((1, TOK_TILE, di), lambda i, t: (i, lat_t(t), 0))],
        out_shape=[jax.ShapeDtypeStruct((b, l_lat, d), F32),
                   jax.ShapeDtypeStruct((b, tot, di), BF16),
                   jax.ShapeDtypeStruct((b, di, tot), BF16),
                   jax.ShapeDtypeStruct((b, di, l_lat), BF16),
                   jax.ShapeDtypeStruct((b, l_lat, di), BF16)],
        scratch_shapes=[pltpu.VMEM((TOK_TILE, di), BF16)],
        compiler_params=_params(("arbitrary", "arbitrary")),
        name="hg_out_da_proj",
    )(of, ob, gate0, ctx, x, gate_mod, hg_norm_g, w_out, mod, norm_g,
      wq.T.astype(BF16), wk.astype(BF16), wv.T.astype(BF16), wg.astype(BF16),
      *[jnp.asarray(a) for a in ktabs], jnp.asarray(qc), jnp.asarray(qs))


def _da_attn_kernel(k_ref, vt_ref, qt_ref, lam_ref, sg_ref, o_ref, pa_ref, pb_ref, a_ref, la_ref, lb_ref,
                    ota_ref, otb_ref, *, lambda_init):
    lq1, lk1, lq2, lk2 = (lam_ref[i:i + 1, :] for i in range(4))
    lam = (jnp.exp(jnp.sum(lq1 * lk1, axis=-1, keepdims=True))
           - jnp.exp(jnp.sum(lq2 * lk2, axis=-1, keepdims=True)) + lambda_init)
    out_gain = sg_ref[...] * (1.0 - lambda_init)
    n_chunk = k_ref.shape[1] // KEY_CHUNK
    n_q = qt_ref.shape[-1] // Q_TILE
    w2 = 2 * Q_TILE

    kf = k_ref[0].astype(F32)
    sel_r = lax.broadcasted_iota(jnp.int32, (HEAD_W, HEAD_W), 0) // DA_DQK
    sel_c = lax.broadcasted_iota(jnp.int32, (HEAD_W, HEAD_W), 1)
    sel = jnp.where(sel_r == sel_c, 1.0, 0.0).astype(BF16)
    norm2 = jnp.dot((kf * kf).astype(BF16), sel, preferred_element_type=F32)
    kmax = jnp.sqrt(jnp.max(norm2, axis=0, keepdims=True)) * BOUND_SLACK
    kmax = jnp.concatenate([jnp.broadcast_to(kmax[:, 0:1], (1, Q_TILE)),
                            jnp.broadcast_to(kmax[:, 1:2], (1, Q_TILE))], axis=1)
    half0 = lax.broadcasted_iota(jnp.int32, (HEAD_W, Q_TILE), 0) < DA_DQK

    def fold(x):
        return x.reshape(KEY_CHUNK // 8, 8, w2)

    def exp_stage(i, p_ref, l_ref, exact):
        start = i * Q_TILE if isinstance(i, int) else pl.multiple_of(i * Q_TILE, Q_TILE)
        qt = qt_ref[0, :, pl.ds(start, Q_TILE)]
        zero = jnp.zeros_like(qt)
        rhs = jnp.concatenate([jnp.where(half0, qt, zero), jnp.where(half0, zero, qt)], axis=1)

        def scores(c):
            return jnp.dot(k_ref[0, c * KEY_CHUNK:(c + 1) * KEY_CHUNK, :], rhs, preferred_element_type=F32)

        if exact:
            mx = jnp.full((8, w2), -jnp.inf, F32)
            for c in range(n_chunk):
                mx = jnp.maximum(mx, jnp.max(fold(scores(c)), axis=0))
            m = jnp.max(mx, axis=0, keepdims=True)
        else:
            rf = rhs.astype(F32)
            m = jnp.sqrt(jnp.sum(rf * rf, axis=0, keepdims=True)) * kmax
        l8 = jnp.zeros((8, w2), F32)
        for c in range(n_chunk):
            p = jnp.exp2(scores(c) - m)
            l8 = l8 + jnp.sum(fold(p), axis=0)
            p_ref[c * KEY_CHUNK:(c + 1) * KEY_CHUNK, :] = p.astype(BF16)
        l_ref[...] = l8
        return jnp.where(l8 > MIN_COLUMN_SUM, 0.0, 1.0)

    def value_stage(p_ref, l_ref, ot_ref):
        l = jnp.sum(l_ref[...], axis=0, keepdims=True)
        l0, l1 = l[:, :Q_TILE], l[:, Q_TILE:]
        rho = (lam * l0 / l1).astype(BF16)
        for c in range(n_chunk):
            rows = slice(c * KEY_CHUNK, (c + 1) * KEY_CHUNK)
            a_ref[rows, :] = p_ref[rows, :Q_TILE] - p_ref[rows, Q_TILE:] * rho
        ot_ref[...] = jnp.dot(vt_ref[0], a_ref[...], preferred_element_type=F32) * (1.0 / l0)

    def store_stage(i, ot_ref):
        start = i * Q_TILE if isinstance(i, int) else pl.multiple_of(i * Q_TILE, Q_TILE)
        ot = ot_ref[...]
        ot = ot * lax.rsqrt(jnp.mean(ot * ot, axis=0, keepdims=True) + EPS) * out_gain
        o_ref[0, pl.ds(start, Q_TILE), :] = ot.T.astype(o_ref.dtype)

    even = (pa_ref, la_ref, ota_ref)
    odd = (pb_ref, lb_ref, otb_ref)
    bad = exp_stage(0, *even[:2], exact=False)
    bad = jnp.maximum(bad, exp_stage(1, *odd[:2], exact=False))
    value_stage(*even)

    def body(j, bad):
        s = 2 * j + 2
        bad = jnp.maximum(bad, exp_stage(s, *even[:2], exact=False))
        value_stage(*odd)
        store_stage(s - 2, even[2])
        bad = jnp.maximum(bad, exp_stage(s + 1, *odd[:2], exact=False))
        value_stage(*even)
        store_stage(s - 1, odd[2])
        return bad

    bad = lax.fori_loop(0, n_q // 2 - 1, body, bad)
    value_stage(*odd)
    store_stage(n_q - 2, even[2])
    store_stage(n_q - 1, odd[2])

    @pl.when(jnp.max(bad) > 0.0)
    def _():
        def redo(i, carry):
            exp_stage(i, *even[:2], exact=True)
            value_stage(*even)
            store_stage(i, even[2])
            return carry

        lax.fori_loop(0, n_q, redo, 0)


def _da_attn(k, vt, qt, lam_vecs, subln_g, lambda_init):
    b, tot, di = k.shape
    l_lat = qt.shape[-1]
    n_head = di // HEAD_W
    return pl.pallas_call(
        functools.partial(_da_attn_kernel, lambda_init=lambda_init),
        grid=(b, n_head),
        in_specs=[pl.BlockSpec((1, tot, HEAD_W), lambda i, h: (i, 0, h)),
                  pl.BlockSpec((1, HEAD_W, tot), lambda i, h: (i, h, 0)),
                  pl.BlockSpec((1, HEAD_W, l_lat), lambda i, h: (i, h, 0)),
                  pl.BlockSpec((4, DA_DQK), lambda i, h: (0, 0)),
                  pl.BlockSpec((HEAD_W, 1), lambda i, h: (0, 0))],
        out_specs=pl.BlockSpec((1, l_lat, HEAD_W), lambda i, h: (i, 0, h)),
        out_shape=jax.ShapeDtypeStruct((b, l_lat, di), BF16),
        scratch_shapes=[pltpu.VMEM((tot, 2 * Q_TILE), BF16),
                        pltpu.VMEM((tot, 2 * Q_TILE), BF16),
                        pltpu.VMEM((tot, Q_TILE), BF16),
                        pltpu.VMEM((8, 2 * Q_TILE), F32),
                        pltpu.VMEM((8, 2 * Q_TILE), F32),
                        pltpu.VMEM((HEAD_W, Q_TILE), F32),
                        pltpu.VMEM((HEAD_W, Q_TILE), F32)],
        compiler_params=_params(("arbitrary", "arbitrary")),
        name="da_attn",
    )(k, vt, qt, lam_vecs, subln_g.reshape(HEAD_W, 1))


def _da_out_kernel(o_ref, gate_ref, x_ref, gm_ref, fg_ref, w_ref, out_ref):
    y = o_ref[0] * gate_ref[0]
    x = x_ref[0] + gm_ref[0] * jnp.dot(y, w_ref[...], preferred_element_type=F32)
    out_ref[0] = x * lax.rsqrt(jnp.mean(x * x, axis=-1, keepdims=True) + EPS) * fg_ref[...]


def _da_out(o, gate, x1, gate_mod, final_g, w_out):
    b, l_lat, di = o.shape
    d = x1.shape[-1]
    tile = math.gcd(l_lat, OUT_TILE)
    tokw = pl.BlockSpec((1, tile, di), lambda i, t: (i, t, 0))
    return pl.pallas_call(
        _da_out_kernel,
        grid=(b, l_lat // tile),
        in_specs=[tokw, tokw,
                  pl.BlockSpec((1, tile, d), lambda i, t: (i, t, 0)),
                  pl.BlockSpec((1, 1, d), lambda i, t: (i, 0, 0)),
                  pl.BlockSpec((1, d), lambda i, t: (0, 0)),
                  _whole_vmem()],
        out_specs=pl.BlockSpec((1, tile, d), lambda i, t: (i, t, 0)),
        out_shape=jax.ShapeDtypeStruct((b, l_lat, d), F32),
        compiler_params=_params(("arbitrary", "arbitrary")),
        name="da_out",
    )(o, gate, x1, gate_mod, final_g, w_out)


def kernel(x, c, ctx, c_ctx, w_ada, b_ada, norm_g, hg_w_in, hg_lb_logits, hg_norm_g, hg_w_out,
           da_w_in, da_lam_q1, da_lam_k1, da_lam_q2, da_lam_k2, da_subln_g, da_w_out, final_g):
    b, l_lat, d = x.shape
    assert ctx.shape[1] == TOK_TILE and l_lat % TOK_TILE == 0 and l_lat % GRID_W == 0
    assert w_ada.shape[0] == 2 and hg_w_in.shape[0] == 1 and da_w_in.shape[0] == 1
    assert hg_lb_logits.shape[0] == 2

    rows = -(-(b + 1) // 8) * 8
    cc = jnp.concatenate([c, c_ctx[None], jnp.zeros((rows - b - 1, d), F32)], axis=0)
    ada = _adaln(cc, w_ada, b_ada)

    def mods(layer):
        m = ada[layer].reshape(rows, 3, d)
        pair = jnp.stack([jnp.broadcast_to(m[b], (b, 3, d)), m[:b]], axis=1)
        return pair[:, :, 0:2], pair[:, :, 2:3]

    mod0, gmod0 = mods(0)
    mod1, gmod1 = mods(1)

    q, gf, gb, v, gate0 = _hg_proj(ctx, x, mod0, norm_g[0:1], hg_w_in[0].astype(BF16), hg_lb_logits)
    of, ob = _hg_scan(q, gf, gb, v)
    x1, k, vt, qt, gate1 = _mid(of, ob, gate0, ctx, x, gmod0, hg_norm_g, hg_w_out[0].astype(BF16),
                                mod1, norm_g[1:2], da_w_in[0])

    lambda_init = 0.8 - 0.6 * math.exp(-0.3 * 1)
    lam_vecs = jnp.concatenate([da_lam_q1, da_lam_k1, da_lam_q2, da_lam_k2], axis=0)
    o = _da_attn(k, vt, qt, lam_vecs, da_subln_g[0], lambda_init)
    return _da_out(o, gate1, x1, gmod1[:, 1], final_g[None], da_w_out[0].astype(BF16))
```

```python
import functools
import math

import numpy as np
import jax
import jax.numpy as jnp
from jax import lax
from jax.experimental import pallas as pl
from jax.experimental.pallas import tpu as pltpu

F32 = jnp.float32
BF16 = jnp.bfloat16

EPS = 1e-6
GRID_W = 64
ROPE_THETA = 10000.0
HEAD_W = 128
DA_DQK = 64
TOK_TILE = 256
OUT_TILE = 1024
SCAN_CHUNK = 64
SCAN_HEADS = 16
Q_TILE = 256
KEY_CHUNK = 256
ATTN_HEADS = 2
BOUND_SLACK = 1.01
MIN_COLUMN_SUM = 2.0 ** -64
EXP2_CLAMP = 115.0
LOG2E = 1.4426950408889634
VMEM_LIMIT = 56 * 1024 * 1024

NT = (((1,), (1,)), ((), ()))
TN = (((0,), (0,)), ((), ()))


def _sigmoid(x):
    return 1.0 / (1.0 + jnp.exp(-x))


def _silu(x):
    return x * _sigmoid(x)


def _params(sem):
    return pltpu.CompilerParams(dimension_semantics=sem, vmem_limit_bytes=VMEM_LIMIT)


def _whole_vmem():
    return pl.BlockSpec(memory_space=pltpu.VMEM)


def _adaln_kernel(c_ref, w_ref, b_ref, o_ref):
    s = _silu(c_ref[...]).astype(BF16)
    o_ref[0] = jnp.dot(s, w_ref[0], preferred_element_type=F32) + b_ref[0]


def _adaln(cc, w_ada, b_ada):
    depth, d, d3 = w_ada.shape
    rows = cc.shape[0]
    return pl.pallas_call(
        _adaln_kernel,
        grid=(depth,),
        in_specs=[pl.BlockSpec((rows, d), lambda i: (0, 0)),
                  pl.BlockSpec((1, d, d3), lambda i: (i, 0, 0)),
                  pl.BlockSpec((1, 1, d3), lambda i: (i, 0, 0))],
        out_specs=pl.BlockSpec((1, rows, d3), lambda i: (i, 0, 0)),
        out_shape=jax.ShapeDtypeStruct((depth, rows, d3), F32),
        compiler_params=_params(("arbitrary",)),
        name="adaln",
    )(cc, w_ada.astype(BF16), b_ada.reshape(depth, 1, d3))


def _modulated_norm(x, g, shift, scale):
    y = x * lax.rsqrt(jnp.mean(x * x, axis=-1, keepdims=True) + EPS)
    return (y * g) * (1.0 + scale) + shift


def _hg_proj_kernel(ctx_ref, x_ref, mod_ref, ng_ref, w_ref, lbl_ref,
                    q_ref, gf_ref, gb_ref, v_ref, gate_ref):
    t = pl.program_id(1)
    x = jnp.where(t == 0, ctx_ref[0], x_ref[0])
    h = _modulated_norm(x, ng_ref[...], mod_ref[0, 0, 0:1, :], mod_ref[0, 0, 1:2, :]).astype(BF16)
    di = q_ref.shape[-1]

    def seg(j):
        return jnp.dot(h, w_ref[:, j * di:(j + 1) * di], preferred_element_type=F32)

    def log_forget(z, d):
        l0 = lbl_ref[0, d:d + 1, :]
        l1 = lbl_ref[1, d:d + 1, :]
        m = jnp.maximum(l0, l1)
        e0 = jnp.exp(l0 - m)
        lb = e0 / (e0 + jnp.exp(l1 - m))
        return jnp.log2(lb + (1.0 - lb) * _sigmoid(z))

    gf_ref[0] = log_forget(seg(1), 0).astype(BF16)
    gb_ref[0] = log_forget(seg(2), 1).astype(BF16)
    q_ref[0] = _silu(seg(0)).astype(BF16)
    gate_ref[0] = _silu(seg(4)).astype(BF16)
    v_ref[0] = seg(3).astype(BF16)


def _hg_proj(ctx, x, mod, norm_g, w_in, lb_logits):
    b, l, d = x.shape
    di = w_in.shape[1] // 5
    nt = l // TOK_TILE + 1
    tot = l + TOK_TILE
    tok = lambda dt: jax.ShapeDtypeStruct((b, tot, di), dt)
    out_blk = pl.BlockSpec((1, TOK_TILE, di), lambda i, t: (i, t, 0))
    return pl.pallas_call(
        _hg_proj_kernel,
        grid=(b, nt),
        in_specs=[pl.BlockSpec((1, TOK_TILE, d), lambda i, t: (i, 0, 0)),
                  pl.BlockSpec((1, TOK_TILE, d), lambda i, t: (i, jnp.maximum(t - 1, 0), 0)),
                  pl.BlockSpec((1, 1, 2, d), lambda i, t: (i, jnp.minimum(t, 1), 0, 0)),
                  pl.BlockSpec((1, d), lambda i, t: (0, 0)),
                  _whole_vmem(),
                  pl.BlockSpec((2, 2, di), lambda i, t: (0, 0, 0))],
        out_specs=[out_blk] * 5,
        out_shape=[tok(BF16)] * 5,
        compiler_params=_params(("arbitrary", "arbitrary")),
        name="hg_proj",
    )(ctx, x, mod, norm_g, w_in, lb_logits)


def _hg_scan_kernel(qf_ref, gf_ref, vf_ref, qb_ref, gb_ref, vb_ref, of_ref, ob_ref,
                    s_ref, bf_ref, bb_ref, tri_ref):
    t = pl.program_id(2)
    c = SCAN_CHUNK
    n_chunk = TOK_TILE // c
    w = SCAN_HEADS * HEAD_W

    @pl.when((pl.program_id(0) == 0) & (pl.program_id(1) == 0) & (t == 0))
    def _():
        row = lax.broadcasted_iota(jnp.int32, (TOK_TILE, TOK_TILE), 0)
        col = lax.broadcasted_iota(jnp.int32, (TOK_TILE, TOK_TILE), 1)
        same = (row // c) == (col // c)
        tri_ref[0] = jnp.where(same & (col <= row), 1.0, 0.0).astype(BF16)
        tri_ref[1] = jnp.where(same & (col >= row), 1.0, 0.0).astype(BF16)

    @pl.when(t == 0)
    def _():
        s_ref[...] = jnp.zeros_like(s_ref)

    def cumsum(tri, g):
        return jnp.dot(tri, g, preferred_element_type=F32)

    bf_ref[...] = cumsum(tri_ref[0], gf_ref[0])
    bb_ref[...] = cumsum(tri_ref[1], gb_ref[0])

    def chunk_decay_columns(b_ref, last_row):
        rows = [b_ref[j * c + last_row:j * c + last_row + 1, :] for j in range(n_chunk)]
        rows.append(jnp.zeros((8 - n_chunk, w), F32))
        return jnp.exp2(jnp.concatenate(rows, axis=0).T)

    dec_f = chunk_decay_columns(bf_ref, c - 1)
    dec_b = chunk_decay_columns(bb_ref, 0)

    ri = lax.broadcasted_iota(jnp.int32, (c, c), 0)
    ci = lax.broadcasted_iota(jnp.int32, (c, c), 1)
    keep_f = ci <= ri
    keep_b = ci >= ri

    def chunk_head(direction, j, hh, q_ref, g_ref, v_ref, b_ref, o_ref, dec, keep, mid_row, last_row):
        rows = slice(j * c, (j + 1) * c)
        lanes = slice(hh * HEAD_W, (hh + 1) * HEAD_W)
        q = q_ref[0, rows, lanes]
        g = g_ref[0, rows, lanes].astype(F32)
        v = v_ref[0, rows, lanes]
        bcum = b_ref[rows, lanes]
        r = bcum[mid_row:mid_row + 1, :]
        b_last = bcum[last_row:last_row + 1, :]
        d = jnp.clip(bcum - r, -EXP2_CLAMP, EXP2_CLAMP)
        q_mid = q * jnp.exp2(d).astype(BF16)
        k_mid = ((1.0 - jnp.exp2(g)) * jnp.exp2(-d)).astype(BF16)
        q_dec = q_mid * jnp.exp2(r).astype(BF16)
        k_dec = k_mid * jnp.exp2(b_last - r).astype(BF16)
        a = lax.dot_general(q_mid, k_mid, NT, preferred_element_type=F32)
        a = jnp.where(keep, a, 0.0).astype(BF16)
        s = s_ref[direction, hh]
        o_ref[0, rows, lanes] = jnp.dot(jnp.concatenate([q_dec, a], axis=1),
                                        jnp.concatenate([s.astype(BF16), v], axis=0),
                                        preferred_element_type=F32).astype(o_ref.dtype)
        s_ref[direction, hh] = s * dec[lanes, j:j + 1] + lax.dot_general(
            k_dec, v, TN, preferred_element_type=F32)

    for i in range(n_chunk):
        for hh in range(SCAN_HEADS):
            chunk_head(0, i, hh, qf_ref, gf_ref, vf_ref, bf_ref, of_ref, dec_f, keep_f, c // 2 - 1, c - 1)
            chunk_head(1, n_chunk - 1 - i, hh, qb_ref, gb_ref, vb_ref, bb_ref, ob_ref, dec_b, keep_b, c // 2, 0)


def _hg_scan(q, gf, gb, v):
    b, tot, di = q.shape
    nt = tot // TOK_TILE
    w = SCAN_HEADS * HEAD_W
    fwd = pl.BlockSpec((1, TOK_TILE, w), lambda i, h, t: (i, t, h))
    bwd = pl.BlockSpec((1, TOK_TILE, w), lambda i, h, t: (i, jnp.where(t == 0, 0, nt - t), h))
    out = jax.ShapeDtypeStruct((b, tot, di), BF16)
    return pl.pallas_call(
        _hg_scan_kernel,
        grid=(b, di // w, nt),
        in_specs=[fwd, fwd, fwd, bwd, bwd, bwd],
        out_specs=[fwd, bwd],
        out_shape=[out, out],
        scratch_shapes=[pltpu.VMEM((2, SCAN_HEADS, HEAD_W, HEAD_W), F32),
                        pltpu.VMEM((TOK_TILE, w), F32),
                        pltpu.VMEM((TOK_TILE, w), F32),
                        pltpu.VMEM((2, TOK_TILE, TOK_TILE), BF16)],
        compiler_params=_params(("arbitrary", "arbitrary", "arbitrary")),
        name="hg_scan",
    )(q, gf, v, q, gb, v)


def _rope_lane_tables(l_lat):
    ax = DA_DQK // 2
    inv = 1.0 / (ROPE_THETA ** (np.arange(0, ax, 2, dtype=np.float64) / ax))
    pos = np.arange(l_lat)
    ang_r = (pos // GRID_W)[:, None] * inv
    ang_c = (pos % GRID_W)[:, None] * inv
    zero = np.zeros_like(ang_r)
    cos64 = np.concatenate([np.cos(ang_r), np.cos(ang_r), np.cos(ang_c), np.cos(ang_c)], axis=1)
    up64 = np.concatenate([-np.sin(ang_r), zero, -np.sin(ang_c), zero], axis=1)
    dn64 = np.concatenate([zero, np.sin(ang_r), zero, np.sin(ang_c)], axis=1)
    lat = [np.concatenate([t, t], axis=1) for t in (cos64, up64, dn64)]
    ident = [np.ones((TOK_TILE, HEAD_W)), np.zeros((TOK_TILE, HEAD_W)), np.zeros((TOK_TILE, HEAD_W))]
    return [np.concatenate([i, t], axis=0).astype(np.float32) for i, t in zip(ident, lat)], (cos64, up64, dn64)


def _mid_kernel(of_ref, ob_ref, gate0_ref, ctx_ref, x_ref, gm_ref, hng_ref, wo_ref,
                mod_ref, ng_ref, wq_ref, wk_ref, wv_ref, wg_ref,
                kc_ref, ku_ref, kd_ref, qc_ref, qs_ref,
                x1_ref, k_ref, vt_ref, qt_ref, gate_ref, y_ref):
    t = pl.program_id(1)
    n_head = k_ref.shape[-1] // HEAD_W

    for hh in range(n_head):
        lanes = slice(hh * HEAD_W, (hh + 1) * HEAD_W)
        o = of_ref[0, :, lanes].astype(F32) + ob_ref[0, :, lanes].astype(F32)
        o = o * lax.rsqrt(jnp.mean(o * o, axis=-1, keepdims=True) + EPS) * hng_ref[:, lanes]
        y_ref[:, lanes] = (o * gate0_ref[0, :, lanes].astype(F32)).astype(BF16)
    y = jnp.dot(y_ref[...], wo_ref[...], preferred_element_type=F32)
    x1 = jnp.where(t == 0, ctx_ref[0], x_ref[0]) + gm_ref[0, 0] * y

    h = _modulated_norm(x1, ng_ref[...], mod_ref[0, 0, 0:1, :], mod_ref[0, 0, 1:2, :]).astype(BF16)
    k = jnp.dot(h, wk_ref[...], preferred_element_type=F32)
    cos, s_up, s_dn = kc_ref[...], ku_ref[...], kd_ref[...]
    for hh in range(n_head):
        lanes = slice(hh * HEAD_W, (hh + 1) * HEAD_W)
        kh = k[:, lanes]
        rot = (kh * cos + pltpu.roll(kh, HEAD_W - 16, axis=1) * s_up
               + pltpu.roll(kh, 16, axis=1) * s_dn)
        k_ref[0, :, lanes] = rot.astype(BF16)

    vt_ref[0] = lax.dot_general(wv_ref[...], h, NT, preferred_element_type=F32).astype(BF16)

    @pl.when(t > 0)
    def _():
        x1_ref[0] = x1
        qt = lax.dot_general(wq_ref[...], h, NT, preferred_element_type=F32)
        qc, qs = qc_ref[...], qs_ref[...]
        for grp in range(qt.shape[0] // DA_DQK):
            x = qt[grp * DA_DQK:(grp + 1) * DA_DQK]
            partner = jnp.concatenate([x[16:32], x[0:16], x[48:64], x[32:48]], axis=0)
            qt_ref[0, grp * DA_DQK:(grp + 1) * DA_DQK, :] = (x * qc + partner * qs).astype(BF16)
        gate_ref[0] = _silu(jnp.dot(h, wg_ref[...], preferred_element_type=F32)).astype(BF16)


def _mid(of, ob, gate0, ctx, x, gate_mod, hg_norm_g, w_out, mod, norm_g, w_in):
    b, tot, di = of.shape
    l_lat, d = x.shape[1], x.shape[2]
    nt = tot // TOK_TILE
    wq, wk, wv, wg = (w_in[:, j * di:(j + 1) * di] for j in range(4))
    ktabs, (cos64, up64, dn64) = _rope_lane_tables(l_lat)
    q_scale = DA_DQK ** -0.5 * LOG2E
    qc = (cos64.T * q_scale).astype(np.float32)
    qs = ((up64 + dn64).T * q_scale).astype(np.float32)
    lat_t = lambda t: jnp.maximum(t - 1, 0)
    tokw = pl.BlockSpec((1, TOK_TILE, di), lambda i, t: (i, t, 0))
    ktab = pl.BlockSpec((TOK_TILE, HEAD_W), lambda i, t: (t, 0))
    qtab = pl.BlockSpec((DA_DQK, TOK_TILE), lambda i, t: (0, lat_t(t)))
    return pl.pallas_call(
        _mid_kernel,
        grid=(b, nt),
        in_specs=[tokw, tokw, tokw,
                  pl.BlockSpec((1, TOK_TILE, d), lambda i, t: (i, 0, 0)),
                  pl.BlockSpec((1, TOK_TILE, d), lambda i, t: (i, lat_t(t), 0)),
                  pl.BlockSpec((1, 1, 1, d), lambda i, t: (i, jnp.minimum(t, 1), 0, 0)),
                  pl.BlockSpec((1, di), lambda i, t: (0, 0)),
                  _whole_vmem(),
                  pl.BlockSpec((1, 1, 2, d), lambda i, t: (i, jnp.minimum(t, 1), 0, 0)),
                  pl.BlockSpec((1, d), lambda i, t: (0, 0)),
                  _whole_vmem(), _whole_vmem(), _whole_vmem(), _whole_vmem(),
                  ktab, ktab, ktab, qtab, qtab],
        out_specs=[pl.BlockSpec((1, TOK_TILE, d), lambda i, t: (i, lat_t(t), 0)),
                   pl.BlockSpec((1, TOK_TILE, di), lambda i, t: (i, t, 0)),
                   pl.BlockSpec((1, di, TOK_TILE), lambda i, t: (i, 0, t)),
                   pl.BlockSpec((1, di, TOK_TILE), lambda i, t: (i, 0, lat_t(t))),
                   pl.BlockSpec((1, TOK_TILE, di), lambda i, t: (i, lat_t(t), 0))],
        out_shape=[jax.ShapeDtypeStruct((b, l_lat, d), F32),
                   jax.ShapeDtypeStruct((b, tot, di), BF16),
                   jax.ShapeDtypeStruct((b, di, tot), BF16),
                   jax.ShapeDtypeStruct((b, di, l_lat), BF16),
                   jax.ShapeDtypeStruct((b, l_lat, di), BF16)],
        scratch_shapes=[pltpu.VMEM((TOK_TILE, di), BF16)],
        compiler_params=_params(("arbitrary", "arbitrary")),
        name="hg_out_da_proj",
    )(of, ob, gate0, ctx, x, gate_mod, hg_norm_g, w_out, mod, norm_g,
      wq.T.astype(BF16), wk.astype(BF16), wv.T.astype(BF16), wg.astype(BF16),
      *[jnp.asarray(a) for a in ktabs], jnp.asarray(qc), jnp.asarray(qs))


def _da_attn_kernel(k_ref, vt_ref, qt_ref, lam_ref, sg_ref, o_ref, pa_ref, pb_ref, a_ref, la_ref, lb_ref,
                    ota_ref, otb_ref, *, lambda_init):
    lq1, lk1, lq2, lk2 = (lam_ref[i:i + 1, :] for i in range(4))
    lam = (jnp.exp(jnp.sum(lq1 * lk1, axis=-1, keepdims=True))
           - jnp.exp(jnp.sum(lq2 * lk2, axis=-1, keepdims=True)) + lambda_init)
    out_gain = sg_ref[...] * (1.0 - lambda_init)
    n_chunk = k_ref.shape[1] // KEY_CHUNK
    n_q = qt_ref.shape[-1] // Q_TILE
    w2 = 2 * Q_TILE
    half0 = lax.broadcasted_iota(jnp.int32, (HEAD_W, Q_TILE), 0) < DA_DQK

    def head(hh):
        return slice(hh * HEAD_W, (hh + 1) * HEAD_W)

    def key_norm_bound(hh):
        kf = k_ref[0, :, head(hh)].astype(F32)
        sel_r = lax.broadcasted_iota(jnp.int32, (HEAD_W, HEAD_W), 0) // DA_DQK
        sel_c = lax.broadcasted_iota(jnp.int32, (HEAD_W, HEAD_W), 1)
        sel = jnp.where(sel_r == sel_c, 1.0, 0.0).astype(BF16)
        norm2 = jnp.dot((kf * kf).astype(BF16), sel, preferred_element_type=F32)
        kmax = jnp.sqrt(jnp.max(norm2, axis=0, keepdims=True)) * BOUND_SLACK
        return jnp.concatenate([jnp.broadcast_to(kmax[:, 0:1], (1, Q_TILE)),
                                jnp.broadcast_to(kmax[:, 1:2], (1, Q_TILE))], axis=1)

    kmax = [key_norm_bound(hh) for hh in range(ATTN_HEADS)]

    def fold(x):
        return x.reshape(KEY_CHUNK // 8, 8, w2)

    def exp_stage(hh, i, p_ref, l_ref, exact):
        start = i * Q_TILE if isinstance(i, int) else pl.multiple_of(i * Q_TILE, Q_TILE)
        qt = qt_ref[0, head(hh), pl.ds(start, Q_TILE)]
        zero = jnp.zeros_like(qt)
        rhs = jnp.concatenate([jnp.where(half0, qt, zero), jnp.where(half0, zero, qt)], axis=1)

        def scores(c):
            return jnp.dot(k_ref[0, c * KEY_CHUNK:(c + 1) * KEY_CHUNK, head(hh)], rhs,
                           preferred_element_type=F32)

        if exact:
            mx = jnp.full((8, w2), -jnp.inf, F32)
            for c in range(n_chunk):
                mx = jnp.maximum(mx, jnp.max(fold(scores(c)), axis=0))
            m = jnp.max(mx, axis=0, keepdims=True)
        else:
            rf = rhs.astype(F32)
            m = jnp.sqrt(jnp.sum(rf * rf, axis=0, keepdims=True)) * kmax[hh]
        l8 = jnp.zeros((8, w2), F32)
        for c in range(n_chunk):
            p = jnp.exp2(scores(c) - m)
            l8 = l8 + jnp.sum(fold(p), axis=0)
            p_ref[c * KEY_CHUNK:(c + 1) * KEY_CHUNK, :] = p.astype(BF16)
        l_ref[...] = l8
        return jnp.where(l8 > MIN_COLUMN_SUM, 0.0, 1.0)

    def value_stage(hh, p_ref, l_ref, ot_ref):
        l = jnp.sum(l_ref[...], axis=0, keepdims=True)
        l0, l1 = l[:, :Q_TILE], l[:, Q_TILE:]
        rho = (lam * l0 / l1).astype(BF16)
        for c in range(n_chunk):
            rows = slice(c * KEY_CHUNK, (c + 1) * KEY_CHUNK)
            a_ref[rows, :] = p_ref[rows, :Q_TILE] - p_ref[rows, Q_TILE:] * rho
        ot_ref[...] = jnp.dot(vt_ref[0, head(hh), :], a_ref[...], preferred_element_type=F32) * (1.0 / l0)

    def store_stage(hh, i, ot_ref):
        start = i * Q_TILE if isinstance(i, int) else pl.multiple_of(i * Q_TILE, Q_TILE)
        ot = ot_ref[...]
        ot = ot * lax.rsqrt(jnp.mean(ot * ot, axis=0, keepdims=True) + EPS) * out_gain
        o_ref[0, pl.ds(start, Q_TILE), head(hh)] = ot.T.astype(o_ref.dtype)

    even = (pa_ref, la_ref, ota_ref)
    odd = (pb_ref, lb_ref, otb_ref)
    last = n_q - 1
    bad = jnp.zeros((8, w2), F32)
    for hh in range(ATTN_HEADS):
        prev = hh - 1
        bad = jnp.maximum(bad, exp_stage(hh, 0, *even[:2], exact=False))
        if hh > 0:
            value_stage(prev, *odd)
            store_stage(prev, last - 1, even[2])
        bad = jnp.maximum(bad, exp_stage(hh, 1, *odd[:2], exact=False))
        value_stage(hh, *even)
        if hh > 0:
            store_stage(prev, last, odd[2])

        def body(j, bad, hh=hh):
            s = 2 * j + 2
            bad = jnp.maximum(bad, exp_stage(hh, s, *even[:2], exact=False))
            value_stage(hh, *odd)
            store_stage(hh, s - 2, even[2])
            bad = jnp.maximum(bad, exp_stage(hh, s + 1, *odd[:2], exact=False))
            value_stage(hh, *even)
            store_stage(hh, s - 1, odd[2])
            return bad

        bad = lax.fori_loop(0, n_q // 2 - 1, body, bad)
    value_stage(ATTN_HEADS - 1, *odd)
    store_stage(ATTN_HEADS - 1, last - 1, even[2])
    store_stage(ATTN_HEADS - 1, last, odd[2])

    @pl.when(jnp.max(bad) > 0.0)
    def _():
        for hh in range(ATTN_HEADS):
            def redo(i, carry, hh=hh):
                exp_stage(hh, i, *even[:2], exact=True)
                value_stage(hh, *even)
                store_stage(hh, i, even[2])
                return carry

            lax.fori_loop(0, n_q, redo, 0)


def _da_attn(k, vt, qt, lam_vecs, subln_g, lambda_init):
    b, tot, di = k.shape
    l_lat = qt.shape[-1]
    w = ATTN_HEADS * HEAD_W
    assert (l_lat // Q_TILE) % 2 == 0 and di % w == 0
    return pl.pallas_call(
        functools.partial(_da_attn_kernel, lambda_init=lambda_init),
        grid=(b, di // w),
        in_specs=[pl.BlockSpec((1, tot, w), lambda i, h: (i, 0, h)),
                  pl.BlockSpec((1, w, tot), lambda i, h: (i, h, 0)),
                  pl.BlockSpec((1, w, l_lat), lambda i, h: (i, h, 0)),
                  pl.BlockSpec((4, DA_DQK), lambda i, h: (0, 0)),
                  pl.BlockSpec((HEAD_W, 1), lambda i, h: (0, 0))],
        out_specs=pl.BlockSpec((1, l_lat, w), lambda i, h: (i, 0, h)),
        out_shape=jax.ShapeDtypeStruct((b, l_lat, di), BF16),
        scratch_shapes=[pltpu.VMEM((tot, 2 * Q_TILE), BF16),
                        pltpu.VMEM((tot, 2 * Q_TILE), BF16),
                        pltpu.VMEM((tot, Q_TILE), BF16),
                        pltpu.VMEM((8, 2 * Q_TILE), F32),
                        pltpu.VMEM((8, 2 * Q_TILE), F32),
                        pltpu.VMEM((HEAD_W, Q_TILE), F32),
                        pltpu.VMEM((HEAD_W, Q_TILE), F32)],
        compiler_params=_params(("arbitrary", "arbitrary")),
        name="da_attn",
    )(k, vt, qt, lam_vecs, subln_g.reshape(HEAD_W, 1))


def _da_out_kernel(o_ref, gate_ref, x_ref, gm_ref, fg_ref, w_ref, out_ref):
    y = o_ref[0] * gate_ref[0]
    x = x_ref[0] + gm_ref[0] * jnp.dot(y, w_ref[...], preferred_element_type=F32)
    out_ref[0] = x * lax.rsqrt(jnp.mean(x * x, axis=-1, keepdims=True) + EPS) * fg_ref[...]


def _da_out(o, gate, x1, gate_mod, final_g, w_out):
    b, l_lat, di = o.shape
    d = x1.shape[-1]
    tile = math.gcd(l_lat, OUT_TILE)
    tokw = pl.BlockSpec((1, tile, di), lambda i, t: (i, t, 0))
    return pl.pallas_call(
        _da_out_kernel,
        grid=(b, l_lat // tile),
        in_specs=[tokw, tokw,
                  pl.BlockSpec((1, tile, d), lambda i, t: (i, t, 0)),
                  pl.BlockSpec((1, 1, d), lambda i, t: (i, 0, 0)),
                  pl.BlockSpec((1, d), lambda i, t: (0, 0)),
                  _whole_vmem()],
        out_specs=pl.BlockSpec((1, tile, d), lambda i, t: (i, t, 0)),
        out_shape=jax.ShapeDtypeStruct((b, l_lat, d), F32),
        compiler_params=_params(("arbitrary", "arbitrary")),
        name="da_out",
    )(o, gate, x1, gate_mod, final_g, w_out)


def kernel(x, c, ctx, c_ctx, w_ada, b_ada, norm_g, hg_w_in, hg_lb_logits, hg_norm_g, hg_w_out,
           da_w_in, da_lam_q1, da_lam_k1, da_lam_q2, da_lam_k2, da_subln_g, da_w_out, final_g):
    b, l_lat, d = x.shape
    assert ctx.shape[1] == TOK_TILE and l_lat % TOK_TILE == 0 and l_lat % GRID_W == 0
    assert w_ada.shape[0] == 2 and hg_w_in.shape[0] == 1 and da_w_in.shape[0] == 1
    assert hg_lb_logits.shape[0] == 2

    rows = -(-(b + 1) // 8) * 8
    cc = jnp.concatenate([c, c_ctx[None], jnp.zeros((rows - b - 1, d), F32)], axis=0)
    ada = _adaln(cc, w_ada, b_ada)

    def mods(layer):
        m = ada[layer].reshape(rows, 3, d)
        pair = jnp.stack([jnp.broadcast_to(m[b], (b, 3, d)), m[:b]], axis=1)
        return pair[:, :, 0:2], pair[:, :, 2:3]

    mod0, gmod0 = mods(0)
    mod1, gmod1 = mods(1)

    q, gf, gb, v, gate0 = _hg_proj(ctx, x, mod0, norm_g[0:1], hg_w_in[0].astype(BF16), hg_lb_logits)
    of, ob = _hg_scan(q, gf, gb, v)
    x1, k, vt, qt, gate1 = _mid(of, ob, gate0, ctx, x, gmod0, hg_norm_g, hg_w_out[0].astype(BF16),
                                mod1, norm_g[1:2], da_w_in[0])

    lambda_init = 0.8 - 0.6 * math.exp(-0.3 * 1)
    lam_vecs = jnp.concatenate([da_lam_q1, da_lam_k1, da_lam_q2, da_lam_k2], axis=0)
    o = _da_attn(k, vt, qt, lam_vecs, da_subln_g[0], lambda_init)
    return _da_out(o, gate1, x1, gmod1[:, 1], final_g[None], da_w_out[0].astype(BF16))
```

```python
import functools
import math

import numpy as np
import jax
import jax.numpy as jnp
from jax import lax
from jax.experimental import pallas as pl
from jax.experimental.pallas import tpu as pltpu

F32 = jnp.float32
BF16 = jnp.bfloat16

EPS = 1e-6
GRID_W = 64
ROPE_THETA = 10000.0
HEAD_W = 128
DA_DQK = 64
TOK_TILE = 256
OUT_TILE = 1024
SCAN_CHUNK = 64
SCAN_HEADS = 16
Q_TILE = 256
KEY_CHUNK = 256
ATTN_HEADS = 4
BOUND_SLACK = 1.01
MIN_COLUMN_SUM = 2.0 ** -64
EXP2_CLAMP = 115.0
LOG2E = 1.4426950408889634
VMEM_LIMIT = 56 * 1024 * 1024

NT = (((1,), (1,)), ((), ()))
TN = (((0,), (0,)), ((), ()))


def _sigmoid(x):
    return 1.0 / (1.0 + jnp.exp(-x))


def _silu(x):
    return x * _sigmoid(x)


def _params(sem):
    return pltpu.CompilerParams(dimension_semantics=sem, vmem_limit_bytes=VMEM_LIMIT)


def _whole_vmem():
    return pl.BlockSpec(memory_space=pltpu.VMEM)


def _adaln_kernel(c_ref, w_ref, b_ref, o_ref):
    s = _silu(c_ref[...]).astype(BF16)
    o_ref[0] = jnp.dot(s, w_ref[0], preferred_element_type=F32) + b_ref[0]


def _adaln(cc, w_ada, b_ada):
    depth, d, d3 = w_ada.shape
    rows = cc.shape[0]
    return pl.pallas_call(
        _adaln_kernel,
        grid=(depth,),
        in_specs=[pl.BlockSpec((rows, d), lambda i: (0, 0)),
                  pl.BlockSpec((1, d, d3), lambda i: (i, 0, 0)),
                  pl.BlockSpec((1, 1, d3), lambda i: (i, 0, 0))],
        out_specs=pl.BlockSpec((1, rows, d3), lambda i: (i, 0, 0)),
        out_shape=jax.ShapeDtypeStruct((depth, rows, d3), F32),
        compiler_params=_params(("arbitrary",)),
        name="adaln",
    )(cc, w_ada.astype(BF16), b_ada.reshape(depth, 1, d3))


def _modulated_norm(x, g, shift, scale):
    y = x * lax.rsqrt(jnp.mean(x * x, axis=-1, keepdims=True) + EPS)
    return (y * g) * (1.0 + scale) + shift


def _hg_proj_kernel(ctx_ref, x_ref, mod_ref, ng_ref, w_ref, lbl_ref,
                    q_ref, gf_ref, gb_ref, v_ref, gate_ref):
    t = pl.program_id(1)
    x = jnp.where(t == 0, ctx_ref[0], x_ref[0])
    h = _modulated_norm(x, ng_ref[...], mod_ref[0, 0, 0:1, :], mod_ref[0, 0, 1:2, :]).astype(BF16)
    di = q_ref.shape[-1]

    def seg(j):
        return jnp.dot(h, w_ref[:, j * di:(j + 1) * di], preferred_element_type=F32)

    def log_forget(z, d):
        l0 = lbl_ref[0, d:d + 1, :]
        l1 = lbl_ref[1, d:d + 1, :]
        m = jnp.maximum(l0, l1)
        e0 = jnp.exp(l0 - m)
        lb = e0 / (e0 + jnp.exp(l1 - m))
        return jnp.log2(lb + (1.0 - lb) * _sigmoid(z))

    gf_ref[0] = log_forget(seg(1), 0).astype(BF16)
    gb_ref[0] = log_forget(seg(2), 1).astype(BF16)
    q_ref[0] = _silu(seg(0)).astype(BF16)
    gate_ref[0] = _silu(seg(4)).astype(BF16)
    v_ref[0] = seg(3).astype(BF16)


def _hg_proj(ctx, x, mod, norm_g, w_in, lb_logits):
    b, l, d = x.shape
    di = w_in.shape[1] // 5
    nt = l // TOK_TILE + 1
    tot = l + TOK_TILE
    tok = lambda dt: jax.ShapeDtypeStruct((b, tot, di), dt)
    out_blk = pl.BlockSpec((1, TOK_TILE, di), lambda i, t: (i, t, 0))
    return pl.pallas_call(
        _hg_proj_kernel,
        grid=(b, nt),
        in_specs=[pl.BlockSpec((1, TOK_TILE, d), lambda i, t: (i, 0, 0)),
                  pl.BlockSpec((1, TOK_TILE, d), lambda i, t: (i, jnp.maximum(t - 1, 0), 0)),
                  pl.BlockSpec((1, 1, 2, d), lambda i, t: (i, jnp.minimum(t, 1), 0, 0)),
                  pl.BlockSpec((1, d), lambda i, t: (0, 0)),
                  _whole_vmem(),
                  pl.BlockSpec((2, 2, di), lambda i, t: (0, 0, 0))],
        out_specs=[out_blk] * 5,
        out_shape=[tok(BF16)] * 5,
        compiler_params=_params(("arbitrary", "arbitrary")),
        name="hg_proj",
    )(ctx, x, mod, norm_g, w_in, lb_logits)


def _hg_scan_kernel(qf_ref, gf_ref, vf_ref, qb_ref, gb_ref, vb_ref, of_ref, ob_ref,
                    s_ref, bf_ref, bb_ref, tri_ref):
    t = pl.program_id(2)
    c = SCAN_CHUNK
    n_chunk = TOK_TILE // c
    w = SCAN_HEADS * HEAD_W

    @pl.when((pl.program_id(0) == 0) & (pl.program_id(1) == 0) & (t == 0))
    def _():
        row = lax.broadcasted_iota(jnp.int32, (TOK_TILE, TOK_TILE), 0)
        col = lax.broadcasted_iota(jnp.int32, (TOK_TILE, TOK_TILE), 1)
        same = (row // c) == (col // c)
        tri_ref[0] = jnp.where(same & (col <= row), 1.0, 0.0).astype(BF16)
        tri_ref[1] = jnp.where(same & (col >= row), 1.0, 0.0).astype(BF16)

    @pl.when(t == 0)
    def _():
        s_ref[...] = jnp.zeros_like(s_ref)

    def cumsum(tri, g):
        return jnp.dot(tri, g, preferred_element_type=F32)

    bf_ref[...] = cumsum(tri_ref[0], gf_ref[0])
    bb_ref[...] = cumsum(tri_ref[1], gb_ref[0])

    def chunk_decay_columns(b_ref, last_row):
        rows = [b_ref[j * c + last_row:j * c + last_row + 1, :] for j in range(n_chunk)]
        rows.append(jnp.zeros((8 - n_chunk, w), F32))
        return jnp.exp2(jnp.concatenate(rows, axis=0).T)

    dec_f = chunk_decay_columns(bf_ref, c - 1)
    dec_b = chunk_decay_columns(bb_ref, 0)

    ri = lax.broadcasted_iota(jnp.int32, (c, c), 0)
    ci = lax.broadcasted_iota(jnp.int32, (c, c), 1)
    keep_f = ci <= ri
    keep_b = ci >= ri

    def chunk_head(direction, j, hh, q_ref, g_ref, v_ref, b_ref, o_ref, dec, keep, mid_row, last_row):
        rows = slice(j * c, (j + 1) * c)
        lanes = slice(hh * HEAD_W, (hh + 1) * HEAD_W)
        q = q_ref[0, rows, lanes]
        g = g_ref[0, rows, lanes].astype(F32)
        v = v_ref[0, rows, lanes]
        bcum = b_ref[rows, lanes]
        r = bcum[mid_row:mid_row + 1, :]
        b_last = bcum[last_row:last_row + 1, :]
        d = jnp.clip(bcum - r, -EXP2_CLAMP, EXP2_CLAMP)
        q_mid = q * jnp.exp2(d).astype(BF16)
        k_mid = ((1.0 - jnp.exp2(g)) * jnp.exp2(-d)).astype(BF16)
        q_dec = q_mid * jnp.exp2(r).astype(BF16)
        k_dec = k_mid * jnp.exp2(b_last - r).astype(BF16)
        a = lax.dot_general(q_mid, k_mid, NT, preferred_element_type=F32)
        a = jnp.where(keep, a, 0.0).astype(BF16)
        s = s_ref[direction, hh]
        o_ref[0, rows, lanes] = jnp.dot(jnp.concatenate([q_dec, a], axis=1),
                                        jnp.concatenate([s.astype(BF16), v], axis=0),
                                        preferred_element_type=F32).astype(o_ref.dtype)
        s_ref[direction, hh] = s * dec[lanes, j:j + 1] + lax.dot_general(
            k_dec, v, TN, preferred_element_type=F32)

    for i in range(n_chunk):
        for hh in range(SCAN_HEADS):
            chunk_head(0, i, hh, qf_ref, gf_ref, vf_ref, bf_ref, of_ref, dec_f, keep_f, c // 2 - 1, c - 1)
            chunk_head(1, n_chunk - 1 - i, hh, qb_ref, gb_ref, vb_ref, bb_ref, ob_ref, dec_b, keep_b, c // 2, 0)


def _hg_scan(q, gf, gb, v):
    b, tot, di = q.shape
    nt = tot // TOK_TILE
    w = SCAN_HEADS * HEAD_W
    fwd = pl.BlockSpec((1, TOK_TILE, w), lambda i, h, t: (i, t, h))
    bwd = pl.BlockSpec((1, TOK_TILE, w), lambda i, h, t: (i, jnp.where(t == 0, 0, nt - t), h))
    out = jax.ShapeDtypeStruct((b, tot, di), BF16)
    return pl.pallas_call(
        _hg_scan_kernel,
        grid=(b, di // w, nt),
        in_specs=[fwd, fwd, fwd, bwd, bwd, bwd],
        out_specs=[fwd, bwd],
        out_shape=[out, out],
        scratch_shapes=[pltpu.VMEM((2, SCAN_HEADS, HEAD_W, HEAD_W), F32),
                        pltpu.VMEM((TOK_TILE, w), F32),
                        pltpu.VMEM((TOK_TILE, w), F32),
                        pltpu.VMEM((2, TOK_TILE, TOK_TILE), BF16)],
        compiler_params=_params(("arbitrary", "arbitrary", "arbitrary")),
        name="hg_scan",
    )(q, gf, v, q, gb, v)


def _rope_lane_tables(l_lat):
    ax = DA_DQK // 2
    inv = 1.0 / (ROPE_THETA ** (np.arange(0, ax, 2, dtype=np.float64) / ax))
    pos = np.arange(l_lat)
    ang_r = (pos // GRID_W)[:, None] * inv
    ang_c = (pos % GRID_W)[:, None] * inv
    zero = np.zeros_like(ang_r)
    cos64 = np.concatenate([np.cos(ang_r), np.cos(ang_r), np.cos(ang_c), np.cos(ang_c)], axis=1)
    up64 = np.concatenate([-np.sin(ang_r), zero, -np.sin(ang_c), zero], axis=1)
    dn64 = np.concatenate([zero, np.sin(ang_r), zero, np.sin(ang_c)], axis=1)
    lat = [np.concatenate([t, t], axis=1) for t in (cos64, up64, dn64)]
    ident = [np.ones((TOK_TILE, HEAD_W)), np.zeros((TOK_TILE, HEAD_W)), np.zeros((TOK_TILE, HEAD_W))]
    return [np.concatenate([i, t], axis=0).astype(np.float32) for i, t in zip(ident, lat)], (cos64, up64, dn64)


def _mid_kernel(of_ref, ob_ref, gate0_ref, ctx_ref, x_ref, gm_ref, hng_ref, wo_ref,
                mod_ref, ng_ref, wq_ref, wk_ref, wv_ref, wg_ref,
                kc_ref, ku_ref, kd_ref, qc_ref, qs_ref,
                x1_ref, k_ref, vt_ref, qt_ref, gate_ref, y_ref):
    t = pl.program_id(1)
    n_head = k_ref.shape[-1] // HEAD_W

    for hh in range(n_head):
        lanes = slice(hh * HEAD_W, (hh + 1) * HEAD_W)
        o = of_ref[0, :, lanes].astype(F32) + ob_ref[0, :, lanes].astype(F32)
        o = o * lax.rsqrt(jnp.mean(o * o, axis=-1, keepdims=True) + EPS) * hng_ref[:, lanes]
        y_ref[:, lanes] = (o * gate0_ref[0, :, lanes].astype(F32)).astype(BF16)
    y = jnp.dot(y_ref[...], wo_ref[...], preferred_element_type=F32)
    x1 = jnp.where(t == 0, ctx_ref[0], x_ref[0]) + gm_ref[0, 0] * y

    h = _modulated_norm(x1, ng_ref[...], mod_ref[0, 0, 0:1, :], mod_ref[0, 0, 1:2, :]).astype(BF16)
    k = jnp.dot(h, wk_ref[...], preferred_element_type=F32)
    cos, s_up, s_dn = kc_ref[...], ku_ref[...], kd_ref[...]
    for hh in range(n_head):
        lanes = slice(hh * HEAD_W, (hh + 1) * HEAD_W)
        kh = k[:, lanes]
        rot = (kh * cos + pltpu.roll(kh, HEAD_W - 16, axis=1) * s_up
               + pltpu.roll(kh, 16, axis=1) * s_dn)
        k_ref[0, :, lanes] = rot.astype(BF16)

    vt_ref[0] = lax.dot_general(wv_ref[...], h, NT, preferred_element_type=F32).astype(BF16)

    @pl.when(t > 0)
    def _():
        x1_ref[0] = x1
        gate_ref[0] = _silu(jnp.dot(h, wg_ref[...], preferred_element_type=F32)).astype(BF16)
        qt = lax.dot_general(wq_ref[...], h, NT, preferred_element_type=F32)
        qc, qs = qc_ref[...], qs_ref[...]
        for grp in range(qt.shape[0] // DA_DQK):
            x = qt[grp * DA_DQK:(grp + 1) * DA_DQK]
            partner = jnp.concatenate([x[16:32], x[0:16], x[48:64], x[32:48]], axis=0)
            qt_ref[0, grp * DA_DQK:(grp + 1) * DA_DQK, :] = (x * qc + partner * qs).astype(BF16)


def _mid(of, ob, gate0, ctx, x, gate_mod, hg_norm_g, w_out, mod, norm_g, w_in):
    b, tot, di = of.shape
    l_lat, d = x.shape[1], x.shape[2]
    nt = tot // TOK_TILE
    wq, wk, wv, wg = (w_in[:, j * di:(j + 1) * di] for j in range(4))
    ktabs, (cos64, up64, dn64) = _rope_lane_tables(l_lat)
    q_scale = DA_DQK ** -0.5 * LOG2E
    qc = (cos64.T * q_scale).astype(np.float32)
    qs = ((up64 + dn64).T * q_scale).astype(np.float32)
    lat_t = lambda t: jnp.maximum(t - 1, 0)
    tokw = pl.BlockSpec((1, TOK_TILE, di), lambda i, t: (i, t, 0))
    ktab = pl.BlockSpec((TOK_TILE, HEAD_W), lambda i, t: (t, 0))
    qtab = pl.BlockSpec((DA_DQK, TOK_TILE), lambda i, t: (0, lat_t(t)))
    return pl.pallas_call(
        _mid_kernel,
        grid=(b, nt),
        in_specs=[tokw, tokw, tokw,
                  pl.BlockSpec((1, TOK_TILE, d), lambda i, t: (i, 0, 0)),
                  pl.BlockSpec((1, TOK_TILE, d), lambda i, t: (i, lat_t(t), 0)),
                  pl.BlockSpec((1, 1, 1, d), lambda i, t: (i, jnp.minimum(t, 1), 0, 0)),
                  pl.BlockSpec((1, di), lambda i, t: (0, 0)),
                  _whole_vmem(),
                  pl.BlockSpec((1, 1, 2, d), lambda i, t: (i, jnp.minimum(t, 1), 0, 0)),
                  pl.BlockSpec((1, d), lambda i, t: (0, 0)),
                  _whole_vmem(), _whole_vmem(), _whole_vmem(), _whole_vmem(),
                  ktab, ktab, ktab, qtab, qtab],
        out_specs=[pl.BlockSpec((1, TOK_TILE, d), lambda i, t: (i, lat_t(t), 0)),
                   pl.BlockSpec((1, TOK_TILE, di), lambda i, t: (i, t, 0)),
                   pl.BlockSpec((1, di, TOK_TILE), lambda i, t: (i, 0, t)),
                   pl.BlockSpec((1, di, TOK_TILE), lambda i, t: (i, 0, lat_t(t))),
                   pl.BlockSpec((1, TOK_TILE, di), lambda i, t: (i, lat_t(t), 0))],
        out_shape=[jax.ShapeDtypeStruct((b, l_lat, d), F32),
                   jax.ShapeDtypeStruct((b, tot, di), BF16),
                   jax.ShapeDtypeStruct((b, di, tot), BF16),
                   jax.ShapeDtypeStruct((b, di, l_lat), BF16),
                   jax.ShapeDtypeStruct((b, l_lat, di), BF16)],
        scratch_shapes=[pltpu.VMEM((TOK_TILE, di), BF16)],
        compiler_params=_params(("arbitrary", "arbitrary")),
        name="hg_out_da_proj",
    )(of, ob, gate0, ctx, x, gate_mod, hg_norm_g, w_out, mod, norm_g,
      wq.T.astype(BF16), wk.astype(BF16), wv.T.astype(BF16), wg.astype(BF16),
      *[jnp.asarray(a) for a in ktabs], jnp.asarray(qc), jnp.asarray(qs))


def _da_attn_kernel(k_ref, vt_ref, qt_ref, lam_ref, sg_ref, o_ref, pa_ref, pb_ref, a_ref, la_ref, lb_ref,
                    ota_ref, otb_ref, *, lambda_init):
    lq1, lk1, lq2, lk2 = (lam_ref[i:i + 1, :] for i in range(4))
    lam = (jnp.exp(jnp.sum(lq1 * lk1, axis=-1, keepdims=True))
           - jnp.exp(jnp.sum(lq2 * lk2, axis=-1, keepdims=True)) + lambda_init)
    out_gain = sg_ref[...] * (1.0 - lambda_init)
    n_chunk = k_ref.shape[1] // KEY_CHUNK
    n_q = qt_ref.shape[-1] // Q_TILE
    w2 = 2 * Q_TILE
    half0 = lax.broadcasted_iota(jnp.int32, (HEAD_W, Q_TILE), 0) < DA_DQK

    def head(hh):
        return slice(hh * HEAD_W, (hh + 1) * HEAD_W)

    def key_norm_bound(hh):
        kf = k_ref[0, :, head(hh)].astype(F32)
        sel_r = lax.broadcasted_iota(jnp.int32, (HEAD_W, HEAD_W), 0) // DA_DQK
        sel_c = lax.broadcasted_iota(jnp.int32, (HEAD_W, HEAD_W), 1)
        sel = jnp.where(sel_r == sel_c, 1.0, 0.0).astype(BF16)
        norm2 = jnp.dot((kf * kf).astype(BF16), sel, preferred_element_type=F32)
        kmax = jnp.sqrt(jnp.max(norm2, axis=0, keepdims=True)) * BOUND_SLACK
        return jnp.concatenate([jnp.broadcast_to(kmax[:, 0:1], (1, Q_TILE)),
                                jnp.broadcast_to(kmax[:, 1:2], (1, Q_TILE))], axis=1)

    kmax = [key_norm_bound(hh) for hh in range(ATTN_HEADS)]

    def fold(x):
        return x.reshape(KEY_CHUNK // 8, 8, w2)

    def exp_stage(hh, i, p_ref, l_ref, exact):
        start = i * Q_TILE if isinstance(i, int) else pl.multiple_of(i * Q_TILE, Q_TILE)
        qt = qt_ref[0, head(hh), pl.ds(start, Q_TILE)]
        zero = jnp.zeros_like(qt)
        rhs = jnp.concatenate([jnp.where(half0, qt, zero), jnp.where(half0, zero, qt)], axis=1)

        def scores(c):
            return jnp.dot(k_ref[0, c * KEY_CHUNK:(c + 1) * KEY_CHUNK, head(hh)], rhs,
                           preferred_element_type=F32)

        if exact:
            mx = jnp.full((8, w2), -jnp.inf, F32)
            for c in range(n_chunk):
                mx = jnp.maximum(mx, jnp.max(fold(scores(c)), axis=0))
            m = jnp.max(mx, axis=0, keepdims=True)
        else:
            rf = rhs.astype(F32)
            m = jnp.sqrt(jnp.sum(rf * rf, axis=0, keepdims=True)) * kmax[hh]
        l8 = jnp.zeros((8, w2), F32)
        for c in range(n_chunk):
            p = jnp.exp2(scores(c) - m)
            l8 = l8 + jnp.sum(fold(p), axis=0)
            p_ref[c * KEY_CHUNK:(c + 1) * KEY_CHUNK, :] = p.astype(BF16)
        l_ref[...] = l8
        return jnp.where(l8 > MIN_COLUMN_SUM, 0.0, 1.0)

    def value_stage(hh, p_ref, l_ref, ot_ref):
        l = jnp.sum(l_ref[...], axis=0, keepdims=True)
        l0, l1 = l[:, :Q_TILE], l[:, Q_TILE:]
        rho = (lam * l0 / l1).astype(BF16)
        for c in range(n_chunk):
            rows = slice(c * KEY_CHUNK, (c + 1) * KEY_CHUNK)
            a_ref[rows, :] = p_ref[rows, :Q_TILE] - p_ref[rows, Q_TILE:] * rho
        ot_ref[...] = jnp.dot(vt_ref[0, head(hh), :], a_ref[...], preferred_element_type=F32) * (1.0 / l0)

    def store_stage(hh, i, ot_ref):
        start = i * Q_TILE if isinstance(i, int) else pl.multiple_of(i * Q_TILE, Q_TILE)
        ot = ot_ref[...]
        ot = ot * lax.rsqrt(jnp.mean(ot * ot, axis=0, keepdims=True) + EPS) * out_gain
        o_ref[0, pl.ds(start, Q_TILE), head(hh)] = ot.T.astype(o_ref.dtype)

    even = (pa_ref, la_ref, ota_ref)
    odd = (pb_ref, lb_ref, otb_ref)
    last = n_q - 1
    bad = jnp.zeros((8, w2), F32)
    for hh in range(ATTN_HEADS):
        prev = hh - 1
        bad = jnp.maximum(bad, exp_stage(hh, 0, *even[:2], exact=False))
        if hh > 0:
            value_stage(prev, *odd)
            store_stage(prev, last - 1, even[2])
        bad = jnp.maximum(bad, exp_stage(hh, 1, *odd[:2], exact=False))
        value_stage(hh, *even)
        if hh > 0:
            store_stage(prev, last, odd[2])

        def body(j, bad, hh=hh):
            s = 2 * j + 2
            bad = jnp.maximum(bad, exp_stage(hh, s, *even[:2], exact=False))
            value_stage(hh, *odd)
            store_stage(hh, s - 2, even[2])
            bad = jnp.maximum(bad, exp_stage(hh, s + 1, *odd[:2], exact=False))
            value_stage(hh, *even)
            store_stage(hh, s - 1, odd[2])
            return bad

        bad = lax.fori_loop(0, n_q // 2 - 1, body, bad)
    value_stage(ATTN_HEADS - 1, *odd)
    store_stage(ATTN_HEADS - 1, last - 1, even[2])
    store_stage(ATTN_HEADS - 1, last, odd[2])

    @pl.when(jnp.max(bad) > 0.0)
    def _():
        for hh in range(ATTN_HEADS):
            def redo(i, carry, hh=hh):
                exp_stage(hh, i, *even[:2], exact=True)
                value_stage(hh, *even)
                store_stage(hh, i, even[2])
                return carry

            lax.fori_loop(0, n_q, redo, 0)


def _da_attn(k, vt, qt, lam_vecs, subln_g, lambda_init):
    b, tot, di = k.shape
    l_lat = qt.shape[-1]
    w = ATTN_HEADS * HEAD_W
    assert (l_lat // Q_TILE) % 2 == 0 and di % w == 0
    return pl.pallas_call(
        functools.partial(_da_attn_kernel, lambda_init=lambda_init),
        grid=(b, di // w),
        in_specs=[pl.BlockSpec((1, tot, w), lambda i, h: (i, 0, h)),
                  pl.BlockSpec((1, w, tot), lambda i, h: (i, h, 0)),
                  pl.BlockSpec((1, w, l_lat), lambda i, h: (i, h, 0)),
                  pl.BlockSpec((4, DA_DQK), lambda i, h: (0, 0)),
                  pl.BlockSpec((HEAD_W, 1), lambda i, h: (0, 0))],
        out_specs=pl.BlockSpec((1, l_lat, w), lambda i, h: (i, 0, h)),
        out_shape=jax.ShapeDtypeStruct((b, l_lat, di), BF16),
        scratch_shapes=[pltpu.VMEM((tot, 2 * Q_TILE), BF16),
                        pltpu.VMEM((tot, 2 * Q_TILE), BF16),
                        pltpu.VMEM((tot, Q_TILE), BF16),
                        pltpu.VMEM((8, 2 * Q_TILE), F32),
                        pltpu.VMEM((8, 2 * Q_TILE), F32),
                        pltpu.VMEM((HEAD_W, Q_TILE), F32),
                        pltpu.VMEM((HEAD_W, Q_TILE), F32)],
        compiler_params=_params(("arbitrary", "arbitrary")),
        name="da_attn",
    )(k, vt, qt, lam_vecs, subln_g.reshape(HEAD_W, 1))


def _da_out_kernel(o_ref, gate_ref, x_ref, gm_ref, fg_ref, w_ref, out_ref):
    y = o_ref[0] * gate_ref[0]
    x = x_ref[0] + gm_ref[0] * jnp.dot(y, w_ref[...], preferred_element_type=F32)
    out_ref[0] = x * lax.rsqrt(jnp.mean(x * x, axis=-1, keepdims=True) + EPS) * fg_ref[...]


def _da_out(o, gate, x1, gate_mod, final_g, w_out):
    b, l_lat, di = o.shape
    d = x1.shape[-1]
    tile = math.gcd(l_lat, OUT_TILE)
    tokw = pl.BlockSpec((1, tile, di), lambda i, t: (i, t, 0))
    return pl.pallas_call(
        _da_out_kernel,
        grid=(b, l_lat // tile),
        in_specs=[tokw, tokw,
                  pl.BlockSpec((1, tile, d), lambda i, t: (i, t, 0)),
                  pl.BlockSpec((1, 1, d), lambda i, t: (i, 0, 0)),
                  pl.BlockSpec((1, d), lambda i, t: (0, 0)),
                  _whole_vmem()],
        out_specs=pl.BlockSpec((1, tile, d), lambda i, t: (i, t, 0)),
        out_shape=jax.ShapeDtypeStruct((b, l_lat, d), F32),
        compiler_params=_params(("arbitrary", "arbitrary")),
        name="da_out",
    )(o, gate, x1, gate_mod, final_g, w_out)


def kernel(x, c, ctx, c_ctx, w_ada, b_ada, norm_g, hg_w_in, hg_lb_logits, hg_norm_g, hg_w_out,
           da_w_in, da_lam_q1, da_lam_k1, da_lam_q2, da_lam_k2, da_subln_g, da_w_out, final_g):
    b, l_lat, d = x.shape
    assert ctx.shape[1] == TOK_TILE and l_lat % TOK_TILE == 0 and l_lat % GRID_W == 0
    assert w_ada.shape[0] == 2 and hg_w_in.shape[0] == 1 and da_w_in.shape[0] == 1
    assert hg_lb_logits.shape[0] == 2

    rows = -(-(b + 1) // 8) * 8
    cc = jnp.concatenate([c, c_ctx[None], jnp.zeros((rows - b - 1, d), F32)], axis=0)
    ada = _adaln(cc, w_ada, b_ada)

    def mods(layer):
        m = ada[layer].reshape(rows, 3, d)
        pair = jnp.stack([jnp.broadcast_to(m[b], (b, 3, d)), m[:b]], axis=1)
        return pair[:, :, 0:2], pair[:, :, 2:3]

    mod0, gmod0 = mods(0)
    mod1, gmod1 = mods(1)

    q, gf, gb, v, gate0 = _hg_proj(ctx, x, mod0, norm_g[0:1], hg_w_in[0].astype(BF16), hg_lb_logits)
    of, ob = _hg_scan(q, gf, gb, v)
    x1, k, vt, qt, gate1 = _mid(of, ob, gate0, ctx, x, gmod0, hg_norm_g, hg_w_out[0].astype(BF16),
                                mod1, norm_g[1:2], da_w_in[0])

    lambda_init = 0.8 - 0.6 * math.exp(-0.3 * 1)
    lam_vecs = jnp.concatenate([da_lam_q1, da_lam_k1, da_lam_q2, da_lam_k2], axis=0)
    o = _da_attn(k, vt, qt, lam_vecs, da_subln_g[0], lambda_init)
    return _da_out(o, gate1, x1, gmod1[:, 1], final_g[None], da_w_out[0].astype(BF16))
```

```python
import functools
import math

import numpy as np
import jax
import jax.numpy as jnp
from jax import lax
from jax.experimental import pallas as pl
from jax.experimental.pallas import tpu as pltpu

F32 = jnp.float32
BF16 = jnp.bfloat16

EPS = 1e-6
GRID_W = 64
ROPE_THETA = 10000.0
HEAD_W = 128
DA_DQK = 64
TOK_TILE = 256
OUT_TILE = 1024
SCAN_CHUNK = 64
SCAN_HEADS = 16
Q_TILE = 256
KEY_CHUNK = 256
ATTN_HEADS = 2
BOUND_SLACK = 1.01
MIN_COLUMN_SUM = 2.0 ** -64
EXP2_CLAMP = 115.0
LOG2E = 1.4426950408889634
VMEM_LIMIT = 56 * 1024 * 1024

NT = (((1,), (1,)), ((), ()))
TN = (((0,), (0,)), ((), ()))


def _sigmoid(x):
    return 1.0 / (1.0 + jnp.exp(-x))


def _silu(x):
    return x * _sigmoid(x)


def _params(sem):
    return pltpu.CompilerParams(dimension_semantics=sem, vmem_limit_bytes=VMEM_LIMIT)


def _whole_vmem():
    return pl.BlockSpec(memory_space=pltpu.VMEM)


def _adaln_kernel(c_ref, w_ref, b_ref, o_ref):
    s = _silu(c_ref[...]).astype(BF16)
    o_ref[0] = jnp.dot(s, w_ref[0], preferred_element_type=F32) + b_ref[0]


def _adaln(cc, w_ada, b_ada):
    depth, d, d3 = w_ada.shape
    rows = cc.shape[0]
    return pl.pallas_call(
        _adaln_kernel,
        grid=(depth,),
        in_specs=[pl.BlockSpec((rows, d), lambda i: (0, 0)),
                  pl.BlockSpec((1, d, d3), lambda i: (i, 0, 0)),
                  pl.BlockSpec((1, 1, d3), lambda i: (i, 0, 0))],
        out_specs=pl.BlockSpec((1, rows, d3), lambda i: (i, 0, 0)),
        out_shape=jax.ShapeDtypeStruct((depth, rows, d3), F32),
        compiler_params=_params(("arbitrary",)),
        name="adaln",
    )(cc, w_ada.astype(BF16), b_ada.reshape(depth, 1, d3))


def _modulated_norm(x, g, shift, scale):
    y = x * lax.rsqrt(jnp.mean(x * x, axis=-1, keepdims=True) + EPS)
    return (y * g) * (1.0 + scale) + shift


def _hg_proj_kernel(ctx_ref, x_ref, mod_ref, ng_ref, w_ref, lbl_ref,
                    q_ref, gf_ref, gb_ref, v_ref, gate_ref):
    t = pl.program_id(1)
    x = jnp.where(t == 0, ctx_ref[0], x_ref[0])
    h = _modulated_norm(x, ng_ref[...], mod_ref[0, 0, 0:1, :], mod_ref[0, 0, 1:2, :]).astype(BF16)
    di = q_ref.shape[-1]

    def seg(j):
        return jnp.dot(h, w_ref[:, j * di:(j + 1) * di], preferred_element_type=F32)

    def log_forget(z, d):
        l0 = lbl_ref[0, d:d + 1, :]
        l1 = lbl_ref[1, d:d + 1, :]
        m = jnp.maximum(l0, l1)
        e0 = jnp.exp(l0 - m)
        lb = e0 / (e0 + jnp.exp(l1 - m))
        return jnp.log2(lb + (1.0 - lb) * _sigmoid(z))

    gf_ref[0] = log_forget(seg(1), 0).astype(BF16)
    gb_ref[0] = log_forget(seg(2), 1).astype(BF16)
    q_ref[0] = _silu(seg(0)).astype(BF16)
    gate_ref[0] = _silu(seg(4)).astype(BF16)
    v_ref[0] = seg(3).astype(BF16)


def _hg_proj(ctx, x, mod, norm_g, w_in, lb_logits):
    b, l, d = x.shape
    di = w_in.shape[1] // 5
    nt = l // TOK_TILE + 1
    tot = l + TOK_TILE
    tok = lambda dt: jax.ShapeDtypeStruct((b, tot, di), dt)
    out_blk = pl.BlockSpec((1, TOK_TILE, di), lambda i, t: (i, t, 0))
    return pl.pallas_call(
        _hg_proj_kernel,
        grid=(b, nt),
        in_specs=[pl.BlockSpec((1, TOK_TILE, d), lambda i, t: (i, 0, 0)),
                  pl.BlockSpec((1, TOK_TILE, d), lambda i, t: (i, jnp.maximum(t - 1, 0), 0)),
                  pl.BlockSpec((1, 1, 2, d), lambda i, t: (i, jnp.minimum(t, 1), 0, 0)),
                  pl.BlockSpec((1, d), lambda i, t: (0, 0)),
                  _whole_vmem(),
                  pl.BlockSpec((2, 2, di), lambda i, t: (0, 0, 0))],
        out_specs=[out_blk] * 5,
        out_shape=[tok(BF16)] * 5,
        compiler_params=_params(("arbitrary", "arbitrary")),
        name="hg_proj",
    )(ctx, x, mod, norm_g, w_in, lb_logits)


def _hg_scan_kernel(qf_ref, gf_ref, vf_ref, qb_ref, gb_ref, vb_ref, of_ref, ob_ref,
                    s_ref, bf_ref, bb_ref, tri_ref):
    t = pl.program_id(2)
    c = SCAN_CHUNK
    n_chunk = TOK_TILE // c
    w = SCAN_HEADS * HEAD_W

    @pl.when((pl.program_id(0) == 0) & (pl.program_id(1) == 0) & (t == 0))
    def _():
        row = lax.broadcasted_iota(jnp.int32, (TOK_TILE, TOK_TILE), 0)
        col = lax.broadcasted_iota(jnp.int32, (TOK_TILE, TOK_TILE), 1)
        same = (row // c) == (col // c)
        tri_ref[0] = jnp.where(same & (col <= row), 1.0, 0.0).astype(BF16)
        tri_ref[1] = jnp.where(same & (col >= row), 1.0, 0.0).astype(BF16)

    @pl.when(t == 0)
    def _():
        s_ref[...] = jnp.zeros_like(s_ref)

    def cumsum(tri, g):
        return jnp.dot(tri, g, preferred_element_type=F32)

    bf_ref[...] = cumsum(tri_ref[0], gf_ref[0])
    bb_ref[...] = cumsum(tri_ref[1], gb_ref[0])

    def chunk_decay_columns(b_ref, last_row):
        rows = [b_ref[j * c + last_row:j * c + last_row + 1, :] for j in range(n_chunk)]
        rows.append(jnp.zeros((8 - n_chunk, w), F32))
        return jnp.exp2(jnp.concatenate(rows, axis=0).T)

    dec_f = chunk_decay_columns(bf_ref, c - 1)
    dec_b = chunk_decay_columns(bb_ref, 0)

    ri = lax.broadcasted_iota(jnp.int32, (c, c), 0)
    ci = lax.broadcasted_iota(jnp.int32, (c, c), 1)
    keep_f = ci <= ri
    keep_b = ci >= ri

    def chunk_head(direction, j, hh, q_ref, g_ref, v_ref, b_ref, o_ref, dec, keep, mid_row, last_row):
        rows = slice(j * c, (j + 1) * c)
        lanes = slice(hh * HEAD_W, (hh + 1) * HEAD_W)
        q = q_ref[0, rows, lanes]
        g = g_ref[0, rows, lanes].astype(F32)
        v = v_ref[0, rows, lanes]
        bcum = b_ref[rows, lanes]
        r = bcum[mid_row:mid_row + 1, :]
        b_last = bcum[last_row:last_row + 1, :]
        d = jnp.clip(bcum - r, -EXP2_CLAMP, EXP2_CLAMP)
        q_mid = q * jnp.exp2(d).astype(BF16)
        k_mid = ((1.0 - jnp.exp2(g)) * jnp.exp2(-d)).astype(BF16)
        q_dec = q_mid * jnp.exp2(r).astype(BF16)
        k_dec = k_mid * jnp.exp2(b_last - r).astype(BF16)
        a = lax.dot_general(q_mid, k_mid, NT, preferred_element_type=F32)
        a = jnp.where(keep, a, 0.0).astype(BF16)
        s = s_ref[direction, hh]
        o_ref[0, rows, lanes] = jnp.dot(jnp.concatenate([q_dec, a], axis=1),
                                        jnp.concatenate([s.astype(BF16), v], axis=0),
                                        preferred_element_type=F32).astype(o_ref.dtype)
        s_ref[direction, hh] = s * dec[lanes, j:j + 1] + lax.dot_general(
            k_dec, v, TN, preferred_element_type=F32)

    for i in range(n_chunk):
        for hh in range(SCAN_HEADS):
            chunk_head(0, i, hh, qf_ref, gf_ref, vf_ref, bf_ref, of_ref, dec_f, keep_f, c // 2 - 1, c - 1)
            chunk_head(1, n_chunk - 1 - i, hh, qb_ref, gb_ref, vb_ref, bb_ref, ob_ref, dec_b, keep_b, c // 2, 0)


def _hg_scan(q, gf, gb, v):
    b, tot, di = q.shape
    nt = tot // TOK_TILE
    w = SCAN_HEADS * HEAD_W
    fwd = pl.BlockSpec((1, TOK_TILE, w), lambda i, h, t: (i, t, h))
    bwd = pl.BlockSpec((1, TOK_TILE, w), lambda i, h, t: (i, jnp.where(t == 0, 0, nt - t), h))
    out = jax.ShapeDtypeStruct((b, tot, di), BF16)
    return pl.pallas_call(
        _hg_scan_kernel,
        grid=(b, di // w, nt),
        in_specs=[fwd, fwd, fwd, bwd, bwd, bwd],
        out_specs=[fwd, bwd],
        out_shape=[out, out],
        scratch_shapes=[pltpu.VMEM((2, SCAN_HEADS, HEAD_W, HEAD_W), F32),
                        pltpu.VMEM((TOK_TILE, w), F32),
                        pltpu.VMEM((TOK_TILE, w), F32),
                        pltpu.VMEM((2, TOK_TILE, TOK_TILE), BF16)],
        compiler_params=_params(("arbitrary", "arbitrary", "arbitrary")),
        name="hg_scan",
    )(q, gf, v, q, gb, v)


def _rope_lane_tables(l_lat):
    ax = DA_DQK // 2
    inv = 1.0 / (ROPE_THETA ** (np.arange(0, ax, 2, dtype=np.float64) / ax))
    pos = np.arange(l_lat)
    ang_r = (pos // GRID_W)[:, None] * inv
    ang_c = (pos % GRID_W)[:, None] * inv
    zero = np.zeros_like(ang_r)
    cos64 = np.concatenate([np.cos(ang_r), np.cos(ang_r), np.cos(ang_c), np.cos(ang_c)], axis=1)
    up64 = np.concatenate([-np.sin(ang_r), zero, -np.sin(ang_c), zero], axis=1)
    dn64 = np.concatenate([zero, np.sin(ang_r), zero, np.sin(ang_c)], axis=1)
    lat = [np.concatenate([t, t], axis=1) for t in (cos64, up64, dn64)]
    ident = [np.ones((TOK_TILE, HEAD_W)), np.zeros((TOK_TILE, HEAD_W)), np.zeros((TOK_TILE, HEAD_W))]
    return [np.concatenate([i, t], axis=0).astype(np.float32) for i, t in zip(ident, lat)], (cos64, up64, dn64)


def _mid_kernel(of_ref, ob_ref, gate0_ref, ctx_ref, x_ref, gm_ref, hng_ref, wo_ref,
                mod_ref, ng_ref, wq_ref, wk_ref, wv_ref, wg_ref,
                kc_ref, ku_ref, kd_ref, qc_ref, qs_ref,
                x1_ref, k_ref, vt_ref, qt_ref, gate_ref, y_ref):
    t = pl.program_id(1)
    n_head = k_ref.shape[-1] // HEAD_W

    for hh in range(n_head):
        lanes = slice(hh * HEAD_W, (hh + 1) * HEAD_W)
        o = of_ref[0, :, lanes].astype(F32) + ob_ref[0, :, lanes].astype(F32)
        o = o * lax.rsqrt(jnp.mean(o * o, axis=-1, keepdims=True) + EPS) * hng_ref[:, lanes]
        y_ref[:, lanes] = (o * gate0_ref[0, :, lanes].astype(F32)).astype(BF16)
    y = jnp.dot(y_ref[...], wo_ref[...], preferred_element_type=F32)
    x1 = jnp.where(t == 0, ctx_ref[0], x_ref[0]) + gm_ref[0, 0] * y

    h = _modulated_norm(x1, ng_ref[...], mod_ref[0, 0, 0:1, :], mod_ref[0, 0, 1:2, :]).astype(BF16)
    k = jnp.dot(h, wk_ref[...], preferred_element_type=F32)
    cos, s_up, s_dn = kc_ref[...], ku_ref[...], kd_ref[...]
    for hh in range(n_head):
        lanes = slice(hh * HEAD_W, (hh + 1) * HEAD_W)
        kh = k[:, lanes]
        rot = (kh * cos + pltpu.roll(kh, HEAD_W - 16, axis=1) * s_up
               + pltpu.roll(kh, 16, axis=1) * s_dn)
        k_ref[0, :, lanes] = rot.astype(BF16)

    vt_ref[0] = lax.dot_general(wv_ref[...], h, NT, preferred_element_type=F32).astype(BF16)

    @pl.when(t > 0)
    def _():
        x1_ref[0] = x1
        gate_ref[0] = _silu(jnp.dot(h, wg_ref[...], preferred_element_type=F32)).astype(BF16)
        qt = lax.dot_general(wq_ref[...], h, NT, preferred_element_type=F32)
        qc, qs = qc_ref[...], qs_ref[...]
        for grp in range(qt.shape[0] // DA_DQK):
            x = qt[grp * DA_DQK:(grp + 1) * DA_DQK]
            partner = jnp.concatenate([x[16:32], x[0:16], x[48:64], x[32:48]], axis=0)
            qt_ref[0, grp * DA_DQK:(grp + 1) * DA_DQK, :] = (x * qc + partner * qs).astype(BF16)


def _mid(of, ob, gate0, ctx, x, gate_mod, hg_norm_g, w_out, mod, norm_g, w_in):
    b, tot, di = of.shape
    l_lat, d = x.shape[1], x.shape[2]
    nt = tot // TOK_TILE
    wq, wk, wv, wg = (w_in[:, j * di:(j + 1) * di] for j in range(4))
    ktabs, (cos64, up64, dn64) = _rope_lane_tables(l_lat)
    q_scale = DA_DQK ** -0.5 * LOG2E
    qc = (cos64.T * q_scale).astype(np.float32)
    qs = ((up64 + dn64).T * q_scale).astype(np.float32)
    lat_t = lambda t: jnp.maximum(t - 1, 0)
    tokw = pl.BlockSpec((1, TOK_TILE, di), lambda i, t: (i, t, 0))
    ktab = pl.BlockSpec((TOK_TILE, HEAD_W), lambda i, t: (t, 0))
    qtab = pl.BlockSpec((DA_DQK, TOK_TILE), lambda i, t: (0, lat_t(t)))
    return pl.pallas_call(
        _mid_kernel,
        grid=(b, nt),
        in_specs=[tokw, tokw, tokw,
                  pl.BlockSpec((1, TOK_TILE, d), lambda i, t: (i, 0, 0)),
                  pl.BlockSpec((1, TOK_TILE, d), lambda i, t: (i, lat_t(t), 0)),
                  pl.BlockSpec((1, 1, 1, d), lambda i, t: (i, jnp.minimum(t, 1), 0, 0)),
                  pl.BlockSpec((1, di), lambda i, t: (0, 0)),
                  _whole_vmem(),
                  pl.BlockSpec((1, 1, 2, d), lambda i, t: (i, jnp.minimum(t, 1), 0, 0)),
                  pl.BlockSpec((1, d), lambda i, t: (0, 0)),
                  _whole_vmem(), _whole_vmem(), _whole_vmem(), _whole_vmem(),
                  ktab, ktab, ktab, qtab, qtab],
        out_specs=[pl.BlockSpec((1, TOK_TILE, d), lambda i, t: (i, lat_t(t), 0)),
                   pl.BlockSpec((1, TOK_TILE, di), lambda i, t: (i, t, 0)),
                   pl.BlockSpec((1, di, TOK_TILE), lambda i, t: (i, 0, t)),
                   pl.BlockSpec((1, di, TOK_TILE), lambda i, t: (i, 0, lat_t(t))),
                   pl.BlockSpec((1, TOK_TILE, di), lambda i, t: (i, lat_t(t), 0))],
        out_shape=[jax.ShapeDtypeStruct((b, l_lat, d), F32),
                   jax.ShapeDtypeStruct((b, tot, di), BF16),
                   jax.ShapeDtypeStruct((b, di, tot), BF16),
                   jax.ShapeDtypeStruct((b, di, l_lat), BF16),
                   jax.ShapeDtypeStruct((b, l_lat, di), BF16)],
        scratch_shapes=[pltpu.VMEM((TOK_TILE, di), BF16)],
        compiler_params=_params(("arbitrary", "arbitrary")),
        name="hg_out_da_proj",
    )(of, ob, gate0, ctx, x, gate_mod, hg_norm_g, w_out, mod, norm_g,
      wq.T.astype(BF16), wk.astype(BF16), wv.T.astype(BF16), wg.astype(BF16),
      *[jnp.asarray(a) for a in ktabs], jnp.asarray(qc), jnp.asarray(qs))


def _da_attn_kernel(k_ref, vt_ref, qt_ref, lam_ref, sg_ref, o_ref, pa_ref, pb_ref, a_ref, la_ref, lb_ref,
                    ota_ref, otb_ref, *, lambda_init):
    lq1, lk1, lq2, lk2 = (lam_ref[i:i + 1, :] for i in range(4))
    lam = (jnp.exp(jnp.sum(lq1 * lk1, axis=-1, keepdims=True))
           - jnp.exp(jnp.sum(lq2 * lk2, axis=-1, keepdims=True)) + lambda_init)
    out_gain = sg_ref[...] * (1.0 - lambda_init)
    n_chunk = k_ref.shape[1] // KEY_CHUNK
    n_q = qt_ref.shape[-1] // Q_TILE
    w2 = 2 * Q_TILE
    half0 = lax.broadcasted_iota(jnp.int32, (HEAD_W, Q_TILE), 0) < DA_DQK

    def head(hh):
        return slice(hh * HEAD_W, (hh + 1) * HEAD_W)

    def key_norm_bound(hh):
        kf = k_ref[0, :, head(hh)].astype(F32)
        sel_r = lax.broadcasted_iota(jnp.int32, (HEAD_W, HEAD_W), 0) // DA_DQK
        sel_c = lax.broadcasted_iota(jnp.int32, (HEAD_W, HEAD_W), 1)
        sel = jnp.where(sel_r == sel_c, 1.0, 0.0).astype(BF16)
        norm2 = jnp.dot((kf * kf).astype(BF16), sel, preferred_element_type=F32)
        kmax = jnp.sqrt(jnp.max(norm2, axis=0, keepdims=True)) * BOUND_SLACK
        return jnp.concatenate([jnp.broadcast_to(kmax[:, 0:1], (1, Q_TILE)),
                                jnp.broadcast_to(kmax[:, 1:2], (1, Q_TILE))], axis=1)

    kmax = [key_norm_bound(hh) for hh in range(ATTN_HEADS)]

    def fold(x):
        return x.reshape(KEY_CHUNK // 8, 8, w2)

    def exp_stage(hh, i, p_ref, l_ref, exact):
        start = i * Q_TILE if isinstance(i, int) else pl.multiple_of(i * Q_TILE, Q_TILE)
        qt = qt_ref[0, head(hh), pl.ds(start, Q_TILE)]
        zero = jnp.zeros_like(qt)
        rhs = jnp.concatenate([jnp.where(half0, qt, zero), jnp.where(half0, zero, qt)], axis=1)

        def scores(c):
            return jnp.dot(k_ref[0, c * KEY_CHUNK:(c + 1) * KEY_CHUNK, head(hh)], rhs,
                           preferred_element_type=F32)

        if exact:
            mx = jnp.full((8, w2), -jnp.inf, F32)
            for c in range(n_chunk):
                mx = jnp.maximum(mx, jnp.max(fold(scores(c)), axis=0))
            m = jnp.max(mx, axis=0, keepdims=True)
        else:
            rf = rhs.astype(F32)
            m = jnp.sqrt(jnp.sum(rf * rf, axis=0, keepdims=True)) * kmax[hh]
        l8 = jnp.zeros((8, w2), F32)
        for c in range(n_chunk):
            p = jnp.exp2(scores(c) - m)
            l8 = l8 + jnp.sum(fold(p), axis=0)
            p_ref[c * KEY_CHUNK:(c + 1) * KEY_CHUNK, :] = p.astype(BF16)
        l_ref[...] = l8
        return jnp.where(l8 > MIN_COLUMN_SUM, 0.0, 1.0)

    def value_stage(hh, p_ref, l_ref, ot_ref):
        l = jnp.sum(l_ref[...], axis=0, keepdims=True)
        l0, l1 = l[:, :Q_TILE], l[:, Q_TILE:]
        rho = (lam * l0 / l1).astype(BF16)
        for c in range(n_chunk):
            rows = slice(c * KEY_CHUNK, (c + 1) * KEY_CHUNK)
            a_ref[rows, :] = p_ref[rows, :Q_TILE] - p_ref[rows, Q_TILE:] * rho
        ot_ref[...] = jnp.dot(vt_ref[0, head(hh), :], a_ref[...], preferred_element_type=F32) * (1.0 / l0)

    def store_stage(hh, i, ot_ref):
        start = i * Q_TILE if isinstance(i, int) else pl.multiple_of(i * Q_TILE, Q_TILE)
        ot = ot_ref[...]
        ot = ot * lax.rsqrt(jnp.mean(ot * ot, axis=0, keepdims=True) + EPS) * out_gain
        o_ref[0, pl.ds(start, Q_TILE), head(hh)] = ot.T.astype(o_ref.dtype)

    even = (pa_ref, la_ref, ota_ref)
    odd = (pb_ref, lb_ref, otb_ref)
    last = n_q - 1
    bad = jnp.zeros((8, w2), F32)
    for hh in range(ATTN_HEADS):
        prev = hh - 1
        bad = jnp.maximum(bad, exp_stage(hh, 0, *even[:2], exact=False))
        if hh > 0:
            value_stage(prev, *odd)
            store_stage(prev, last - 1, even[2])
        bad = jnp.maximum(bad, exp_stage(hh, 1, *odd[:2], exact=False))
        value_stage(hh, *even)
        if hh > 0:
            store_stage(prev, last, odd[2])

        def body(j, bad, hh=hh):
            s = 2 * j + 2
            bad = jnp.maximum(bad, exp_stage(hh, s, *even[:2], exact=False))
            value_stage(hh, *odd)
            store_stage(hh, s - 2, even[2])
            bad = jnp.maximum(bad, exp_stage(hh, s + 1, *odd[:2], exact=False))
            value_stage(hh, *even)
            store_stage(hh, s - 1, odd[2])
            return bad

        bad = lax.fori_loop(0, n_q // 2 - 1, body, bad)
    value_stage(ATTN_HEADS - 1, *odd)
    store_stage(ATTN_HEADS - 1, last - 1, even[2])
    store_stage(ATTN_HEADS - 1, last, odd[2])

    @pl.when(jnp.max(bad) > 0.0)
    def _():
        for hh in range(ATTN_HEADS):
            def redo(i, carry, hh=hh):
                exp_stage(hh, i, *even[:2], exact=True)
                value_stage(hh, *even)
                store_stage(hh, i, even[2])
                return carry

            lax.fori_loop(0, n_q, redo, 0)


def _da_attn(k, vt, qt, lam_vecs, subln_g, lambda_init):
    b, tot, di = k.shape
    l_lat = qt.shape[-1]
    w = ATTN_HEADS * HEAD_W
    assert (l_lat // Q_TILE) % 2 == 0 and di % w == 0
    return pl.pallas_call(
        functools.partial(_da_attn_kernel, lambda_init=lambda_init),
        grid=(b, di // w),
        in_specs=[pl.BlockSpec((1, tot, w), lambda i, h: (i, 0, h)),
                  pl.BlockSpec((1, w, tot), lambda i, h: (i, h, 0)),
                  pl.BlockSpec((1, w, l_lat), lambda i, h: (i, h, 0)),
                  pl.BlockSpec((4, DA_DQK), lambda i, h: (0, 0)),
                  pl.BlockSpec((HEAD_W, 1), lambda i, h: (0, 0))],
        out_specs=pl.BlockSpec((1, l_lat, w), lambda i, h: (i, 0, h)),
        out_shape=jax.ShapeDtypeStruct((b, l_lat, di), BF16),
        scratch_shapes=[pltpu.VMEM((tot, 2 * Q_TILE), BF16),
                        pltpu.VMEM((tot, 2 * Q_TILE), BF16),
                        pltpu.VMEM((tot, Q_TILE), BF16),
                        pltpu.VMEM((8, 2 * Q_TILE), F32),
                        pltpu.VMEM((8, 2 * Q_TILE), F32),
                        pltpu.VMEM((HEAD_W, Q_TILE), F32),
                        pltpu.VMEM((HEAD_W, Q_TILE), F32)],
        compiler_params=_params(("arbitrary", "arbitrary")),
        name="da_attn",
    )(k, vt, qt, lam_vecs, subln_g.reshape(HEAD_W, 1))


def _da_out_kernel(o_ref, gate_ref, x_ref, gm_ref, fg_ref, w_ref, out_ref):
    y = o_ref[0] * gate_ref[0]
    x = x_ref[0] + gm_ref[0] * jnp.dot(y, w_ref[...], preferred_element_type=F32)
    out_ref[0] = x * lax.rsqrt(jnp.mean(x * x, axis=-1, keepdims=True) + EPS) * fg_ref[...]


def _da_out(o, gate, x1, gate_mod, final_g, w_out):
    b, l_lat, di = o.shape
    d = x1.shape[-1]
    tile = math.gcd(l_lat, OUT_TILE)
    tokw = pl.BlockSpec((1, tile, di), lambda i, t: (i, t, 0))
    return pl.pallas_call(
        _da_out_kernel,
        grid=(b, l_lat // tile),
        in_specs=[tokw, tokw,
                  pl.BlockSpec((1, tile, d), lambda i, t: (i, t, 0)),
                  pl.BlockSpec((1, 1, d), lambda i, t: (i, 0, 0)),
                  pl.BlockSpec((1, d), lambda i, t: (0, 0)),
                  _whole_vmem()],
        out_specs=pl.BlockSpec((1, tile, d), lambda i, t: (i, t, 0)),
        out_shape=jax.ShapeDtypeStruct((b, l_lat, d), F32),
        compiler_params=_params(("arbitrary", "arbitrary")),
        name="da_out",
    )(o, gate, x1, gate_mod, final_g, w_out)


def kernel(x, c, ctx, c_ctx, w_ada, b_ada, norm_g, hg_w_in, hg_lb_logits, hg_norm_g, hg_w_out,
           da_w_in, da_lam_q1, da_lam_k1, da_lam_q2, da_lam_k2, da_subln_g, da_w_out, final_g):
    b, l_lat, d = x.shape
    assert ctx.shape[1] == TOK_TILE and l_lat % TOK_TILE == 0 and l_lat % GRID_W == 0
    assert w_ada.shape[0] == 2 and hg_w_in.shape[0] == 1 and da_w_in.shape[0] == 1
    assert hg_lb_logits.shape[0] == 2

    rows = -(-(b + 1) // 8) * 8
    cc = jnp.concatenate([c, c_ctx[None], jnp.zeros((rows - b - 1, d), F32)], axis=0)
    ada = _adaln(cc, w_ada, b_ada)

    def mods(layer):
        m = ada[layer].reshape(rows, 3, d)
        pair = jnp.stack([jnp.broadcast_to(m[b], (b, 3, d)), m[:b]], axis=1)
        return pair[:, :, 0:2], pair[:, :, 2:3]

    mod0, gmod0 = mods(0)
    mod1, gmod1 = mods(1)

    q, gf, gb, v, gate0 = _hg_proj(ctx, x, mod0, norm_g[0:1], hg_w_in[0].astype(BF16), hg_lb_logits)
    of, ob = _hg_scan(q, gf, gb, v)
    x1, k, vt, qt, gate1 = _mid(of, ob, gate0, ctx, x, gmod0, hg_norm_g, hg_w_out[0].astype(BF16),
                                mod1, norm_g[1:2], da_w_in[0])

    lambda_init = 0.8 - 0.6 * math.exp(-0.3 * 1)
    lam_vecs = jnp.concatenate([da_lam_q1, da_lam_k1, da_lam_q2, da_lam_k2], axis=0)
    o = _da_attn(k, vt, qt, lam_vecs, da_subln_g[0], lambda_init)
    return _da_out(o, gate1, x1, gmod1[:, 1], final_g[None], da_w_out[0].astype(BF16))
```

```python
import functools
import math

import numpy as np
import jax
import jax.numpy as jnp
from jax import lax
from jax.experimental import pallas as pl
from jax.experimental.pallas import tpu as pltpu

F32 = jnp.float32
BF16 = jnp.bfloat16

EPS = 1e-6
GRID_W = 64
ROPE_THETA = 10000.0
HEAD_W = 128
DA_DQK = 64
TOK_TILE = 256
OUT_TILE = 1024
SCAN_CHUNK = 64
SCAN_HEADS = 16
Q_TILE = 256
KEY_CHUNK = 256
ATTN_HEADS = 2
BOUND_SLACK = 1.01
MIN_COLUMN_SUM = 2.0 ** -64
MAX_CHUNK_LOG2_DECAY = 100
SCAN_EXACT_CHUNK = 16
LOG2E = 1.4426950408889634
VMEM_LIMIT = 56 * 1024 * 1024

NT = (((1,), (1,)), ((), ()))
TN = (((0,), (0,)), ((), ()))


def _sigmoid(x):
    return 1.0 / (1.0 + jnp.exp(-x))


def _silu(x):
    return x * _sigmoid(x)


def _params(sem):
    return pltpu.CompilerParams(dimension_semantics=sem, vmem_limit_bytes=VMEM_LIMIT)


def _whole_vmem():
    return pl.BlockSpec(memory_space=pltpu.VMEM)


def _adaln_kernel(c_ref, w_ref, b_ref, o_ref):
    s = _silu(c_ref[...]).astype(BF16)
    o_ref[0] = jnp.dot(s, w_ref[0], preferred_element_type=F32) + b_ref[0]


def _adaln(cc, w_ada, b_ada):
    depth, d, d3 = w_ada.shape
    rows = cc.shape[0]
    return pl.pallas_call(
        _adaln_kernel,
        grid=(depth,),
        in_specs=[pl.BlockSpec((rows, d), lambda i: (0, 0)),
                  pl.BlockSpec((1, d, d3), lambda i: (i, 0, 0)),
                  pl.BlockSpec((1, 1, d3), lambda i: (i, 0, 0))],
        out_specs=pl.BlockSpec((1, rows, d3), lambda i: (i, 0, 0)),
        out_shape=jax.ShapeDtypeStruct((depth, rows, d3), F32),
        compiler_params=_params(("arbitrary",)),
        name="adaln",
    )(cc, w_ada.astype(BF16), b_ada.reshape(depth, 1, d3))


def _modulated_norm(x, g, shift, scale):
    y = x * lax.rsqrt(jnp.mean(x * x, axis=-1, keepdims=True) + EPS)
    return (y * g) * (1.0 + scale) + shift


def _hg_proj_kernel(ctx_ref, x_ref, mod_ref, ng_ref, w_ref, lbl_ref,
                    q_ref, gf_ref, gb_ref, v_ref, gate_ref):
    t = pl.program_id(1)
    x = jnp.where(t == 0, ctx_ref[0], x_ref[0])
    h = _modulated_norm(x, ng_ref[...], mod_ref[0, 0, 0:1, :], mod_ref[0, 0, 1:2, :]).astype(BF16)
    di = q_ref.shape[-1]

    def seg(j):
        return jnp.dot(h, w_ref[:, j * di:(j + 1) * di], preferred_element_type=F32)

    def log_forget(z, d):
        l0 = lbl_ref[0, d:d + 1, :]
        l1 = lbl_ref[1, d:d + 1, :]
        m = jnp.maximum(l0, l1)
        e0 = jnp.exp(l0 - m)
        lb = e0 / (e0 + jnp.exp(l1 - m))
        return jnp.log2(lb + (1.0 - lb) * _sigmoid(z))

    gf_ref[0] = log_forget(seg(1), 0).astype(BF16)
    gb_ref[0] = log_forget(seg(2), 1).astype(BF16)
    q_ref[0] = _silu(seg(0)).astype(BF16)
    gate_ref[0] = _silu(seg(4)).astype(BF16)
    v_ref[0] = seg(3).astype(BF16)


def _hg_proj(ctx, x, mod, norm_g, w_in, lb_logits):
    b, l, d = x.shape
    di = w_in.shape[1] // 5
    nt = l // TOK_TILE + 1
    tot = l + TOK_TILE
    tok = lambda dt: jax.ShapeDtypeStruct((b, tot, di), dt)
    out_blk = pl.BlockSpec((1, TOK_TILE, di), lambda i, t: (i, t, 0))
    return pl.pallas_call(
        _hg_proj_kernel,
        grid=(b, nt),
        in_specs=[pl.BlockSpec((1, TOK_TILE, d), lambda i, t: (i, 0, 0)),
                  pl.BlockSpec((1, TOK_TILE, d), lambda i, t: (i, jnp.maximum(t - 1, 0), 0)),
                  pl.BlockSpec((1, 1, 2, d), lambda i, t: (i, jnp.minimum(t, 1), 0, 0)),
                  pl.BlockSpec((1, d), lambda i, t: (0, 0)),
                  _whole_vmem(),
                  pl.BlockSpec((2, 2, di), lambda i, t: (0, 0, 0))],
        out_specs=[out_blk] * 5,
        out_shape=[tok(BF16)] * 5,
        compiler_params=_params(("arbitrary", "arbitrary")),
        name="hg_proj",
    )(ctx, x, mod, norm_g, w_in, lb_logits)


def _hg_scan_kernel(qf_ref, gf_ref, vf_ref, qb_ref, gb_ref, vb_ref, of_ref, ob_ref,
                    s_ref, bf_ref, bb_ref, tri_ref):
    t = pl.program_id(2)
    c = SCAN_CHUNK
    n_chunk = TOK_TILE // c
    w = SCAN_HEADS * HEAD_W

    @pl.when((pl.program_id(0) == 0) & (pl.program_id(1) == 0) & (t == 0))
    def _():
        row = lax.broadcasted_iota(jnp.int32, (TOK_TILE, TOK_TILE), 0)
        col = lax.broadcasted_iota(jnp.int32, (TOK_TILE, TOK_TILE), 1)
        same = (row // c) == (col // c)
        tri_ref[0] = jnp.where(same & (col <= row), 1.0, 0.0).astype(BF16)
        tri_ref[1] = jnp.where(same & (col >= row), 1.0, 0.0).astype(BF16)

    @pl.when(t == 0)
    def _():
        s_ref[...] = jnp.zeros_like(s_ref)

    def cumsum(tri, g):
        return jnp.dot(tri, g, preferred_element_type=F32)

    bf_ref[...] = cumsum(tri_ref[0], gf_ref[0])
    bb_ref[...] = cumsum(tri_ref[1], gb_ref[0])

    def chunk_decay_columns(b_ref, last_row):
        rows = [b_ref[j * c + last_row:j * c + last_row + 1, :] for j in range(n_chunk)]
        rows.append(jnp.zeros((8 - n_chunk, w), F32))
        return jnp.exp2(jnp.concatenate(rows, axis=0).T)

    dec_f = chunk_decay_columns(bf_ref, c - 1)
    dec_b = chunk_decay_columns(bb_ref, 0)

    ri = lax.broadcasted_iota(jnp.int32, (c, c), 0)
    ci = lax.broadcasted_iota(jnp.int32, (c, c), 1)
    keep_f = ci <= ri
    keep_b = ci >= ri

    def chunk_head(direction, j, hh, q_ref, g_ref, v_ref, b_ref, o_ref, dec, keep, mid_row, last_row):
        rows = slice(j * c, (j + 1) * c)
        lanes = slice(hh * HEAD_W, (hh + 1) * HEAD_W)
        q = q_ref[0, rows, lanes]
        g = g_ref[0, rows, lanes].astype(F32)
        v = v_ref[0, rows, lanes]
        bcum = b_ref[rows, lanes]
        r = bcum[mid_row:mid_row + 1, :]
        b_last = bcum[last_row:last_row + 1, :]
        d = bcum - r
        q_mid = q * jnp.exp2(d).astype(BF16)
        k_mid = ((1.0 - jnp.exp2(g)) * jnp.exp2(-d)).astype(BF16)
        q_dec = q_mid * jnp.exp2(r).astype(BF16)
        k_dec = k_mid * jnp.exp2(b_last - r).astype(BF16)
        a = lax.dot_general(q_mid, k_mid, NT, preferred_element_type=F32)
        a = jnp.where(keep, a, 0.0).astype(BF16)
        s = s_ref[direction, hh]
        o_ref[0, rows, lanes] = jnp.dot(jnp.concatenate([q_dec, a], axis=1),
                                        jnp.concatenate([s.astype(BF16), v], axis=0),
                                        preferred_element_type=F32).astype(o_ref.dtype)
        s_ref[direction, hh] = s * dec[lanes, j:j + 1] + lax.dot_general(
            k_dec, v, TN, preferred_element_type=F32)

    mild = jnp.min(jnp.minimum(dec_f, dec_b)) > 2.0 ** -MAX_CHUNK_LOG2_DECAY

    @pl.when(mild)
    def _():
        for i in range(n_chunk):
            for hh in range(SCAN_HEADS):
                chunk_head(0, i, hh, qf_ref, gf_ref, vf_ref, bf_ref, of_ref, dec_f, keep_f, c // 2 - 1, c - 1)
                chunk_head(1, n_chunk - 1 - i, hh, qb_ref, gb_ref, vb_ref, bb_ref, ob_ref, dec_b, keep_b,
                           c // 2, 0)

    @pl.when(jnp.logical_not(mild))
    def _():
        _hg_scan_exact_tile((qf_ref, gf_ref, vf_ref, of_ref), (qb_ref, gb_ref, vb_ref, ob_ref), s_ref)


def _hg_scan_exact_tile(fwd_refs, bwd_refs, s_ref):
    sub = SCAN_EXACT_CHUNK
    n_sub = TOK_TILE // sub
    ri = lax.broadcasted_iota(jnp.int32, (sub, sub), 0)
    ci = lax.broadcasted_iota(jnp.int32, (sub, sub), 1)
    tri = (jnp.where(ci <= ri, 1.0, 0.0).astype(BF16), jnp.where(ci >= ri, 1.0, 0.0).astype(BF16))
    jcol = lax.broadcasted_iota(jnp.int32, (sub, 1), 0)

    def head_body(hh, carry):
        lanes = pl.ds(pl.multiple_of(hh * HEAD_W, HEAD_W), HEAD_W)

        def sub_body(step, carry):
            for direction, (q_ref, g_ref, v_ref, o_ref) in enumerate((fwd_refs, bwd_refs)):
                fwd = direction == 0
                cc = step if fwd else n_sub - 1 - step
                rows = pl.ds(pl.multiple_of(cc * sub, sub), sub)
                q = q_ref[0, rows, lanes].astype(F32)
                g = g_ref[0, rows, lanes]
                v = v_ref[0, rows, lanes]
                b = jnp.dot(tri[direction], g, preferred_element_type=F32)
                k = 1.0 - jnp.exp2(g.astype(F32))
                b_last = b[sub - 1:sub, :] if fwd else b[0:1, :]
                s = s_ref[direction, hh]
                o = jnp.dot((q * jnp.exp2(b)).astype(BF16), s.astype(BF16), preferred_element_type=F32)
                vf = v.astype(F32)
                intra = []
                for i in range(sub):
                    e = jnp.exp2(jnp.minimum(b[i:i + 1, :] - b, 0.0))
                    a = jnp.sum(q[i:i + 1, :] * e * k, axis=1, keepdims=True)
                    a = jnp.where((jcol <= i) if fwd else (jcol >= i), a, 0.0)
                    intra.append(jnp.sum(a * vf, axis=0, keepdims=True))
                o_ref[0, rows, lanes] = (o + jnp.concatenate(intra, axis=0)).astype(o_ref.dtype)
                k_dec = (k * jnp.exp2(b_last - b)).astype(BF16)
                dec_col = jnp.broadcast_to(jnp.exp2(b_last), (8, HEAD_W)).T[:, 0:1]
                s_ref[direction, hh] = s * dec_col + lax.dot_general(k_dec, v, TN, preferred_element_type=F32)
            return carry

        return lax.fori_loop(0, n_sub, sub_body, carry)

    lax.fori_loop(0, SCAN_HEADS, head_body, 0)


def _hg_scan(q, gf, gb, v):
    b, tot, di = q.shape
    nt = tot // TOK_TILE
    w = SCAN_HEADS * HEAD_W
    fwd = pl.BlockSpec((1, TOK_TILE, w), lambda i, h, t: (i, t, h))
    bwd = pl.BlockSpec((1, TOK_TILE, w), lambda i, h, t: (i, jnp.where(t == 0, 0, nt - t), h))
    out = jax.ShapeDtypeStruct((b, tot, di), BF16)
    return pl.pallas_call(
        _hg_scan_kernel,
        grid=(b, di // w, nt),
        in_specs=[fwd, fwd, fwd, bwd, bwd, bwd],
        out_specs=[fwd, bwd],
        out_shape=[out, out],
        scratch_shapes=[pltpu.VMEM((2, SCAN_HEADS, HEAD_W, HEAD_W), F32),
                        pltpu.VMEM((TOK_TILE, w), F32),
                        pltpu.VMEM((TOK_TILE, w), F32),
                        pltpu.VMEM((2, TOK_TILE, TOK_TILE), BF16)],
        compiler_params=_params(("arbitrary", "arbitrary", "arbitrary")),
        name="hg_scan",
    )(q, gf, v, q, gb, v)


def _rope_lane_tables(l_lat):
    ax = DA_DQK // 2
    inv = 1.0 / (ROPE_THETA ** (np.arange(0, ax, 2, dtype=np.float64) / ax))
    pos = np.arange(l_lat)
    ang_r = (pos // GRID_W)[:, None] * inv
    ang_c = (pos % GRID_W)[:, None] * inv
    zero = np.zeros_like(ang_r)
    cos64 = np.concatenate([np.cos(ang_r), np.cos(ang_r), np.cos(ang_c), np.cos(ang_c)], axis=1)
    up64 = np.concatenate([-np.sin(ang_r), zero, -np.sin(ang_c), zero], axis=1)
    dn64 = np.concatenate([zero, np.sin(ang_r), zero, np.sin(ang_c)], axis=1)
    lat = [np.concatenate([t, t], axis=1) for t in (cos64, up64, dn64)]
    ident = [np.ones((TOK_TILE, HEAD_W)), np.zeros((TOK_TILE, HEAD_W)), np.zeros((TOK_TILE, HEAD_W))]
    return [np.concatenate([i, t], axis=0).astype(np.float32) for i, t in zip(ident, lat)], (cos64, up64, dn64)


def _mid_kernel(of_ref, ob_ref, gate0_ref, ctx_ref, x_ref, gm_ref, hng_ref, wo_ref,
                mod_ref, ng_ref, wq_ref, wk_ref, wv_ref, wg_ref,
                kc_ref, ku_ref, kd_ref, qc_ref, qs_ref,
                x1_ref, k_ref, vt_ref, qt_ref, gate_ref, y_ref):
    t = pl.program_id(1)
    n_head = k_ref.shape[-1] // HEAD_W

    for hh in range(n_head):
        lanes = slice(hh * HEAD_W, (hh + 1) * HEAD_W)
        o = of_ref[0, :, lanes].astype(F32) + ob_ref[0, :, lanes].astype(F32)
        o = o * lax.rsqrt(jnp.mean(o * o, axis=-1, keepdims=True) + EPS) * hng_ref[:, lanes]
        y_ref[:, lanes] = (o * gate0_ref[0, :, lanes].astype(F32)).astype(BF16)
    y = jnp.dot(y_ref[...], wo_ref[...], preferred_element_type=F32)
    x1 = jnp.where(t == 0, ctx_ref[0], x_ref[0]) + gm_ref[0, 0] * y

    h = _modulated_norm(x1, ng_ref[...], mod_ref[0, 0, 0:1, :], mod_ref[0, 0, 1:2, :]).astype(BF16)
    k = jnp.dot(h, wk_ref[...], preferred_element_type=F32)
    cos, s_up, s_dn = kc_ref[...], ku_ref[...], kd_ref[...]
    for hh in range(n_head):
        lanes = slice(hh * HEAD_W, (hh + 1) * HEAD_W)
        kh = k[:, lanes]
        rot = (kh * cos + pltpu.roll(kh, HEAD_W - 16, axis=1) * s_up
               + pltpu.roll(kh, 16, axis=1) * s_dn)
        k_ref[0, :, lanes] = rot.astype(BF16)

    vt_ref[0] = lax.dot_general(wv_ref[...], h, NT, preferred_element_type=F32).astype(BF16)

    @pl.when(t > 0)
    def _():
        x1_ref[0] = x1
        gate_ref[0] = _silu(jnp.dot(h, wg_ref[...], preferred_element_type=F32)).astype(BF16)
        qt = lax.dot_general(wq_ref[...], h, NT, preferred_element_type=F32)
        qc, qs = qc_ref[...], qs_ref[...]
        for grp in range(qt.shape[0] // DA_DQK):
            x = qt[grp * DA_DQK:(grp + 1) * DA_DQK]
            partner = jnp.concatenate([x[16:32], x[0:16], x[48:64], x[32:48]], axis=0)
            qt_ref[0, grp * DA_DQK:(grp + 1) * DA_DQK, :] = (x * qc + partner * qs).astype(BF16)


def _mid(of, ob, gate0, ctx, x, gate_mod, hg_norm_g, w_out, mod, norm_g, w_in):
    b, tot, di = of.shape
    l_lat, d = x.shape[1], x.shape[2]
    nt = tot // TOK_TILE
    wq, wk, wv, wg = (w_in[:, j * di:(j + 1) * di] for j in range(4))
    ktabs, (cos64, up64, dn64) = _rope_lane_tables(l_lat)
    q_scale = DA_DQK ** -0.5 * LOG2E
    qc = (cos64.T * q_scale).astype(np.float32)
    qs = ((up64 + dn64).T * q_scale).astype(np.float32)
    lat_t = lambda t: jnp.maximum(t - 1, 0)
    tokw = pl.BlockSpec((1, TOK_TILE, di), lambda i, t: (i, t, 0))
    ktab = pl.BlockSpec((TOK_TILE, HEAD_W), lambda i, t: (t, 0))
    qtab = pl.BlockSpec((DA_DQK, TOK_TILE), lambda i, t: (0, lat_t(t)))
    return pl.pallas_call(
        _mid_kernel,
        grid=(b, nt),
        in_specs=[tokw, tokw, tokw,
                  pl.BlockSpec((1, TOK_TILE, d), lambda i, t: (i, 0, 0)),
                  pl.BlockSpec((1, TOK_TILE, d), lambda i, t: (i, lat_t(t), 0)),
                  pl.BlockSpec((1, 1, 1, d), lambda i, t: (i, jnp.minimum(t, 1), 0, 0)),
                  pl.BlockSpec((1, di), lambda i, t: (0, 0)),
                  _whole_vmem(),
                  pl.BlockSpec((1, 1, 2, d), lambda i, t: (i, jnp.minimum(t, 1), 0, 0)),
                  pl.BlockSpec((1, d), lambda i, t: (0, 0)),
                  _whole_vmem(), _whole_vmem(), _whole_vmem(), _whole_vmem(),
                  ktab, ktab, ktab, qtab, qtab],
        out_specs=[pl.BlockSpec((1, TOK_TILE, d), lambda i, t: (i, lat_t(t), 0)),
                   pl.BlockSpec((1, TOK_TILE, di), lambda i, t: (i, t, 0)),
                   pl.BlockSpec((1, di, TOK_TILE), lambda i, t: (i, 0, t)),
                   pl.BlockSpec((1, di, TOK_TILE), lambda i, t: (i, 0, lat_t(t))),
                   pl.BlockSpec((1, TOK_TILE, di), lambda i, t: (i, lat_t(t), 0))],
        out_shape=[jax.ShapeDtypeStruct((b, l_lat, d), F32),
                   jax.ShapeDtypeStruct((b, tot, di), BF16),
                   jax.ShapeDtypeStruct((b, di, tot), BF16),
                   jax.ShapeDtypeStruct((b, di, l_lat), BF16),
                   jax.ShapeDtypeStruct((b, l_lat, di), BF16)],
        scratch_shapes=[pltpu.VMEM((TOK_TILE, di), BF16)],
        compiler_params=_params(("arbitrary", "arbitrary")),
        name="hg_out_da_proj",
    )(of, ob, gate0, ctx, x, gate_mod, hg_norm_g, w_out, mod, norm_g,
      wq.T.astype(BF16), wk.astype(BF16), wv.T.astype(BF16), wg.astype(BF16),
      *[jnp.asarray(a) for a in ktabs], jnp.asarray(qc), jnp.asarray(qs))


def _da_attn_kernel(k_ref, vt_ref, qt_ref, lam_ref, sg_ref, o_ref, pa_ref, pb_ref, a_ref, la_ref, lb_ref,
                    ota_ref, otb_ref, *, lambda_init):
    lq1, lk1, lq2, lk2 = (lam_ref[i:i + 1, :] for i in range(4))
    lam = (jnp.exp(jnp.sum(lq1 * lk1, axis=-1, keepdims=True))
           - jnp.exp(jnp.sum(lq2 * lk2, axis=-1, keepdims=True)) + lambda_init)
    out_gain = sg_ref[...] * (1.0 - lambda_init)
    n_chunk = k_ref.shape[1] // KEY_CHUNK
    n_q = qt_ref.shape[-1] // Q_TILE
    w2 = 2 * Q_TILE
    half0 = lax.broadcasted_iota(jnp.int32, (HEAD_W, Q_TILE), 0) < DA_DQK

    def head(hh):
        return slice(hh * HEAD_W, (hh + 1) * HEAD_W)

    def key_norm_bound(hh):
        kf = k_ref[0, :, head(hh)].astype(F32)
        sel_r = lax.broadcasted_iota(jnp.int32, (HEAD_W, HEAD_W), 0) // DA_DQK
        sel_c = lax.broadcasted_iota(jnp.int32, (HEAD_W, HEAD_W), 1)
        sel = jnp.where(sel_r == sel_c, 1.0, 0.0).astype(BF16)
        norm2 = jnp.dot((kf * kf).astype(BF16), sel, preferred_element_type=F32)
        kmax = jnp.sqrt(jnp.max(norm2, axis=0, keepdims=True)) * BOUND_SLACK
        return jnp.concatenate([jnp.broadcast_to(kmax[:, 0:1], (1, Q_TILE)),
                                jnp.broadcast_to(kmax[:, 1:2], (1, Q_TILE))], axis=1)

    kmax = [key_norm_bound(hh) for hh in range(ATTN_HEADS)]

    def fold(x):
        return x.reshape(KEY_CHUNK // 8, 8, w2)

    def exp_stage(hh, i, p_ref, l_ref, exact):
        start = i * Q_TILE if isinstance(i, int) else pl.multiple_of(i * Q_TILE, Q_TILE)
        qt = qt_ref[0, head(hh), pl.ds(start, Q_TILE)]
        zero = jnp.zeros_like(qt)
        rhs = jnp.concatenate([jnp.where(half0, qt, zero), jnp.where(half0, zero, qt)], axis=1)

        def scores(c):
            return jnp.dot(k_ref[0, c * KEY_CHUNK:(c + 1) * KEY_CHUNK, head(hh)], rhs,
                           preferred_element_type=F32)

        if exact:
            mx = jnp.full((8, w2), -jnp.inf, F32)
            for c in range(n_chunk):
                mx = jnp.maximum(mx, jnp.max(fold(scores(c)), axis=0))
            m = jnp.max(mx, axis=0, keepdims=True)
        else:
            rf = rhs.astype(F32)
            m = jnp.sqrt(jnp.sum(rf * rf, axis=0, keepdims=True)) * kmax[hh]
        l8 = jnp.zeros((8, w2), F32)
        for c in range(n_chunk):
            p = jnp.exp2(scores(c) - m)
            l8 = l8 + jnp.sum(fold(p), axis=0)
            p_ref[c * KEY_CHUNK:(c + 1) * KEY_CHUNK, :] = p.astype(BF16)
        l_ref[...] = l8
        return jnp.where(l8 > MIN_COLUMN_SUM, 0.0, 1.0)

    def value_stage(hh, p_ref, l_ref, ot_ref):
        l = jnp.sum(l_ref[...], axis=0, keepdims=True)
        l0, l1 = l[:, :Q_TILE], l[:, Q_TILE:]
        rho = (lam * l0 / l1).astype(BF16)
        for c in range(n_chunk):
            rows = slice(c * KEY_CHUNK, (c + 1) * KEY_CHUNK)
            a_ref[rows, :] = p_ref[rows, :Q_TILE] - p_ref[rows, Q_TILE:] * rho
        ot_ref[...] = jnp.dot(vt_ref[0, head(hh), :], a_ref[...], preferred_element_type=F32) * (1.0 / l0)

    def store_stage(hh, i, ot_ref):
        start = i * Q_TILE if isinstance(i, int) else pl.multiple_of(i * Q_TILE, Q_TILE)
        ot = ot_ref[...]
        ot = ot * lax.rsqrt(jnp.mean(ot * ot, axis=0, keepdims=True) + EPS) * out_gain
        o_ref[0, pl.ds(start, Q_TILE), head(hh)] = ot.T.astype(o_ref.dtype)

    even = (pa_ref, la_ref, ota_ref)
    odd = (pb_ref, lb_ref, otb_ref)
    last = n_q - 1
    bad = jnp.zeros((8, w2), F32)
    for hh in range(ATTN_HEADS):
        prev = hh - 1
        bad = jnp.maximum(bad, exp_stage(hh, 0, *even[:2], exact=False))
        if hh > 0:
            value_stage(prev, *odd)
            store_stage(prev, last - 1, even[2])
        bad = jnp.maximum(bad, exp_stage(hh, 1, *odd[:2], exact=False))
        value_stage(hh, *even)
        if hh > 0:
            store_stage(prev, last, odd[2])

        def body(j, bad, hh=hh):
            s = 2 * j + 2
            bad = jnp.maximum(bad, exp_stage(hh, s, *even[:2], exact=False))
            value_stage(hh, *odd)
            store_stage(hh, s - 2, even[2])
            bad = jnp.maximum(bad, exp_stage(hh, s + 1, *odd[:2], exact=False))
            value_stage(hh, *even)
            store_stage(hh, s - 1, odd[2])
            return bad

        bad = lax.fori_loop(0, n_q // 2 - 1, body, bad)
    value_stage(ATTN_HEADS - 1, *odd)
    store_stage(ATTN_HEADS - 1, last - 1, even[2])
    store_stage(ATTN_HEADS - 1, last, odd[2])

    @pl.when(jnp.max(bad) > 0.0)
    def _():
        for hh in range(ATTN_HEADS):
            def redo(i, carry, hh=hh):
                exp_stage(hh, i, *even[:2], exact=True)
                value_stage(hh, *even)
                store_stage(hh, i, even[2])
                return carry

            lax.fori_loop(0, n_q, redo, 0)


def _da_attn(k, vt, qt, lam_vecs, subln_g, lambda_init):
    b, tot, di = k.shape
    l_lat = qt.shape[-1]
    w = ATTN_HEADS * HEAD_W
    assert (l_lat // Q_TILE) % 2 == 0 and di % w == 0
    return pl.pallas_call(
        functools.partial(_da_attn_kernel, lambda_init=lambda_init),
        grid=(b, di // w),
        in_specs=[pl.BlockSpec((1, tot, w), lambda i, h: (i, 0, h)),
                  pl.BlockSpec((1, w, tot), lambda i, h: (i, h, 0)),
                  pl.BlockSpec((1, w, l_lat), lambda i, h: (i, h, 0)),
                  pl.BlockSpec((4, DA_DQK), lambda i, h: (0, 0)),
                  pl.BlockSpec((HEAD_W, 1), lambda i, h: (0, 0))],
        out_specs=pl.BlockSpec((1, l_lat, w), lambda i, h: (i, 0, h)),
        out_shape=jax.ShapeDtypeStruct((b, l_lat, di), BF16),
        scratch_shapes=[pltpu.VMEM((tot, 2 * Q_TILE), BF16),
                        pltpu.VMEM((tot, 2 * Q_TILE), BF16),
                        pltpu.VMEM((tot, Q_TILE), BF16),
                        pltpu.VMEM((8, 2 * Q_TILE), F32),
                        pltpu.VMEM((8, 2 * Q_TILE), F32),
                        pltpu.VMEM((HEAD_W, Q_TILE), F32),
                        pltpu.VMEM((HEAD_W, Q_TILE), F32)],
        compiler_params=_params(("arbitrary", "arbitrary")),
        name="da_attn",
    )(k, vt, qt, lam_vecs, subln_g.reshape(HEAD_W, 1))


def _da_out_kernel(o_ref, gate_ref, x_ref, gm_ref, fg_ref, w_ref, out_ref):
    y = o_ref[0] * gate_ref[0]
    x = x_ref[0] + gm_ref[0] * jnp.dot(y, w_ref[...], preferred_element_type=F32)
    out_ref[0] = x * lax.rsqrt(jnp.mean(x * x, axis=-1, keepdims=True) + EPS) * fg_ref[...]


def _da_out(o, gate, x1, gate_mod, final_g, w_out):
    b, l_lat, di = o.shape
    d = x1.shape[-1]
    tile = math.gcd(l_lat, OUT_TILE)
    tokw = pl.BlockSpec((1, tile, di), lambda i, t: (i, t, 0))
    return pl.pallas_call(
        _da_out_kernel,
        grid=(b, l_lat // tile),
        in_specs=[tokw, tokw,
                  pl.BlockSpec((1, tile, d), lambda i, t: (i, t, 0)),
                  pl.BlockSpec((1, 1, d), lambda i, t: (i, 0, 0)),
                  pl.BlockSpec((1, d), lambda i, t: (0, 0)),
                  _whole_vmem()],
        out_specs=pl.BlockSpec((1, tile, d), lambda i, t: (i, t, 0)),
        out_shape=jax.ShapeDtypeStruct((b, l_lat, d), F32),
        compiler_params=_params(("arbitrary", "arbitrary")),
        name="da_out",
    )(o, gate, x1, gate_mod, final_g, w_out)


def kernel(x, c, ctx, c_ctx, w_ada, b_ada, norm_g, hg_w_in, hg_lb_logits, hg_norm_g, hg_w_out,
           da_w_in, da_lam_q1, da_lam_k1, da_lam_q2, da_lam_k2, da_subln_g, da_w_out, final_g):
    b, l_lat, d = x.shape
    assert ctx.shape[1] == TOK_TILE and l_lat % TOK_TILE == 0 and l_lat % GRID_W == 0
    assert w_ada.shape[0] == 2 and hg_w_in.shape[0] == 1 and da_w_in.shape[0] == 1
    assert hg_lb_logits.shape[0] == 2

    rows = -(-(b + 1) // 8) * 8
    cc = jnp.concatenate([c, c_ctx[None], jnp.zeros((rows - b - 1, d), F32)], axis=0)
    ada = _adaln(cc, w_ada, b_ada)

    def mods(layer):
        m = ada[layer].reshape(rows, 3, d)
        pair = jnp.stack([jnp.broadcast_to(m[b], (b, 3, d)), m[:b]], axis=1)
        return pair[:, :, 0:2], pair[:, :, 2:3]

    mod0, gmod0 = mods(0)
    mod1, gmod1 = mods(1)

    q, gf, gb, v, gate0 = _hg_proj(ctx, x, mod0, norm_g[0:1], hg_w_in[0].astype(BF16), hg_lb_logits)
    of, ob = _hg_scan(q, gf, gb, v)
    x1, k, vt, qt, gate1 = _mid(of, ob, gate0, ctx, x, gmod0, hg_norm_g, hg_w_out[0].astype(BF16),
                                mod1, norm_g[1:2], da_w_in[0])

    lambda_init = 0.8 - 0.6 * math.exp(-0.3 * 1)
    lam_vecs = jnp.concatenate([da_lam_q1, da_lam_k1, da_lam_q2, da_lam_k2], axis=0)
    o = _da_attn(k, vt, qt, lam_vecs, da_subln_g[0], lambda_init)
    return _da_out(o, gate1, x1, gmod1[:, 1], final_g[None], da_w_out[0].astype(BF16))
```

```python
import functools
import math

import numpy as np
import jax
import jax.numpy as jnp
from jax import lax
from jax.experimental import pallas as pl
from jax.experimental.pallas import tpu as pltpu

F32 = jnp.float32
BF16 = jnp.bfloat16

EPS = 1e-6
GRID_W = 64
ROPE_THETA = 10000.0
HEAD_W = 128
DA_DQK = 64
TOK_TILE = 256
OUT_TILE = 1024
SCAN_CHUNK = 64
SCAN_HEADS = 16
Q_TILE = 256
KEY_CHUNK = 256
ATTN_HEADS = 2
BOUND_SLACK = 1.01
MIN_COLUMN_SUM = 2.0 ** -64
MAX_HALF_CHUNK_LOG2_DECAY = 100.0
SCAN_EXACT_CHUNK = 16
LOG2E = 1.4426950408889634
VMEM_LIMIT = 56 * 1024 * 1024

NT = (((1,), (1,)), ((), ()))
TN = (((0,), (0,)), ((), ()))


def _sigmoid(x):
    return 1.0 / (1.0 + jnp.exp(-x))


def _silu(x):
    return x * _sigmoid(x)


def _params(sem):
    return pltpu.CompilerParams(dimension_semantics=sem, vmem_limit_bytes=VMEM_LIMIT)


def _whole_vmem():
    return pl.BlockSpec(memory_space=pltpu.VMEM)


def _adaln_kernel(c_ref, w_ref, b_ref, o_ref):
    s = _silu(c_ref[...]).astype(BF16)
    o_ref[0] = jnp.dot(s, w_ref[0], preferred_element_type=F32) + b_ref[0]


def _adaln(cc, w_ada, b_ada):
    depth, d, d3 = w_ada.shape
    rows = cc.shape[0]
    return pl.pallas_call(
        _adaln_kernel,
        grid=(depth,),
        in_specs=[pl.BlockSpec((rows, d), lambda i: (0, 0)),
                  pl.BlockSpec((1, d, d3), lambda i: (i, 0, 0)),
                  pl.BlockSpec((1, 1, d3), lambda i: (i, 0, 0))],
        out_specs=pl.BlockSpec((1, rows, d3), lambda i: (i, 0, 0)),
        out_shape=jax.ShapeDtypeStruct((depth, rows, d3), F32),
        compiler_params=_params(("arbitrary",)),
        name="adaln",
    )(cc, w_ada.astype(BF16), b_ada.reshape(depth, 1, d3))


def _modulated_norm(x, g, shift, scale):
    y = x * lax.rsqrt(jnp.mean(x * x, axis=-1, keepdims=True) + EPS)
    return (y * g) * (1.0 + scale) + shift


def _hg_proj_kernel(ctx_ref, x_ref, mod_ref, ng_ref, w_ref, lbl_ref,
                    q_ref, gf_ref, gb_ref, v_ref, gate_ref):
    t = pl.program_id(1)
    x = jnp.where(t == 0, ctx_ref[0], x_ref[0])
    h = _modulated_norm(x, ng_ref[...], mod_ref[0, 0, 0:1, :], mod_ref[0, 0, 1:2, :]).astype(BF16)
    di = q_ref.shape[-1]

    def seg(j):
        return jnp.dot(h, w_ref[:, j * di:(j + 1) * di], preferred_element_type=F32)

    def log_forget(z, d):
        l0 = lbl_ref[0, d:d + 1, :]
        l1 = lbl_ref[1, d:d + 1, :]
        m = jnp.maximum(l0, l1)
        e0 = jnp.exp(l0 - m)
        lb = e0 / (e0 + jnp.exp(l1 - m))
        return jnp.log2(lb + (1.0 - lb) * _sigmoid(z))

    gf_ref[0] = log_forget(seg(1), 0).astype(BF16)
    gb_ref[0] = log_forget(seg(2), 1).astype(BF16)
    q_ref[0] = _silu(seg(0)).astype(BF16)
    gate_ref[0] = _silu(seg(4)).astype(BF16)
    v_ref[0] = seg(3).astype(BF16)


def _hg_proj(ctx, x, mod, norm_g, w_in, lb_logits):
    b, l, d = x.shape
    di = w_in.shape[1] // 5
    nt = l // TOK_TILE + 1
    tot = l + TOK_TILE
    tok = lambda dt: jax.ShapeDtypeStruct((b, tot, di), dt)
    out_blk = pl.BlockSpec((1, TOK_TILE, di), lambda i, t: (i, t, 0))
    return pl.pallas_call(
        _hg_proj_kernel,
        grid=(b, nt),
        in_specs=[pl.BlockSpec((1, TOK_TILE, d), lambda i, t: (i, 0, 0)),
                  pl.BlockSpec((1, TOK_TILE, d), lambda i, t: (i, jnp.maximum(t - 1, 0), 0)),
                  pl.BlockSpec((1, 1, 2, d), lambda i, t: (i, jnp.minimum(t, 1), 0, 0)),
                  pl.BlockSpec((1, d), lambda i, t: (0, 0)),
                  _whole_vmem(),
                  pl.BlockSpec((2, 2, di), lambda i, t: (0, 0, 0))],
        out_specs=[out_blk] * 5,
        out_shape=[tok(BF16)] * 5,
        compiler_params=_params(("arbitrary", "arbitrary")),
        name="hg_proj",
    )(ctx, x, mod, norm_g, w_in, lb_logits)


def _hg_scan_kernel(qf_ref, gf_ref, vf_ref, qb_ref, gb_ref, vb_ref, of_ref, ob_ref,
                    s_ref, bf_ref, bb_ref, tri_ref):
    t = pl.program_id(2)
    c = SCAN_CHUNK
    n_chunk = TOK_TILE // c
    w = SCAN_HEADS * HEAD_W

    @pl.when((pl.program_id(0) == 0) & (pl.program_id(1) == 0) & (t == 0))
    def _():
        row = lax.broadcasted_iota(jnp.int32, (TOK_TILE, TOK_TILE), 0)
        col = lax.broadcasted_iota(jnp.int32, (TOK_TILE, TOK_TILE), 1)
        same = (row // c) == (col // c)
        tri_ref[0] = jnp.where(same & (col <= row), 1.0, 0.0).astype(BF16)
        tri_ref[1] = jnp.where(same & (col >= row), 1.0, 0.0).astype(BF16)

    @pl.when(t == 0)
    def _():
        s_ref[...] = jnp.zeros_like(s_ref)

    def cumsum(tri, g):
        return jnp.dot(tri, g, preferred_element_type=F32)

    bf_ref[...] = cumsum(tri_ref[0], gf_ref[0])
    bb_ref[...] = cumsum(tri_ref[1], gb_ref[0])

    def chunk_rows(b_ref, row):
        return jnp.concatenate([b_ref[j * c + row:j * c + row + 1, :] for j in range(n_chunk)], axis=0)

    def chunk_decay_columns(total):
        return jnp.exp2(jnp.concatenate([total, jnp.zeros((8 - n_chunk, w), F32)], axis=0).T)

    mid_f, mid_b = c // 2 - 1, c // 2
    tot_f, tot_b = chunk_rows(bf_ref, c - 1), chunk_rows(bb_ref, 0)
    ref_f, ref_b = chunk_rows(bf_ref, mid_f), chunk_rows(bb_ref, mid_b)
    dec_f = chunk_decay_columns(tot_f)
    dec_b = chunk_decay_columns(tot_b)

    ri = lax.broadcasted_iota(jnp.int32, (c, c), 0)
    ci = lax.broadcasted_iota(jnp.int32, (c, c), 1)
    keep_f = ci <= ri
    keep_b = ci >= ri

    def chunk_head(direction, j, hh, q_ref, g_ref, v_ref, b_ref, o_ref, dec, keep, mid_row, last_row):
        rows = slice(j * c, (j + 1) * c)
        lanes = slice(hh * HEAD_W, (hh + 1) * HEAD_W)
        q = q_ref[0, rows, lanes]
        g = g_ref[0, rows, lanes].astype(F32)
        v = v_ref[0, rows, lanes]
        bcum = b_ref[rows, lanes]
        r = bcum[mid_row:mid_row + 1, :]
        b_last = bcum[last_row:last_row + 1, :]
        d = bcum - r
        q_mid = q * jnp.exp2(d).astype(BF16)
        k_mid = ((1.0 - jnp.exp2(g)) * jnp.exp2(-d)).astype(BF16)
        q_dec = q_mid * jnp.exp2(r).astype(BF16)
        k_dec = k_mid * jnp.exp2(b_last - r).astype(BF16)
        a = lax.dot_general(q_mid, k_mid, NT, preferred_element_type=F32)
        a = jnp.where(keep, a, 0.0).astype(BF16)
        s = s_ref[direction, hh]
        o_ref[0, rows, lanes] = jnp.dot(jnp.concatenate([q_dec, a], axis=1),
                                        jnp.concatenate([s.astype(BF16), v], axis=0),
                                        preferred_element_type=F32).astype(o_ref.dtype)
        s_ref[direction, hh] = s * dec[lanes, j:j + 1] + lax.dot_general(
            k_dec, v, TN, preferred_element_type=F32)

    reach = jnp.minimum(jnp.minimum(ref_f, tot_f - ref_f), jnp.minimum(ref_b, tot_b - ref_b))
    mild = jnp.min(reach) > -MAX_HALF_CHUNK_LOG2_DECAY

    @pl.when(mild)
    def _():
        for i in range(n_chunk):
            for hh in range(SCAN_HEADS):
                chunk_head(0, i, hh, qf_ref, gf_ref, vf_ref, bf_ref, of_ref, dec_f, keep_f, mid_f, c - 1)
                chunk_head(1, n_chunk - 1 - i, hh, qb_ref, gb_ref, vb_ref, bb_ref, ob_ref, dec_b, keep_b, mid_b, 0)

    @pl.when(jnp.logical_not(mild))
    def _():
        _hg_scan_exact_tile((qf_ref, gf_ref, vf_ref, of_ref), (qb_ref, gb_ref, vb_ref, ob_ref), s_ref)


def _hg_scan_exact_tile(fwd_refs, bwd_refs, s_ref):
    sub = SCAN_EXACT_CHUNK
    n_sub = TOK_TILE // sub
    ri = lax.broadcasted_iota(jnp.int32, (sub, sub), 0)
    ci = lax.broadcasted_iota(jnp.int32, (sub, sub), 1)
    tri = (jnp.where(ci <= ri, 1.0, 0.0).astype(BF16), jnp.where(ci >= ri, 1.0, 0.0).astype(BF16))
    jcol = lax.broadcasted_iota(jnp.int32, (sub, 1), 0)

    def head_body(hh, carry):
        lanes = pl.ds(pl.multiple_of(hh * HEAD_W, HEAD_W), HEAD_W)

        def sub_body(step, carry):
            for direction, (q_ref, g_ref, v_ref, o_ref) in enumerate((fwd_refs, bwd_refs)):
                fwd = direction == 0
                cc = step if fwd else n_sub - 1 - step
                rows = pl.ds(pl.multiple_of(cc * sub, sub), sub)
                q = q_ref[0, rows, lanes].astype(F32)
                g = g_ref[0, rows, lanes]
                v = v_ref[0, rows, lanes]
                b = jnp.dot(tri[direction], g, preferred_element_type=F32)
                k = 1.0 - jnp.exp2(g.astype(F32))
                b_last = b[sub - 1:sub, :] if fwd else b[0:1, :]
                s = s_ref[direction, hh]
                o = jnp.dot((q * jnp.exp2(b)).astype(BF16), s.astype(BF16), preferred_element_type=F32)
                vf = v.astype(F32)
                intra = []
                for i in range(sub):
                    e = jnp.exp2(jnp.minimum(b[i:i + 1, :] - b, 0.0))
                    a = jnp.sum(q[i:i + 1, :] * e * k, axis=1, keepdims=True)
                    a = jnp.where((jcol <= i) if fwd else (jcol >= i), a, 0.0)
                    intra.append(jnp.sum(a * vf, axis=0, keepdims=True))
                o_ref[0, rows, lanes] = (o + jnp.concatenate(intra, axis=0)).astype(o_ref.dtype)
                k_dec = (k * jnp.exp2(b_last - b)).astype(BF16)
                dec_col = jnp.broadcast_to(jnp.exp2(b_last), (8, HEAD_W)).T[:, 0:1]
                s_ref[direction, hh] = s * dec_col + lax.dot_general(k_dec, v, TN, preferred_element_type=F32)
            return carry

        return lax.fori_loop(0, n_sub, sub_body, carry)

    lax.fori_loop(0, SCAN_HEADS, head_body, 0)


def _hg_scan(q, gf, gb, v):
    b, tot, di = q.shape
    nt = tot // TOK_TILE
    w = SCAN_HEADS * HEAD_W
    fwd = pl.BlockSpec((1, TOK_TILE, w), lambda i, h, t: (i, t, h))
    bwd = pl.BlockSpec((1, TOK_TILE, w), lambda i, h, t: (i, jnp.where(t == 0, 0, nt - t), h))
    out = jax.ShapeDtypeStruct((b, tot, di), BF16)
    return pl.pallas_call(
        _hg_scan_kernel,
        grid=(b, di // w, nt),
        in_specs=[fwd, fwd, fwd, bwd, bwd, bwd],
        out_specs=[fwd, bwd],
        out_shape=[out, out],
        scratch_shapes=[pltpu.VMEM((2, SCAN_HEADS, HEAD_W, HEAD_W), F32),
                        pltpu.VMEM((TOK_TILE, w), F32),
                        pltpu.VMEM((TOK_TILE, w), F32),
                        pltpu.VMEM((2, TOK_TILE, TOK_TILE), BF16)],
        compiler_params=_params(("arbitrary", "arbitrary", "arbitrary")),
        name="hg_scan",
    )(q, gf, v, q, gb, v)


def _rope_lane_tables(l_lat):
    ax = DA_DQK // 2
    inv = 1.0 / (ROPE_THETA ** (np.arange(0, ax, 2, dtype=np.float64) / ax))
    pos = np.arange(l_lat)
    ang_r = (pos // GRID_W)[:, None] * inv
    ang_c = (pos % GRID_W)[:, None] * inv
    zero = np.zeros_like(ang_r)
    cos64 = np.concatenate([np.cos(ang_r), np.cos(ang_r), np.cos(ang_c), np.cos(ang_c)], axis=1)
    up64 = np.concatenate([-np.sin(ang_r), zero, -np.sin(ang_c), zero], axis=1)
    dn64 = np.concatenate([zero, np.sin(ang_r), zero, np.sin(ang_c)], axis=1)
    lat = [np.concatenate([t, t], axis=1) for t in (cos64, up64, dn64)]
    ident = [np.ones((TOK_TILE, HEAD_W)), np.zeros((TOK_TILE, HEAD_W)), np.zeros((TOK_TILE, HEAD_W))]
    return [np.concatenate([i, t], axis=0).astype(np.float32) for i, t in zip(ident, lat)], (cos64, up64, dn64)


def _mid_kernel(of_ref, ob_ref, gate0_ref, ctx_ref, x_ref, gm_ref, hng_ref, wo_ref,
                mod_ref, ng_ref, wq_ref, wk_ref, wv_ref, wg_ref,
                kc_ref, ku_ref, kd_ref, qc_ref, qs_ref,
                x1_ref, k_ref, vt_ref, qt_ref, gate_ref, y_ref):
    t = pl.program_id(1)
    n_head = k_ref.shape[-1] // HEAD_W

    for hh in range(n_head):
        lanes = slice(hh * HEAD_W, (hh + 1) * HEAD_W)
        o = of_ref[0, :, lanes].astype(F32) + ob_ref[0, :, lanes].astype(F32)
        o = o * lax.rsqrt(jnp.mean(o * o, axis=-1, keepdims=True) + EPS) * hng_ref[:, lanes]
        y_ref[:, lanes] = (o * gate0_ref[0, :, lanes].astype(F32)).astype(BF16)
    y = jnp.dot(y_ref[...], wo_ref[...], preferred_element_type=F32)
    x1 = jnp.where(t == 0, ctx_ref[0], x_ref[0]) + gm_ref[0, 0] * y

    h = _modulated_norm(x1, ng_ref[...], mod_ref[0, 0, 0:1, :], mod_ref[0, 0, 1:2, :]).astype(BF16)
    k = jnp.dot(h, wk_ref[...], preferred_element_type=F32)
    cos, s_up, s_dn = kc_ref[...], ku_ref[...], kd_ref[...]
    for hh in range(n_head):
        lanes = slice(hh * HEAD_W, (hh + 1) * HEAD_W)
        kh = k[:, lanes]
        rot = (kh * cos + pltpu.roll(kh, HEAD_W - 16, axis=1) * s_up
               + pltpu.roll(kh, 16, axis=1) * s_dn)
        k_ref[0, :, lanes] = rot.astype(BF16)

    vt_ref[0] = lax.dot_general(wv_ref[...], h, NT, preferred_element_type=F32).astype(BF16)

    @pl.when(t > 0)
    def _():
        x1_ref[0] = x1
        gate_ref[0] = _silu(jnp.dot(h, wg_ref[...], preferred_element_type=F32)).astype(BF16)
        qt = lax.dot_general(wq_ref[...], h, NT, preferred_element_type=F32)
        qc, qs = qc_ref[...], qs_ref[...]
        for grp in range(qt.shape[0] // DA_DQK):
            x = qt[grp * DA_DQK:(grp + 1) * DA_DQK]
            partner = jnp.concatenate([x[16:32], x[0:16], x[48:64], x[32:48]], axis=0)
            qt_ref[0, grp * DA_DQK:(grp + 1) * DA_DQK, :] = (x * qc + partner * qs).astype(BF16)


def _mid(of, ob, gate0, ctx, x, gate_mod, hg_norm_g, w_out, mod, norm_g, w_in):
    b, tot, di = of.shape
    l_lat, d = x.shape[1], x.shape[2]
    nt = tot // TOK_TILE
    wq, wk, wv, wg = (w_in[:, j * di:(j + 1) * di] for j in range(4))
    ktabs, (cos64, up64, dn64) = _rope_lane_tables(l_lat)
    q_scale = DA_DQK ** -0.5 * LOG2E
    qc = (cos64.T * q_scale).astype(np.float32)
    qs = ((up64 + dn64).T * q_scale).astype(np.float32)
    lat_t = lambda t: jnp.maximum(t - 1, 0)
    tokw = pl.BlockSpec((1, TOK_TILE, di), lambda i, t: (i, t, 0))
    ktab = pl.BlockSpec((TOK_TILE, HEAD_W), lambda i, t: (t, 0))
    qtab = pl.BlockSpec((DA_DQK, TOK_TILE), lambda i, t: (0, lat_t(t)))
    return pl.pallas_call(
        _mid_kernel,
        grid=(b, nt),
        in_specs=[tokw, tokw, tokw,
                  pl.BlockSpec((1, TOK_TILE, d), lambda i, t: (i, 0, 0)),
                  pl.BlockSpec((1, TOK_TILE, d), lambda i, t: (i, lat_t(t), 0)),
                  pl.BlockSpec((1, 1, 1, d), lambda i, t: (i, jnp.minimum(t, 1), 0, 0)),
                  pl.BlockSpec((1, di), lambda i, t: (0, 0)),
                  _whole_vmem(),
                  pl.BlockSpec((1, 1, 2, d), lambda i, t: (i, jnp.minimum(t, 1), 0, 0)),
                  pl.BlockSpec((1, d), lambda i, t: (0, 0)),
                  _whole_vmem(), _whole_vmem(), _whole_vmem(), _whole_vmem(),
                  ktab, ktab, ktab, qtab, qtab],
        out_specs=[pl.BlockSpec((1, TOK_TILE, d), lambda i, t: (i, lat_t(t), 0)),
                   pl.BlockSpec((1, TOK_TILE, di), lambda i, t: (i, t, 0)),
                   pl.BlockSpec((1, di, TOK_TILE), lambda i, t: (i, 0, t)),
                   pl.BlockSpec((1, di, TOK_TILE), lambda i, t: (i, 0, lat_t(t))),
                   pl.BlockSpec((1, TOK_TILE, di), lambda i, t: (i, lat_t(t), 0))],
        out_shape=[jax.ShapeDtypeStruct((b, l_lat, d), F32),
                   jax.ShapeDtypeStruct((b, tot, di), BF16),
                   jax.ShapeDtypeStruct((b, di, tot), BF16),
                   jax.ShapeDtypeStruct((b, di, l_lat), BF16),
                   jax.ShapeDtypeStruct((b, l_lat, di), BF16)],
        scratch_shapes=[pltpu.VMEM((TOK_TILE, di), BF16)],
        compiler_params=_params(("arbitrary", "arbitrary")),
        name="hg_out_da_proj",
    )(of, ob, gate0, ctx, x, gate_mod, hg_norm_g, w_out, mod, norm_g,
      wq.T.astype(BF16), wk.astype(BF16), wv.T.astype(BF16), wg.astype(BF16),
      *[jnp.asarray(a) for a in ktabs], jnp.asarray(qc), jnp.asarray(qs))


def _da_attn_kernel(k_ref, vt_ref, qt_ref, lam_ref, sg_ref, o_ref, pa_ref, pb_ref, a_ref, la_ref, lb_ref,
                    ota_ref, otb_ref, *, lambda_init):
    lq1, lk1, lq2, lk2 = (lam_ref[i:i + 1, :] for i in range(4))
    lam = (jnp.exp(jnp.sum(lq1 * lk1, axis=-1, keepdims=True))
           - jnp.exp(jnp.sum(lq2 * lk2, axis=-1, keepdims=True)) + lambda_init)
    out_gain = sg_ref[...] * (1.0 - lambda_init)
    n_chunk = k_ref.shape[1] // KEY_CHUNK
    n_q = qt_ref.shape[-1] // Q_TILE
    w2 = 2 * Q_TILE
    half0 = lax.broadcasted_iota(jnp.int32, (HEAD_W, Q_TILE), 0) < DA_DQK

    def head(hh):
        return slice(hh * HEAD_W, (hh + 1) * HEAD_W)

    def key_norm_bound(hh):
        kf = k_ref[0, :, head(hh)].astype(F32)
        sel_r = lax.broadcasted_iota(jnp.int32, (HEAD_W, HEAD_W), 0) // DA_DQK
        sel_c = lax.broadcasted_iota(jnp.int32, (HEAD_W, HEAD_W), 1)
        sel = jnp.where(sel_r == sel_c, 1.0, 0.0).astype(BF16)
        norm2 = jnp.dot((kf * kf).astype(BF16), sel, preferred_element_type=F32)
        kmax = jnp.sqrt(jnp.max(norm2, axis=0, keepdims=True)) * BOUND_SLACK
        return jnp.concatenate([jnp.broadcast_to(kmax[:, 0:1], (1, Q_TILE)),
                                jnp.broadcast_to(kmax[:, 1:2], (1, Q_TILE))], axis=1)

    kmax = [key_norm_bound(hh) for hh in range(ATTN_HEADS)]

    def fold(x):
        return x.reshape(KEY_CHUNK // 8, 8, w2)

    def exp_stage(hh, i, p_ref, l_ref, exact):
        start = i * Q_TILE if isinstance(i, int) else pl.multiple_of(i * Q_TILE, Q_TILE)
        qt = qt_ref[0, head(hh), pl.ds(start, Q_TILE)]
        zero = jnp.zeros_like(qt)
        rhs = jnp.concatenate([jnp.where(half0, qt, zero), jnp.where(half0, zero, qt)], axis=1)

        def scores(c):
            return jnp.dot(k_ref[0, c * KEY_CHUNK:(c + 1) * KEY_CHUNK, head(hh)], rhs,
                           preferred_element_type=F32)

        if exact:
            mx = jnp.full((8, w2), -jnp.inf, F32)
            for c in range(n_chunk):
                mx = jnp.maximum(mx, jnp.max(fold(scores(c)), axis=0))
            m = jnp.max(mx, axis=0, keepdims=True)
        else:
            rf = rhs.astype(F32)
            m = jnp.sqrt(jnp.sum(rf * rf, axis=0, keepdims=True)) * kmax[hh]
        l8 = jnp.zeros((8, w2), F32)
        for c in range(n_chunk):
            p = jnp.exp2(scores(c) - m)
            l8 = l8 + jnp.sum(fold(p), axis=0)
            p_ref[c * KEY_CHUNK:(c + 1) * KEY_CHUNK, :] = p.astype(BF16)
        l_ref[...] = l8
        return jnp.where(l8 > MIN_COLUMN_SUM, 0.0, 1.0)

    def value_stage(hh, p_ref, l_ref, ot_ref):
        l = jnp.sum(l_ref[...], axis=0, keepdims=True)
        l0, l1 = l[:, :Q_TILE], l[:, Q_TILE:]
        rho = (lam * l0 / l1).astype(BF16)
        for c in range(n_chunk):
            rows = slice(c * KEY_CHUNK, (c + 1) * KEY_CHUNK)
            a_ref[rows, :] = p_ref[rows, :Q_TILE] - p_ref[rows, Q_TILE:] * rho
        ot_ref[...] = jnp.dot(vt_ref[0, head(hh), :], a_ref[...], preferred_element_type=F32) * (1.0 / l0)

    def store_stage(hh, i, ot_ref):
        start = i * Q_TILE if isinstance(i, int) else pl.multiple_of(i * Q_TILE, Q_TILE)
        ot = ot_ref[...]
        ot = ot * lax.rsqrt(jnp.mean(ot * ot, axis=0, keepdims=True) + EPS) * out_gain
        o_ref[0, pl.ds(start, Q_TILE), head(hh)] = ot.T.astype(o_ref.dtype)

    even = (pa_ref, la_ref, ota_ref)
    odd = (pb_ref, lb_ref, otb_ref)
    last = n_q - 1
    bad = jnp.zeros((8, w2), F32)
    for hh in range(ATTN_HEADS):
        prev = hh - 1
        bad = jnp.maximum(bad, exp_stage(hh, 0, *even[:2], exact=False))
        if hh > 0:
            value_stage(prev, *odd)
            store_stage(prev, last - 1, even[2])
        bad = jnp.maximum(bad, exp_stage(hh, 1, *odd[:2], exact=False))
        value_stage(hh, *even)
        if hh > 0:
            store_stage(prev, last, odd[2])

        def body(j, bad, hh=hh):
            s = 2 * j + 2
            bad = jnp.maximum(bad, exp_stage(hh, s, *even[:2], exact=False))
            value_stage(hh, *odd)
            store_stage(hh, s - 2, even[2])
            bad = jnp.maximum(bad, exp_stage(hh, s + 1, *odd[:2], exact=False))
            value_stage(hh, *even)
            store_stage(hh, s - 1, odd[2])
            return bad

        bad = lax.fori_loop(0, n_q // 2 - 1, body, bad)
    value_stage(ATTN_HEADS - 1, *odd)
    store_stage(ATTN_HEADS - 1, last - 1, even[2])
    store_stage(ATTN_HEADS - 1, last, odd[2])

    @pl.when(jnp.max(bad) > 0.0)
    def _():
        for hh in range(ATTN_HEADS):
            def redo(i, carry, hh=hh):
                exp_stage(hh, i, *even[:2], exact=True)
                value_stage(hh, *even)
                store_stage(hh, i, even[2])
                return carry

            lax.fori_loop(0, n_q, redo, 0)


def _da_attn(k, vt, qt, lam_vecs, subln_g, lambda_init):
    b, tot, di = k.shape
    l_lat = qt.shape[-1]
    w = ATTN_HEADS * HEAD_W
    assert (l_lat // Q_TILE) % 2 == 0 and di % w == 0
    return pl.pallas_call(
        functools.partial(_da_attn_kernel, lambda_init=lambda_init),
        grid=(b, di // w),
        in_specs=[pl.BlockSpec((1, tot, w), lambda i, h: (i, 0, h)),
                  pl.BlockSpec((1, w, tot), lambda i, h: (i, h, 0)),
                  pl.BlockSpec((1, w, l_lat), lambda i, h: (i, h, 0)),
                  pl.BlockSpec((4, DA_DQK), lambda i, h: (0, 0)),
                  pl.BlockSpec((HEAD_W, 1), lambda i, h: (0, 0))],
        out_specs=pl.BlockSpec((1, l_lat, w), lambda i, h: (i, 0, h)),
        out_shape=jax.ShapeDtypeStruct((b, l_lat, di), BF16),
        scratch_shapes=[pltpu.VMEM((tot, 2 * Q_TILE), BF16),
                        pltpu.VMEM((tot, 2 * Q_TILE), BF16),
                        pltpu.VMEM((tot, Q_TILE), BF16),
                        pltpu.VMEM((8, 2 * Q_TILE), F32),
                        pltpu.VMEM((8, 2 * Q_TILE), F32),
                        pltpu.VMEM((HEAD_W, Q_TILE), F32),
                        pltpu.VMEM((HEAD_W, Q_TILE), F32)],
        compiler_params=_params(("arbitrary", "arbitrary")),
        name="da_attn",
    )(k, vt, qt, lam_vecs, subln_g.reshape(HEAD_W, 1))


def _da_out_kernel(o_ref, gate_ref, x_ref, gm_ref, fg_ref, w_ref, out_ref):
    y = o_ref[0] * gate_ref[0]
    x = x_ref[0] + gm_ref[0] * jnp.dot(y, w_ref[...], preferred_element_type=F32)
    out_ref[0] = x * lax.rsqrt(jnp.mean(x * x, axis=-1, keepdims=True) + EPS) * fg_ref[...]


def _da_out(o, gate, x1, gate_mod, final_g, w_out):
    b, l_lat, di = o.shape
    d = x1.shape[-1]
    tile = math.gcd(l_lat, OUT_TILE)
    tokw = pl.BlockSpec((1, tile, di), lambda i, t: (i, t, 0))
    return pl.pallas_call(
        _da_out_kernel,
        grid=(b, l_lat // tile),
        in_specs=[tokw, tokw,
                  pl.BlockSpec((1, tile, d), lambda i, t: (i, t, 0)),
                  pl.BlockSpec((1, 1, d), lambda i, t: (i, 0, 0)),
                  pl.BlockSpec((1, d), lambda i, t: (0, 0)),
                  _whole_vmem()],
        out_specs=pl.BlockSpec((1, tile, d), lambda i, t: (i, t, 0)),
        out_shape=jax.ShapeDtypeStruct((b, l_lat, d), F32),
        compiler_params=_params(("arbitrary", "arbitrary")),
        name="da_out",
    )(o, gate, x1, gate_mod, final_g, w_out)


def kernel(x, c, ctx, c_ctx, w_ada, b_ada, norm_g, hg_w_in, hg_lb_logits, hg_norm_g, hg_w_out,
           da_w_in, da_lam_q1, da_lam_k1, da_lam_q2, da_lam_k2, da_subln_g, da_w_out, final_g):
    b, l_lat, d = x.shape
    assert ctx.shape[1] == TOK_TILE and l_lat % TOK_TILE == 0 and l_lat % GRID_W == 0
    assert w_ada.shape[0] == 2 and hg_w_in.shape[0] == 1 and da_w_in.shape[0] == 1
    assert hg_lb_logits.shape[0] == 2

    rows = -(-(b + 1) // 8) * 8
    cc = jnp.concatenate([c, c_ctx[None], jnp.zeros((rows - b - 1, d), F32)], axis=0)
    ada = _adaln(cc, w_ada, b_ada)

    def mods(layer):
        m = ada[layer].reshape(rows, 3, d)
        pair = jnp.stack([jnp.broadcast_to(m[b], (b, 3, d)), m[:b]], axis=1)
        return pair[:, :, 0:2], pair[:, :, 2:3]

    mod0, gmod0 = mods(0)
    mod1, gmod1 = mods(1)

    q, gf, gb, v, gate0 = _hg_proj(ctx, x, mod0, norm_g[0:1], hg_w_in[0].astype(BF16), hg_lb_logits)
    of, ob = _hg_scan(q, gf, gb, v)
    x1, k, vt, qt, gate1 = _mid(of, ob, gate0, ctx, x, gmod0, hg_norm_g, hg_w_out[0].astype(BF16),
                                mod1, norm_g[1:2], da_w_in[0])

    lambda_init = 0.8 - 0.6 * math.exp(-0.3 * 1)
    lam_vecs = jnp.concatenate([da_lam_q1, da_lam_k1, da_lam_q2, da_lam_k2], axis=0)
    o = _da_attn(k, vt, qt, lam_vecs, da_subln_g[0], lambda_init)
    return _da_out(o, gate1, x1, gmod1[:, 1], final_g[None], da_w_out[0].astype(BF16))
```

```python
import functools
import math

import numpy as np
import jax
import jax.numpy as jnp
from jax import lax
from jax.experimental import pallas as pl
from jax.experimental.pallas import tpu as pltpu

F32 = jnp.float32
BF16 = jnp.bfloat16

EPS = 1e-6
GRID_W = 64
ROPE_THETA = 10000.0
HEAD_W = 128
DA_DQK = 64
TOK_TILE = 256
OUT_TILE = 1024
SCAN_CHUNK = 64
SCAN_HEADS = 16
Q_TILE = 256
KEY_CHUNK = 256
ATTN_HEADS = 2
BOUND_SLACK = 1.01
MIN_COLUMN_SUM = 2.0 ** -64
MAX_HALF_CHUNK_LOG2_DECAY = 100.0
SCAN_EXACT_CHUNK = 16
LOG2E = 1.4426950408889634
VMEM_LIMIT = 56 * 1024 * 1024

NT = (((1,), (1,)), ((), ()))
TN = (((0,), (0,)), ((), ()))


def _sigmoid(x):
    return 1.0 / (1.0 + jnp.exp(-x))


def _silu(x):
    return x * _sigmoid(x)


def _params(sem):
    return pltpu.CompilerParams(dimension_semantics=sem, vmem_limit_bytes=VMEM_LIMIT)


def _whole_vmem():
    return pl.BlockSpec(memory_space=pltpu.VMEM)


def _adaln_kernel(c_ref, w_ref, b_ref, o_ref):
    s = _silu(c_ref[...]).astype(BF16)
    o_ref[0] = jnp.dot(s, w_ref[0], preferred_element_type=F32) + b_ref[0]


def _adaln(cc, w_ada, b_ada):
    depth, d, d3 = w_ada.shape
    rows = cc.shape[0]
    return pl.pallas_call(
        _adaln_kernel,
        grid=(depth,),
        in_specs=[pl.BlockSpec((rows, d), lambda i: (0, 0)),
                  pl.BlockSpec((1, d, d3), lambda i: (i, 0, 0)),
                  pl.BlockSpec((1, 1, d3), lambda i: (i, 0, 0))],
        out_specs=pl.BlockSpec((1, rows, d3), lambda i: (i, 0, 0)),
        out_shape=jax.ShapeDtypeStruct((depth, rows, d3), F32),
        compiler_params=_params(("arbitrary",)),
        name="adaln",
    )(cc, w_ada.astype(BF16), b_ada.reshape(depth, 1, d3))


def _modulated_norm(x, g, shift, scale):
    y = x * lax.rsqrt(jnp.mean(x * x, axis=-1, keepdims=True) + EPS)
    return (y * g) * (1.0 + scale) + shift


def _hg_proj_kernel(ctx_a, x_a, mod_a, ctx_b, x_b, mod_b, ng_ref, w_ref, lbl_ref,
                    q_ref, gf_ref, gb_ref, v_ref, gate_ref, *, nt):
    di = q_ref.shape[-1]

    def log_forget(z, d):
        l0 = lbl_ref[0, d:d + 1, :]
        l1 = lbl_ref[1, d:d + 1, :]
        m = jnp.maximum(l0, l1)
        e0 = jnp.exp(l0 - m)
        lb = e0 / (e0 + jnp.exp(l1 - m))
        return jnp.log2(lb + (1.0 - lb) * _sigmoid(z))

    for half, (ctx_ref, x_ref, mod_ref) in enumerate(((ctx_a, x_a, mod_a), (ctx_b, x_b, mod_b))):
        t = (2 * pl.program_id(0) + half) % nt
        x = jnp.where(t == 0, ctx_ref[0], x_ref[0])
        h = _modulated_norm(x, ng_ref[...], mod_ref[0, 0, 0:1, :], mod_ref[0, 0, 1:2, :]).astype(BF16)
        rows = slice(half * TOK_TILE, (half + 1) * TOK_TILE)

        def seg(j):
            return jnp.dot(h, w_ref[:, j * di:(j + 1) * di], preferred_element_type=F32)

        gf_ref[rows, :] = log_forget(seg(1), 0).astype(BF16)
        gb_ref[rows, :] = log_forget(seg(2), 1).astype(BF16)
        q_ref[rows, :] = _silu(seg(0)).astype(BF16)
        gate_ref[rows, :] = _silu(seg(4)).astype(BF16)
        v_ref[rows, :] = seg(3).astype(BF16)


def _hg_proj(ctx, x, mod, norm_g, w_in, lb_logits):
    b, l, d = x.shape
    di = w_in.shape[1] // 5
    nt = l // TOK_TILE + 1
    tot = l + TOK_TILE
    assert (b * nt) % 2 == 0
    tile_of = lambda m, half: 2 * m + half

    def half_specs(half):
        return [pl.BlockSpec((1, TOK_TILE, d), lambda m: (tile_of(m, half) // nt, 0, 0)),
                pl.BlockSpec((1, TOK_TILE, d),
                             lambda m: (tile_of(m, half) // nt, jnp.maximum(tile_of(m, half) % nt - 1, 0), 0)),
                pl.BlockSpec((1, 1, 2, d),
                             lambda m: (tile_of(m, half) // nt, jnp.minimum(tile_of(m, half) % nt, 1), 0, 0))]

    outs = pl.pallas_call(
        functools.partial(_hg_proj_kernel, nt=nt),
        grid=(b * nt // 2,),
        in_specs=half_specs(0) + half_specs(1) + [
            pl.BlockSpec((1, d), lambda m: (0, 0)),
            _whole_vmem(),
            pl.BlockSpec((2, 2, di), lambda m: (0, 0, 0))],
        out_specs=[pl.BlockSpec((2 * TOK_TILE, di), lambda m: (m, 0))] * 5,
        out_shape=[jax.ShapeDtypeStruct((b * tot, di), BF16)] * 5,
        compiler_params=_params(("arbitrary",)),
        name="hg_proj",
    )(ctx, x, mod, ctx, x, mod, norm_g, w_in, lb_logits)
    return [o.reshape(b, tot, di) for o in outs]


def _hg_scan_kernel(qf_ref, gf_ref, vf_ref, qb_ref, gb_ref, vb_ref, of_ref, ob_ref,
                    s_ref, bf_ref, bb_ref, tri_ref):
    t = pl.program_id(2)
    c = SCAN_CHUNK
    n_chunk = TOK_TILE // c
    w = SCAN_HEADS * HEAD_W

    @pl.when((pl.program_id(0) == 0) & (pl.program_id(1) == 0) & (t == 0))
    def _():
        row = lax.broadcasted_iota(jnp.int32, (TOK_TILE, TOK_TILE), 0)
        col = lax.broadcasted_iota(jnp.int32, (TOK_TILE, TOK_TILE), 1)
        same = (row // c) == (col // c)
        tri_ref[0] = jnp.where(same & (col <= row), 1.0, 0.0).astype(BF16)
        tri_ref[1] = jnp.where(same & (col >= row), 1.0, 0.0).astype(BF16)

    @pl.when(t == 0)
    def _():
        s_ref[...] = jnp.zeros_like(s_ref)

    def cumsum(tri, g):
        return jnp.dot(tri, g, preferred_element_type=F32)

    bf_ref[...] = cumsum(tri_ref[0], gf_ref[0])
    bb_ref[...] = cumsum(tri_ref[1], gb_ref[0])

    def chunk_rows(b_ref, row):
        return jnp.concatenate([b_ref[j * c + row:j * c + row + 1, :] for j in range(n_chunk)], axis=0)

    def chunk_decay_columns(total):
        return jnp.exp2(jnp.concatenate([total, jnp.zeros((8 - n_chunk, w), F32)], axis=0).T)

    mid_f, mid_b = c // 2 - 1, c // 2
    tot_f, tot_b = chunk_rows(bf_ref, c - 1), chunk_rows(bb_ref, 0)
    ref_f, ref_b = chunk_rows(bf_ref, mid_f), chunk_rows(bb_ref, mid_b)
    dec_f = chunk_decay_columns(tot_f)
    dec_b = chunk_decay_columns(tot_b)

    ri = lax.broadcasted_iota(jnp.int32, (c, c), 0)
    ci = lax.broadcasted_iota(jnp.int32, (c, c), 1)
    keep_f = ci <= ri
    keep_b = ci >= ri

    def chunk_head(direction, j, hh, q_ref, g_ref, v_ref, b_ref, o_ref, dec, keep, mid_row, last_row):
        rows = slice(j * c, (j + 1) * c)
        lanes = slice(hh * HEAD_W, (hh + 1) * HEAD_W)
        q = q_ref[0, rows, lanes]
        g = g_ref[0, rows, lanes].astype(F32)
        v = v_ref[0, rows, lanes]
        bcum = b_ref[rows, lanes]
        r = bcum[mid_row:mid_row + 1, :]
        b_last = bcum[last_row:last_row + 1, :]
        d = bcum - r
        q_mid = q * jnp.exp2(d).astype(BF16)
        k_mid = ((1.0 - jnp.exp2(g)) * jnp.exp2(-d)).astype(BF16)
        q_dec = q_mid * jnp.exp2(r).astype(BF16)
        k_dec = k_mid * jnp.exp2(b_last - r).astype(BF16)
        a = lax.dot_general(q_mid, k_mid, NT, preferred_element_type=F32)
        a = jnp.where(keep, a, 0.0).astype(BF16)
        s = s_ref[direction, hh]
        o_ref[0, rows, lanes] = jnp.dot(jnp.concatenate([q_dec, a], axis=1),
                                        jnp.concatenate([s.astype(BF16), v], axis=0),
                                        preferred_element_type=F32).astype(o_ref.dtype)
        s_ref[direction, hh] = s * dec[lanes, j:j + 1] + lax.dot_general(
            k_dec, v, TN, preferred_element_type=F32)

    reach = jnp.minimum(jnp.minimum(ref_f, tot_f - ref_f), jnp.minimum(ref_b, tot_b - ref_b))
    mild = jnp.min(reach) > -MAX_HALF_CHUNK_LOG2_DECAY

    @pl.when(mild)
    def _():
        for i in range(n_chunk):
            for hh in range(SCAN_HEADS):
                chunk_head(0, i, hh, qf_ref, gf_ref, vf_ref, bf_ref, of_ref, dec_f, keep_f, mid_f, c - 1)
                chunk_head(1, n_chunk - 1 - i, hh, qb_ref, gb_ref, vb_ref, bb_ref, ob_ref, dec_b, keep_b, mid_b, 0)

    @pl.when(jnp.logical_not(mild))
    def _():
        _hg_scan_exact_tile((qf_ref, gf_ref, vf_ref, of_ref), (qb_ref, gb_ref, vb_ref, ob_ref), s_ref)


def _hg_scan_exact_tile(fwd_refs, bwd_refs, s_ref):
    sub = SCAN_EXACT_CHUNK
    n_sub = TOK_TILE // sub
    ri = lax.broadcasted_iota(jnp.int32, (sub, sub), 0)
    ci = lax.broadcasted_iota(jnp.int32, (sub, sub), 1)
    tri = (jnp.where(ci <= ri, 1.0, 0.0).astype(BF16), jnp.where(ci >= ri, 1.0, 0.0).astype(BF16))
    jcol = lax.broadcasted_iota(jnp.int32, (sub, 1), 0)

    def head_body(hh, carry):
        lanes = pl.ds(pl.multiple_of(hh * HEAD_W, HEAD_W), HEAD_W)

        def sub_body(step, carry):
            for direction, (q_ref, g_ref, v_ref, o_ref) in enumerate((fwd_refs, bwd_refs)):
                fwd = direction == 0
                cc = step if fwd else n_sub - 1 - step
                rows = pl.ds(pl.multiple_of(cc * sub, sub), sub)
                q = q_ref[0, rows, lanes].astype(F32)
                g = g_ref[0, rows, lanes]
                v = v_ref[0, rows, lanes]
                b = jnp.dot(tri[direction], g, preferred_element_type=F32)
                k = 1.0 - jnp.exp2(g.astype(F32))
                b_last = b[sub - 1:sub, :] if fwd else b[0:1, :]
                s = s_ref[direction, hh]
                o = jnp.dot((q * jnp.exp2(b)).astype(BF16), s.astype(BF16), preferred_element_type=F32)
                vf = v.astype(F32)
                intra = []
                for i in range(sub):
                    e = jnp.exp2(jnp.minimum(b[i:i + 1, :] - b, 0.0))
                    a = jnp.sum(q[i:i + 1, :] * e * k, axis=1, keepdims=True)
                    a = jnp.where((jcol <= i) if fwd else (jcol >= i), a, 0.0)
                    intra.append(jnp.sum(a * vf, axis=0, keepdims=True))
                o_ref[0, rows, lanes] = (o + jnp.concatenate(intra, axis=0)).astype(o_ref.dtype)
                k_dec = (k * jnp.exp2(b_last - b)).astype(BF16)
                dec_col = jnp.broadcast_to(jnp.exp2(b_last), (8, HEAD_W)).T[:, 0:1]
                s_ref[direction, hh] = s * dec_col + lax.dot_general(k_dec, v, TN, preferred_element_type=F32)
            return carry

        return lax.fori_loop(0, n_sub, sub_body, carry)

    lax.fori_loop(0, SCAN_HEADS, head_body, 0)


def _hg_scan(q, gf, gb, v):
    b, tot, di = q.shape
    nt = tot // TOK_TILE
    w = SCAN_HEADS * HEAD_W
    fwd = pl.BlockSpec((1, TOK_TILE, w), lambda i, h, t: (i, t, h))
    bwd = pl.BlockSpec((1, TOK_TILE, w), lambda i, h, t: (i, jnp.where(t == 0, 0, nt - t), h))
    out = jax.ShapeDtypeStruct((b, tot, di), BF16)
    return pl.pallas_call(
        _hg_scan_kernel,
        grid=(b, di // w, nt),
        in_specs=[fwd, fwd, fwd, bwd, bwd, bwd],
        out_specs=[fwd, bwd],
        out_shape=[out, out],
        scratch_shapes=[pltpu.VMEM((2, SCAN_HEADS, HEAD_W, HEAD_W), F32),
                        pltpu.VMEM((TOK_TILE, w), F32),
                        pltpu.VMEM((TOK_TILE, w), F32),
                        pltpu.VMEM((2, TOK_TILE, TOK_TILE), BF16)],
        compiler_params=_params(("arbitrary", "arbitrary", "arbitrary")),
        name="hg_scan",
    )(q, gf, v, q, gb, v)


def _rope_lane_tables(l_lat):
    ax = DA_DQK // 2
    inv = 1.0 / (ROPE_THETA ** (np.arange(0, ax, 2, dtype=np.float64) / ax))
    pos = np.arange(l_lat)
    ang_r = (pos // GRID_W)[:, None] * inv
    ang_c = (pos % GRID_W)[:, None] * inv
    zero = np.zeros_like(ang_r)
    cos64 = np.concatenate([np.cos(ang_r), np.cos(ang_r), np.cos(ang_c), np.cos(ang_c)], axis=1)
    up64 = np.concatenate([-np.sin(ang_r), zero, -np.sin(ang_c), zero], axis=1)
    dn64 = np.concatenate([zero, np.sin(ang_r), zero, np.sin(ang_c)], axis=1)
    lat = [np.concatenate([t, t], axis=1) for t in (cos64, up64, dn64)]
    ident = [np.ones((TOK_TILE, HEAD_W)), np.zeros((TOK_TILE, HEAD_W)), np.zeros((TOK_TILE, HEAD_W))]
    return [np.concatenate([i, t], axis=0).astype(np.float32) for i, t in zip(ident, lat)], (cos64, up64, dn64)


def _mid_kernel(of_ref, ob_ref, gate0_ref, ctx_ref, x_ref, gm_ref, hng_ref, wo_ref,
                mod_ref, ng_ref, wq_ref, wk_ref, wv_ref, wg_ref,
                kc_ref, ku_ref, kd_ref, qc_ref, qs_ref,
                x1_ref, k_ref, vt_ref, qt_ref, gate_ref, y_ref):
    t = pl.program_id(1)
    n_head = k_ref.shape[-1] // HEAD_W

    for hh in range(n_head):
        lanes = slice(hh * HEAD_W, (hh + 1) * HEAD_W)
        o = of_ref[0, :, lanes].astype(F32) + ob_ref[0, :, lanes].astype(F32)
        o = o * lax.rsqrt(jnp.mean(o * o, axis=-1, keepdims=True) + EPS) * hng_ref[:, lanes]
        y_ref[:, lanes] = (o * gate0_ref[0, :, lanes].astype(F32)).astype(BF16)
    y = jnp.dot(y_ref[...], wo_ref[...], preferred_element_type=F32)
    x1 = jnp.where(t == 0, ctx_ref[0], x_ref[0]) + gm_ref[0, 0] * y

    h = _modulated_norm(x1, ng_ref[...], mod_ref[0, 0, 0:1, :], mod_ref[0, 0, 1:2, :]).astype(BF16)
    k = jnp.dot(h, wk_ref[...], preferred_element_type=F32)
    cos, s_up, s_dn = kc_ref[...], ku_ref[...], kd_ref[...]
    for hh in range(n_head):
        lanes = slice(hh * HEAD_W, (hh + 1) * HEAD_W)
        kh = k[:, lanes]
        rot = (kh * cos + pltpu.roll(kh, HEAD_W - 16, axis=1) * s_up
               + pltpu.roll(kh, 16, axis=1) * s_dn)
        k_ref[0, :, lanes] = rot.astype(BF16)

    vt_ref[0] = lax.dot_general(wv_ref[...], h, NT, preferred_element_type=F32).astype(BF16)

    @pl.when(t > 0)
    def _():
        x1_ref[0] = x1
        gate_ref[0] = _silu(jnp.dot(h, wg_ref[...], preferred_element_type=F32)).astype(BF16)
        qt = lax.dot_general(wq_ref[...], h, NT, preferred_element_type=F32)
        qc, qs = qc_ref[...], qs_ref[...]
        for grp in range(qt.shape[0] // DA_DQK):
            x = qt[grp * DA_DQK:(grp + 1) * DA_DQK]
            partner = jnp.concatenate([x[16:32], x[0:16], x[48:64], x[32:48]], axis=0)
            qt_ref[0, grp * DA_DQK:(grp + 1) * DA_DQK, :] = (x * qc + partner * qs).astype(BF16)


def _mid(of, ob, gate0, ctx, x, gate_mod, hg_norm_g, w_out, mod, norm_g, w_in):
    b, tot, di = of.shape
    l_lat, d = x.shape[1], x.shape[2]
    nt = tot // TOK_TILE
    wq, wk, wv, wg = (w_in[:, j * di:(j + 1) * di] for j in range(4))
    ktabs, (cos64, up64, dn64) = _rope_lane_tables(l_lat)
    q_scale = DA_DQK ** -0.5 * LOG2E
    qc = (cos64.T * q_scale).astype(np.float32)
    qs = ((up64 + dn64).T * q_scale).astype(np.float32)
    lat_t = lambda t: jnp.maximum(t - 1, 0)
    tokw = pl.BlockSpec((1, TOK_TILE, di), lambda i, t: (i, t, 0))
    ktab = pl.BlockSpec((TOK_TILE, HEAD_W), lambda i, t: (t, 0))
    qtab = pl.BlockSpec((DA_DQK, TOK_TILE), lambda i, t: (0, lat_t(t)))
    return pl.pallas_call(
        _mid_kernel,
        grid=(b, nt),
        in_specs=[tokw, tokw, tokw,
                  pl.BlockSpec((1, TOK_TILE, d), lambda i, t: (i, 0, 0)),
                  pl.BlockSpec((1, TOK_TILE, d), lambda i, t: (i, lat_t(t), 0)),
                  pl.BlockSpec((1, 1, 1, d), lambda i, t: (i, jnp.minimum(t, 1), 0, 0)),
                  pl.BlockSpec((1, di), lambda i, t: (0, 0)),
                  _whole_vmem(),
                  pl.BlockSpec((1, 1, 2, d), lambda i, t: (i, jnp.minimum(t, 1), 0, 0)),
                  pl.BlockSpec((1, d), lambda i, t: (0, 0)),
                  _whole_vmem(), _whole_vmem(), _whole_vmem(), _whole_vmem(),
                  ktab, ktab, ktab, qtab, qtab],
        out_specs=[pl.BlockSpec((1, TOK_TILE, d), lambda i, t: (i, lat_t(t), 0)),
                   pl.BlockSpec((1, TOK_TILE, di), lambda i, t: (i, t, 0)),
                   pl.BlockSpec((1, di, TOK_TILE), lambda i, t: (i, 0, t)),
                   pl.BlockSpec((1, di, TOK_TILE), lambda i, t: (i, 0, lat_t(t))),
                   pl.BlockSpec((1, TOK_TILE, di), lambda i, t: (i, lat_t(t), 0))],
        out_shape=[jax.ShapeDtypeStruct((b, l_lat, d), F32),
                   jax.ShapeDtypeStruct((b, tot, di), BF16),
                   jax.ShapeDtypeStruct((b, di, tot), BF16),
                   jax.ShapeDtypeStruct((b, di, l_lat), BF16),
                   jax.ShapeDtypeStruct((b, l_lat, di), BF16)],
        scratch_shapes=[pltpu.VMEM((TOK_TILE, di), BF16)],
        compiler_params=_params(("arbitrary", "arbitrary")),
        name="hg_out_da_proj",
    )(of, ob, gate0, ctx, x, gate_mod, hg_norm_g, w_out, mod, norm_g,
      wq.T.astype(BF16), wk.astype(BF16), wv.T.astype(BF16), wg.astype(BF16),
      *[jnp.asarray(a) for a in ktabs], jnp.asarray(qc), jnp.asarray(qs))


def _da_attn_kernel(k_ref, vt_ref, qt_ref, lam_ref, sg_ref, o_ref, pa_ref, pb_ref, a_ref, la_ref, lb_ref,
                    ota_ref, otb_ref, *, lambda_init):
    lq1, lk1, lq2, lk2 = (lam_ref[i:i + 1, :] for i in range(4))
    lam = (jnp.exp(jnp.sum(lq1 * lk1, axis=-1, keepdims=True))
           - jnp.exp(jnp.sum(lq2 * lk2, axis=-1, keepdims=True)) + lambda_init)
    out_gain = sg_ref[...] * (1.0 - lambda_init)
    n_chunk = k_ref.shape[1] // KEY_CHUNK
    n_q = qt_ref.shape[-1] // Q_TILE
    w2 = 2 * Q_TILE
    half0 = lax.broadcasted_iota(jnp.int32, (HEAD_W, Q_TILE), 0) < DA_DQK

    def head(hh):
        return slice(hh * HEAD_W, (hh + 1) * HEAD_W)

    def key_norm_bound(hh):
        kf = k_ref[0, :, head(hh)].astype(F32)
        sel_r = lax.broadcasted_iota(jnp.int32, (HEAD_W, HEAD_W), 0) // DA_DQK
        sel_c = lax.broadcasted_iota(jnp.int32, (HEAD_W, HEAD_W), 1)
        sel = jnp.where(sel_r == sel_c, 1.0, 0.0).astype(BF16)
        norm2 = jnp.dot((kf * kf).astype(BF16), sel, preferred_element_type=F32)
        kmax = jnp.sqrt(jnp.max(norm2, axis=0, keepdims=True)) * BOUND_SLACK
        return jnp.concatenate([jnp.broadcast_to(kmax[:, 0:1], (1, Q_TILE)),
                                jnp.broadcast_to(kmax[:, 1:2], (1, Q_TILE))], axis=1)

    kmax = [key_norm_bound(hh) for hh in range(ATTN_HEADS)]

    def fold(x):
        return x.reshape(KEY_CHUNK // 8, 8, w2)

    def exp_stage(hh, i, p_ref, l_ref, exact):
        start = i * Q_TILE if isinstance(i, int) else pl.multiple_of(i * Q_TILE, Q_TILE)
        qt = qt_ref[0, head(hh), pl.ds(start, Q_TILE)]
        zero = jnp.zeros_like(qt)
        rhs = jnp.concatenate([jnp.where(half0, qt, zero), jnp.where(half0, zero, qt)], axis=1)

        def scores(c):
            return jnp.dot(k_ref[0, c * KEY_CHUNK:(c + 1) * KEY_CHUNK, head(hh)], rhs,
                           preferred_element_type=F32)

        if exact:
            mx = jnp.full((8, w2), -jnp.inf, F32)
            for c in range(n_chunk):
                mx = jnp.maximum(mx, jnp.max(fold(scores(c)), axis=0))
            m = jnp.max(mx, axis=0, keepdims=True)
        else:
            rf = rhs.astype(F32)
            m = jnp.sqrt(jnp.sum(rf * rf, axis=0, keepdims=True)) * kmax[hh]
        l8 = jnp.zeros((8, w2), F32)
        for c in range(n_chunk):
            p = jnp.exp2(scores(c) - m)
            l8 = l8 + jnp.sum(fold(p), axis=0)
            p_ref[c * KEY_CHUNK:(c + 1) * KEY_CHUNK, :] = p.astype(BF16)
        l_ref[...] = l8
        return jnp.where(l8 > MIN_COLUMN_SUM, 0.0, 1.0)

    def value_stage(hh, p_ref, l_ref, ot_ref):
        l = jnp.sum(l_ref[...], axis=0, keepdims=True)
        l0, l1 = l[:, :Q_TILE], l[:, Q_TILE:]
        rho = (lam * l0 / l1).astype(BF16)
        for c in range(n_chunk):
            rows = slice(c * KEY_CHUNK, (c + 1) * KEY_CHUNK)
            a_ref[rows, :] = p_ref[rows, :Q_TILE] - p_ref[rows, Q_TILE:] * rho
        ot_ref[...] = jnp.dot(vt_ref[0, head(hh), :], a_ref[...], preferred_element_type=F32) * (1.0 / l0)

    def store_stage(hh, i, ot_ref):
        start = i * Q_TILE if isinstance(i, int) else pl.multiple_of(i * Q_TILE, Q_TILE)
        ot = ot_ref[...]
        ot = ot * lax.rsqrt(jnp.mean(ot * ot, axis=0, keepdims=True) + EPS) * out_gain
        o_ref[0, pl.ds(start, Q_TILE), head(hh)] = ot.T.astype(o_ref.dtype)

    even = (pa_ref, la_ref, ota_ref)
    odd = (pb_ref, lb_ref, otb_ref)
    last = n_q - 1
    bad = jnp.zeros((8, w2), F32)
    for hh in range(ATTN_HEADS):
        prev = hh - 1
        bad = jnp.maximum(bad, exp_stage(hh, 0, *even[:2], exact=False))
        if hh > 0:
            value_stage(prev, *odd)
            store_stage(prev, last - 1, even[2])
        bad = jnp.maximum(bad, exp_stage(hh, 1, *odd[:2], exact=False))
        value_stage(hh, *even)
        if hh > 0:
            store_stage(prev, last, odd[2])

        def body(j, bad, hh=hh):
            s = 2 * j + 2
            bad = jnp.maximum(bad, exp_stage(hh, s, *even[:2], exact=False))
            value_stage(hh, *odd)
            store_stage(hh, s - 2, even[2])
            bad = jnp.maximum(bad, exp_stage(hh, s + 1, *odd[:2], exact=False))
            value_stage(hh, *even)
            store_stage(hh, s - 1, odd[2])
            return bad

        bad = lax.fori_loop(0, n_q // 2 - 1, body, bad)
    value_stage(ATTN_HEADS - 1, *odd)
    store_stage(ATTN_HEADS - 1, last - 1, even[2])
    store_stage(ATTN_HEADS - 1, last, odd[2])

    @pl.when(jnp.max(bad) > 0.0)
    def _():
        for hh in range(ATTN_HEADS):
            def redo(i, carry, hh=hh):
                exp_stage(hh, i, *even[:2], exact=True)
                value_stage(hh, *even)
                store_stage(hh, i, even[2])
                return carry

            lax.fori_loop(0, n_q, redo, 0)


def _da_attn(k, vt, qt, lam_vecs, subln_g, lambda_init):
    b, tot, di = k.shape
    l_lat = qt.shape[-1]
    w = ATTN_HEADS * HEAD_W
    assert (l_lat // Q_TILE) % 2 == 0 and di % w == 0
    return pl.pallas_call(
        functools.partial(_da_attn_kernel, lambda_init=lambda_init),
        grid=(b, di // w),
        in_specs=[pl.BlockSpec((1, tot, w), lambda i, h: (i, 0, h)),
                  pl.BlockSpec((1, w, tot), lambda i, h: (i, h, 0)),
                  pl.BlockSpec((1, w, l_lat), lambda i, h: (i, h, 0)),
                  pl.BlockSpec((4, DA_DQK), lambda i, h: (0, 0)),
                  pl.BlockSpec((HEAD_W, 1), lambda i, h: (0, 0))],
        out_specs=pl.BlockSpec((1, l_lat, w), lambda i, h: (i, 0, h)),
        out_shape=jax.ShapeDtypeStruct((b, l_lat, di), BF16),
        scratch_shapes=[pltpu.VMEM((tot, 2 * Q_TILE), BF16),
                        pltpu.VMEM((tot, 2 * Q_TILE), BF16),
                        pltpu.VMEM((tot, Q_TILE), BF16),
                        pltpu.VMEM((8, 2 * Q_TILE), F32),
                        pltpu.VMEM((8, 2 * Q_TILE), F32),
                        pltpu.VMEM((HEAD_W, Q_TILE), F32),
                        pltpu.VMEM((HEAD_W, Q_TILE), F32)],
        compiler_params=_params(("arbitrary", "arbitrary")),
        name="da_attn",
    )(k, vt, qt, lam_vecs, subln_g.reshape(HEAD_W, 1))


def _da_out_kernel(o_ref, gate_ref, x_ref, gm_ref, fg_ref, w_ref, out_ref):
    y = o_ref[0] * gate_ref[0]
    x = x_ref[0] + gm_ref[0] * jnp.dot(y, w_ref[...], preferred_element_type=F32)
    out_ref[0] = x * lax.rsqrt(jnp.mean(x * x, axis=-1, keepdims=True) + EPS) * fg_ref[...]


def _da_out(o, gate, x1, gate_mod, final_g, w_out):
    b, l_lat, di = o.shape
    d = x1.shape[-1]
    tile = math.gcd(l_lat, OUT_TILE)
    tokw = pl.BlockSpec((1, tile, di), lambda i, t: (i, t, 0))
    return pl.pallas_call(
        _da_out_kernel,
        grid=(b, l_lat // tile),
        in_specs=[tokw, tokw,
                  pl.BlockSpec((1, tile, d), lambda i, t: (i, t, 0)),
                  pl.BlockSpec((1, 1, d), lambda i, t: (i, 0, 0)),
                  pl.BlockSpec((1, d), lambda i, t: (0, 0)),
                  _whole_vmem()],
        out_specs=pl.BlockSpec((1, tile, d), lambda i, t: (i, t, 0)),
        out_shape=jax.ShapeDtypeStruct((b, l_lat, d), F32),
        compiler_params=_params(("arbitrary", "arbitrary")),
        name="da_out",
    )(o, gate, x1, gate_mod, final_g, w_out)


def kernel(x, c, ctx, c_ctx, w_ada, b_ada, norm_g, hg_w_in, hg_lb_logits, hg_norm_g, hg_w_out,
           da_w_in, da_lam_q1, da_lam_k1, da_lam_q2, da_lam_k2, da_subln_g, da_w_out, final_g):
    b, l_lat, d = x.shape
    assert ctx.shape[1] == TOK_TILE and l_lat % TOK_TILE == 0 and l_lat % GRID_W == 0
    assert w_ada.shape[0] == 2 and hg_w_in.shape[0] == 1 and da_w_in.shape[0] == 1
    assert hg_lb_logits.shape[0] == 2

    rows = -(-(b + 1) // 8) * 8
    cc = jnp.concatenate([c, c_ctx[None], jnp.zeros((rows - b - 1, d), F32)], axis=0)
    ada = _adaln(cc, w_ada, b_ada)

    def mods(layer):
        m = ada[layer].reshape(rows, 3, d)
        pair = jnp.stack([jnp.broadcast_to(m[b], (b, 3, d)), m[:b]], axis=1)
        return pair[:, :, 0:2], pair[:, :, 2:3]

    mod0, gmod0 = mods(0)
    mod1, gmod1 = mods(1)

    q, gf, gb, v, gate0 = _hg_proj(ctx, x, mod0, norm_g[0:1], hg_w_in[0].astype(BF16), hg_lb_logits)
    of, ob = _hg_scan(q, gf, gb, v)
    x1, k, vt, qt, gate1 = _mid(of, ob, gate0, ctx, x, gmod0, hg_norm_g, hg_w_out[0].astype(BF16),
                                mod1, norm_g[1:2], da_w_in[0])

    lambda_init = 0.8 - 0.6 * math.exp(-0.3 * 1)
    lam_vecs = jnp.concatenate([da_lam_q1, da_lam_k1, da_lam_q2, da_lam_k2], axis=0)
    o = _da_attn(k, vt, qt, lam_vecs, da_subln_g[0], lambda_init)
    return _da_out(o, gate1, x1, gmod1[:, 1], final_g[None], da_w_out[0].astype(BF16))
```
